```python
import math
import jax, jax.numpy as jnp
from jax import lax
import numpy as np

D_MODEL = 1024
BATCH = 16
SEQ = 2048
DEPTH = 2

HEAD_DIM = 64
N_SB_HEADS = 4
N_FOX_HEADS = 4
N_DIL_HEADS = 4
N_LRU_BLOCKS = 4
LRU_BLOCK = HEAD_DIM
LRU_WIDTH = N_LRU_BLOCKS * LRU_BLOCK
SB_W = N_SB_HEADS * HEAD_DIM
FOX_W = N_FOX_HEADS * HEAD_DIM
DIL_W = N_DIL_HEADS * HEAD_DIM
MIX_WIDTH = SB_W + FOX_W + DIL_W + LRU_WIDTH
N_IN = 3 * SB_W + 3 * FOX_W + N_FOX_HEADS + 3 * DIL_W + 2 * LRU_WIDTH
BLOCK = 128
DILATED_PATTERNS = ((128, 1), (512, 4), (2048, 16))
LRU_CONV_WIDTH = 4
LRU_C = 8.0
FFN_CONV_WIDTH = 3
D_FF = 2816
N_MEM = 256
N_CROSS_HEADS = 4
CROSS_W = N_CROSS_HEADS * HEAD_DIM
NUM_BUCKETS = 32
MAX_DISTANCE = 2048
EPS = 1e-6

kernel_name = "hymba_style_sb_fox_dilated_rglru_trunk"


def rms_norm(x, g):
    x32 = x.astype(jnp.float32)
    y = x32 * lax.rsqrt(jnp.mean(x32 * x32, axis=-1, keepdims=True) + EPS)
    return (y * g.astype(jnp.float32)).astype(x.dtype)


def causal_dwconv(x, w, b):
    k_w = w.shape[0]
    s = x.shape[1]
    xp = jnp.pad(x, ((0, 0), (k_w - 1, 0), (0, 0)))
    y = b
    for j in range(k_w):
        y = y + w[j] * xp[:, j:j + s]
    return y


def t5_bucket(dist):
    n = jnp.maximum(dist, 0)
    max_exact = NUM_BUCKETS // 2
    nf = jnp.maximum(n, 1).astype(jnp.float32)
    large = max_exact + (jnp.log(nf / max_exact) / math.log(MAX_DISTANCE / max_exact)
                         * (NUM_BUCKETS - max_exact)).astype(jnp.int32)
    large = jnp.minimum(large, NUM_BUCKETS - 1)
    return jnp.where(n < max_exact, n, large)


def split_cols(t, sizes):
    out, start = [], 0
    for n in sizes:
        out.append(t[..., start:start + n])
        start += n
    return out


def split_qkv(t, n_heads):
    b, s, _ = t.shape
    t = t.reshape(b, s, 3, n_heads, HEAD_DIM).transpose(2, 0, 3, 1, 4)
    return t[0] * HEAD_DIM ** -0.5, t[1], t[2]


def merge_heads(o):
    b, h, s, d = o.shape
    return o.transpose(0, 2, 1, 3).reshape(b, s, h * d)


def stick_breaking_attention(q, k, v):
    s_len = q.shape[2]
    outs = []
    for i in range(s_len // BLOCK):
        q0, q1 = i * BLOCK, (i + 1) * BLOCK
        z = jnp.einsum('bhqd,bhkd->bhqk', q[:, :, q0:q1], k[:, :, :q1]).astype(jnp.float32)
        strict = jnp.arange(q1)[None, :] < jnp.arange(q0, q1)[:, None]
        log_keep = jnp.where(strict, jax.nn.log_sigmoid(-z), 0.0)
        log_keep_after = lax.cumsum(log_keep, axis=3, reverse=True) - log_keep
        att = jnp.where(strict, jnp.exp(jax.nn.log_sigmoid(z) + log_keep_after), 0.0)
        outs.append(jnp.einsum('bhqk,bhkd->bhqd', att, v[:, :, :q1].astype(jnp.float32)))
    return jnp.concatenate(outs, axis=2)


def forgetting_attention(q, k, v, f_logit):
    s_len = q.shape[2]
    log_f = jax.nn.log_sigmoid(f_logit.astype(jnp.float32))
    cum = jnp.cumsum(log_f, axis=1).transpose(0, 2, 1)
    outs = []
    for i in range(s_len // BLOCK):
        q0, q1 = i * BLOCK, (i + 1) * BLOCK
        z = jnp.einsum('bhqd,bhkd->bhqk', q[:, :, q0:q1], k[:, :, :q1]).astype(jnp.float32)
        z = z + cum[:, :, q0:q1, None] - cum[:, :, None, :q1]
        causal = jnp.arange(q1)[None, :] <= jnp.arange(q0, q1)[:, None]
        p = jax.nn.softmax(jnp.where(causal, z, -jnp.inf), axis=-1)
        outs.append(jnp.einsum('bhqk,bhkd->bhqd', p, v[:, :, :q1].astype(jnp.float32)))
    return jnp.concatenate(outs, axis=2)


def dilated_branch(q, k, v, rel_bias, dil, steps):
    b, h, s, hd = q.shape
    L = s // dil
    qb_len = math.gcd(BLOCK, L)
    nb = L // qb_len

    def to_cls(t):
        return t.reshape(b, h, L, dil, hd).transpose(0, 1, 3, 2, 4)

    qc = to_cls(q).reshape(b, h, dil, nb, qb_len, hd)
    pad = ((0, 0), (0, 0), (0, 0), (steps, 0), (0, 0))
    kc = jnp.pad(to_cls(k), pad)
    vc = jnp.pad(to_cls(v), pad)
    idx = (jnp.arange(nb) * qb_len)[:, None] + jnp.arange(qb_len + steps)[None, :]
    kb = kc[:, :, :, idx]
    vb = vc[:, :, :, idx]
    sc = jnp.einsum('bhrnqd,bhrnkd->bhrnqk', qc, kb).astype(jnp.float32)
    qpos = (jnp.arange(nb) * qb_len)[:, None] + jnp.arange(qb_len)[None, :]
    kpos = idx - steps
    delta = qpos[:, :, None] - kpos[:, None, :]
    valid = (delta >= 0) & (delta <= steps) & (kpos[:, None, :] >= 0)
    bias = jnp.moveaxis(rel_bias.astype(jnp.float32)[t5_bucket(delta * dil)], -1, 0)[:, None]
    sc = jnp.where(valid, sc + bias, -jnp.inf)
    m = jnp.max(sc, axis=-1, keepdims=True)
    p = jnp.exp(sc - m)
    l = jnp.sum(p, axis=-1)
    o = jnp.einsum('bhrnqk,bhrnkd->bhrnqd', p, vb.astype(jnp.float32)) / l[..., None]
    lse = m[..., 0] + jnp.log(l)
    o = o.reshape(b, h, dil, L, hd).transpose(0, 1, 3, 2, 4).reshape(b, h, s, hd)
    lse = lse.reshape(b, h, dil, L).transpose(0, 1, 3, 2).reshape(b, h, s)
    return o, lse


def dilated_attention(q, k, v, rel_bias):
    outs, lses = [], []
    for window, dil in DILATED_PATTERNS:
        o, lse = dilated_branch(q, k, v, rel_bias, dil, window // dil)
        outs.append(o)
        lses.append(lse)
    wts = jax.nn.softmax(jnp.stack(lses), axis=0)
    return jnp.sum(wts[..., None] * jnp.stack(outs), axis=0)


def rg_lru_branch(x_br, gate_br, conv_w, conv_b, w_a, b_a, w_x, b_x, lam):
    b, s, c = x_br.shape
    f32 = jnp.float32
    xc = causal_dwconv(x_br.astype(f32), conv_w.astype(f32), conv_b.astype(f32))
    xg = xc.reshape(b, s, N_LRU_BLOCKS, LRU_BLOCK)
    r = jax.nn.sigmoid(jnp.einsum('bsgi,gij->bsgj', xg, w_a.astype(f32)).reshape(b, s, c) + b_a.astype(f32))
    i_gate = jax.nn.sigmoid(jnp.einsum('bsgi,gij->bsgj', xg, w_x.astype(f32)).reshape(b, s, c) + b_x.astype(f32))
    log_a = -LRU_C * r * jax.nn.softplus(-lam.astype(f32))
    a = jnp.exp(log_a)
    u = jnp.sqrt(-jnp.expm1(2.0 * log_a)) * (i_gate * xc)

    def combine(e1, e2):
        a1, b1 = e1
        a2, b2 = e2
        return a1 * a2, a2 * b1 + b2

    _, h = lax.associative_scan(combine, (a, u), axis=1)
    return h * jax.nn.gelu(gate_br.astype(f32), approximate=False)


def memory_cross_attention(h, mem_n, w_cq, w_ck, w_cv, w_co):
    b, s, _ = h.shape
    m = mem_n.shape[1]
    q = (h @ w_cq).reshape(b, s, N_CROSS_HEADS, HEAD_DIM).transpose(0, 2, 1, 3) * HEAD_DIM ** -0.5
    k = (mem_n @ w_ck).reshape(b, m, N_CROSS_HEADS, HEAD_DIM).transpose(0, 2, 1, 3)
    v = (mem_n @ w_cv).reshape(b, m, N_CROSS_HEADS, HEAD_DIM).transpose(0, 2, 1, 3)
    p = jax.nn.softmax(jnp.einsum('bhqd,bhkd->bhqk', q, k).astype(jnp.float32), axis=-1)
    o = jnp.einsum('bhqk,bhkd->bhqd', p, v.astype(jnp.float32))
    return merge_heads(o).astype(h.dtype) @ w_co


def setup_inputs(seed: int = 0) -> dict:
    key = jax.random.key(seed)
    ks = iter(jax.random.split(key, 40))
    f32 = jnp.float32
    L = DEPTH

    def normal(shape, scale):
        return scale * jax.random.normal(next(ks), shape, f32)

    def gain(shape):
        return 1.0 + normal(shape, 0.02)

    u = jax.random.uniform(next(ks), (L, LRU_WIDTH), f32, 0.9, 0.999)
    a0 = u ** (1.0 / LRU_C)
    lru_lambda = jnp.log(a0) - jnp.log1p(-a0)
    return {
        "x": normal((BATCH, SEQ, D_MODEL), 1.0),
        "mem": normal((BATCH, N_MEM, D_MODEL), 1.0),
        "norm_mix_g": gain((L, D_MODEL)),
        "w_in": normal((L, D_MODEL, N_IN), D_MODEL ** -0.5),
        "b_forget": jax.random.uniform(next(ks), (L, N_FOX_HEADS), f32, 1.0, 4.0),
        "lru_conv_w": normal((L, LRU_CONV_WIDTH, LRU_WIDTH), LRU_CONV_WIDTH ** -0.5),
        "lru_conv_b": normal((L, LRU_WIDTH), 0.01),
        "lru_w_a": normal((L, N_LRU_BLOCKS, LRU_BLOCK, LRU_BLOCK), LRU_BLOCK ** -0.5),
        "lru_b_a": normal((L, LRU_WIDTH), 0.01),
        "lru_w_x": normal((L, N_LRU_BLOCKS, LRU_BLOCK, LRU_BLOCK), LRU_BLOCK ** -0.5),
        "lru_b_x": normal((L, LRU_WIDTH), 0.01),
        "lru_lambda": lru_lambda,
        "w_out": normal((L, MIX_WIDTH, D_MODEL), MIX_WIDTH ** -0.5),
        "norm_cross_g": gain((L, D_MODEL)),
        "norm_mem_g": gain((L, D_MODEL)),
        "w_cq": normal((L, D_MODEL, CROSS_W), D_MODEL ** -0.5),
        "w_ck": normal((L, D_MODEL, CROSS_W), D_MODEL ** -0.5),
        "w_cv": normal((L, D_MODEL, CROSS_W), D_MODEL ** -0.5),
        "w_co": normal((L, CROSS_W, D_MODEL), CROSS_W ** -0.5),
        "norm_ffn_g": gain((L, D_MODEL)),
        "w_up": normal((L, D_MODEL, 2 * D_FF), D_MODEL ** -0.5),
        "ffn_conv_w": normal((L, FFN_CONV_WIDTH, 2 * D_FF), FFN_CONV_WIDTH ** -0.5),
        "ffn_conv_b": normal((L, 2 * D_FF), 0.01),
        "w_down": normal((L, D_FF, D_MODEL), D_FF ** -0.5),
        "rel_bias": normal((NUM_BUCKETS, N_DIL_HEADS), 0.5),
        "final_norm_g": gain((D_MODEL,)),
    }


def reference(x, mem, norm_mix_g, w_in, b_forget, lru_conv_w, lru_conv_b, lru_w_a, lru_b_a,
              lru_w_x, lru_b_x, lru_lambda, w_out, norm_cross_g, norm_mem_g, w_cq, w_ck, w_cv,
              w_co, norm_ffn_g, w_up, ffn_conv_w, ffn_conv_b, w_down, rel_bias, final_norm_g):
    col_sizes = [3 * SB_W, 3 * FOX_W, N_FOX_HEADS, 3 * DIL_W, LRU_WIDTH, LRU_WIDTH]
    for l in range(DEPTH):
        h = rms_norm(x, norm_mix_g[l])
        proj = h @ w_in[l]
        sb_qkv, fox_qkv, fox_f, dil_qkv, lru_x, lru_gate = split_cols(proj, col_sizes)
        o_sb = stick_breaking_attention(*split_qkv(sb_qkv, N_SB_HEADS))
        o_fox = forgetting_attention(*split_qkv(fox_qkv, N_FOX_HEADS), fox_f + b_forget[l])
        o_dil = dilated_attention(*split_qkv(dil_qkv, N_DIL_HEADS), rel_bias)
        o_lru = rg_lru_branch(lru_x, lru_gate, lru_conv_w[l], lru_conv_b[l], lru_w_a[l],
                              lru_b_a[l], lru_w_x[l], lru_b_x[l], lru_lambda[l])
        mixed = jnp.concatenate([merge_heads(o_sb), merge_heads(o_fox), merge_heads(o_dil), o_lru],
                                axis=-1).astype(x.dtype)
        x = x + mixed @ w_out[l]
        x = x + memory_cross_attention(rms_norm(x, norm_cross_g[l]), rms_norm(mem, norm_mem_g[l]),
                                       w_cq[l], w_ck[l], w_cv[l], w_co[l])
        hf = rms_norm(x, norm_ffn_g[l]) @ w_up[l]
        hf = causal_dwconv(hf, ffn_conv_w[l], ffn_conv_b[l])
        up, gate = hf[..., :D_FF], hf[..., D_FF:]
        x = x + (jax.nn.gelu(gate, approximate=False) * up).astype(x.dtype) @ w_down[l]
    return rms_norm(x, final_norm_g)
```

```python
import functools
import math

import jax
import jax.numpy as jnp
from jax import lax
from jax.experimental import pallas as pl
from jax.experimental.pallas import tpu as pltpu

F32 = jnp.float32
BF16 = jnp.bfloat16

D_MODEL = 1024
HEAD_DIM = 64
N_HEADS = 4
MIX_W = N_HEADS * HEAD_DIM
D_FF = 2816
FF_CHUNK = 256
N_FF_CHUNKS = D_FF // FF_CHUNK
N_MEM = 256
NUM_BUCKETS = 32
MAX_DISTANCE = 2048
DILATIONS = (1, 4, 16)
DIL_STEPS = 128
LRU_C = 8.0
EPS = 1e-6
NEG_BIG = -1e30

TQ = 128
TK = 256
VMEM_LIMIT = 56 * 1024 * 1024

NT_DIMS = (((1,), (1,)), ((), ()))


def _cparams(*sem):
    return pltpu.CompilerParams(dimension_semantics=sem, vmem_limit_bytes=VMEM_LIMIT)


def _rms(x, g):
    return x * lax.rsqrt(jnp.mean(x * x, axis=-1, keepdims=True) + EPS) * g


def _softplus(x):
    return jnp.maximum(x, 0.0) + jnp.log(1.0 + jnp.exp(-jnp.abs(x)))


def _gelu(x):
    return 0.5 * x * (1.0 + lax.erf(x * math.sqrt(0.5)))


def _dot(a, b):
    return jnp.dot(a, b, preferred_element_type=F32)


def _dot_nt(a, b):
    return lax.dot_general(a, b, NT_DIMS, preferred_element_type=F32)


def _split2(x):
    hi = x.astype(BF16)
    lo = (x - hi.astype(F32)).astype(BF16)
    return hi, lo


def _head_of_lane(rows):
    return lax.broadcasted_iota(jnp.int32, (rows, MIX_W), 1) // HEAD_DIM


def _stack_heads(q):
    t = q.shape[0]
    head = _head_of_lane(t)
    qf = q.astype(F32)
    return jnp.concatenate([jnp.where(head == h, qf, 0.0) for h in range(N_HEADS)], axis=0).astype(BF16)


def _unstack_heads(o, t):
    head = _head_of_lane(t)
    out = jnp.where(head == 0, o[0:t], 0.0)
    for h in range(1, N_HEADS):
        out = jnp.where(head == h, o[h * t:(h + 1) * t], out)
    return out


def _unstack_cols(c, t):
    head = _head_of_lane(t)
    out = jnp.broadcast_to(c[0:t], (t, MIX_W))
    for h in range(1, N_HEADS):
        out = jnp.where(head == h, jnp.broadcast_to(c[h * t:(h + 1) * t], (t, MIX_W)), out)
    return out


def _inproj_kernel(x_ref, g_ref, w_ref, wf_ref, sb_ref, fox_ref, dil_ref, lru_ref, f_ref):
    hn = _rms(x_ref[0], g_ref[...]).astype(BF16)
    for m, ref in enumerate((sb_ref, fox_ref, dil_ref)):
        for j in range(3):
            c = (3 * m + j) * MIX_W
            ref[j, 0] = _dot(hn, w_ref[:, c:c + MIX_W]).astype(BF16)
    for j in range(2):
        c = (9 + j) * MIX_W
        lru_ref[j, 0] = _dot(hn, w_ref[:, c:c + MIX_W])
    f_ref[0] = _dot_nt(wf_ref[...], hn)


def _inproj(x, g, w_main, w_f, ts=512):
    b, s, d = x.shape
    qkv_shape = jax.ShapeDtypeStruct((3, b, s, MIX_W), BF16)
    qkv_spec = pl.BlockSpec((3, 1, ts, MIX_W), lambda bi, i: (0, bi, i, 0))
    return pl.pallas_call(
        _inproj_kernel,
        grid=(b, s // ts),
        in_specs=[
            pl.BlockSpec((1, ts, d), lambda bi, i: (bi, i, 0)),
            pl.BlockSpec((1, d), lambda bi, i: (0, 0)),
            pl.BlockSpec(w_main.shape, lambda bi, i: (0, 0)),
            pl.BlockSpec(w_f.shape, lambda bi, i: (0, 0)),
        ],
        out_specs=[
            qkv_spec, qkv_spec, qkv_spec,
            pl.BlockSpec((2, 1, ts, MIX_W), lambda bi, i: (0, bi, i, 0)),
            pl.BlockSpec((1, 8, ts), lambda bi, i: (bi, 0, i)),
        ],
        out_shape=[
            qkv_shape, qkv_shape, qkv_shape,
            jax.ShapeDtypeStruct((2, b, s, MIX_W), F32),
            jax.ShapeDtypeStruct((b, 8, s), F32),
        ],
        compiler_params=_cparams("parallel", "parallel"),
        name="inproj",
    )(x, g, w_main, w_f)


def _decay_kernel(f_ref, bf_ref, tri_ref, o_ref):
    s = f_ref.shape[2]
    tri = tri_ref[...]
    carry = jnp.zeros((8, 1), F32)
    for c in range(s // 128):
        sl = slice(c * 128, (c + 1) * 128)
        log_f = -_softplus(-(f_ref[0, :, sl] + bf_ref[...]))
        p1 = log_f.astype(BF16)
        r1 = log_f - p1.astype(F32)
        p2 = r1.astype(BF16)
        p3 = (r1 - p2.astype(F32)).astype(BF16)
        inc = _dot(p1, tri) + _dot(p2, tri) + _dot(p3, tri)
        o_ref[0, :, sl] = inc + carry
        carry = carry + jnp.sum(log_f, axis=-1, keepdims=True)


def _decay(f_t, b_f):
    b, _, s = f_t.shape
    tri = (lax.broadcasted_iota(jnp.int32, (128, 128), 0)
           <= lax.broadcasted_iota(jnp.int32, (128, 128), 1)).astype(BF16)
    return pl.pallas_call(
        _decay_kernel,
        grid=(b,),
        in_specs=[
            pl.BlockSpec((1, 8, s), lambda bi: (bi, 0, 0)),
            pl.BlockSpec((8, 1), lambda bi: (0, 0)),
            pl.BlockSpec((128, 128), lambda bi: (0, 0)),
        ],
        out_specs=pl.BlockSpec((1, 8, s), lambda bi: (bi, 0, 0)),
        out_shape=jax.ShapeDtypeStruct((b, 8, s), F32),
        compiler_params=_cparams("parallel"),
        name="fox_decay",
    )(f_t, b_f, tri)


def _block_positions(i, ks):
    row = lax.broadcasted_iota(jnp.int32, (N_HEADS * TQ, TK), 0) % TQ + i * TQ
    col = lax.broadcasted_iota(jnp.int32, (N_HEADS * TQ, TK), 1) + ks
    return row, col


def _sb_kernel(q_ref, k_ref, v_ref, u_ref, o_ref, acc_ref, car_ref):
    i = pl.program_id(1)
    qs = _stack_heads(q_ref[0, 0])
    acc_ref[...] = jnp.zeros_like(acc_ref)
    car_ref[...] = jnp.zeros_like(car_ref)
    n_kb = (i * TQ) // TK + 1

    def body(j, carry):
        ks = pl.multiple_of((n_kb - 1 - j) * TK, TK)
        z = _dot_nt(qs, k_ref[0, 0, pl.ds(ks, TK), :])
        row, col = _block_positions(i, ks)
        strict = col < row
        log_keep_all = -_softplus(z)
        log_keep = jnp.where(strict, log_keep_all, 0.0)
        hi, lo = _split2(log_keep)
        u = u_ref[...]
        after = _dot(hi, u) + _dot(lo, u) + car_ref[...]
        att = jnp.where(strict, jnp.exp(log_keep_all + z + after), 0.0)
        acc_ref[...] += _dot(att.astype(BF16), v_ref[0, 0, pl.ds(ks, TK), :])
        car_ref[...] += jnp.sum(log_keep, axis=-1, keepdims=True)
        return carry

    lax.fori_loop(0, n_kb, body, 0)
    o_ref[0] = _unstack_heads(acc_ref[...], TQ).astype(o_ref.dtype)


def _sb_attention(qkv):
    _, b, s, _ = qkv.shape
    u = (lax.broadcasted_iota(jnp.int32, (TK, TK), 0)
         > lax.broadcasted_iota(jnp.int32, (TK, TK), 1)).astype(BF16)
    return pl.pallas_call(
        _sb_kernel,
        grid=(b, s // TQ),
        in_specs=[
            pl.BlockSpec((1, 1, TQ, MIX_W), lambda bi, i: (0, bi, i, 0)),
            pl.BlockSpec((1, 1, s, MIX_W), lambda bi, i: (1, bi, 0, 0)),
            pl.BlockSpec((1, 1, s, MIX_W), lambda bi, i: (2, bi, 0, 0)),
            pl.BlockSpec((TK, TK), lambda bi, i: (0, 0)),
        ],
        out_specs=pl.BlockSpec((1, TQ, MIX_W), lambda bi, i: (bi, i, 0)),
        out_shape=jax.ShapeDtypeStruct((b, s, MIX_W), BF16),
        scratch_shapes=[
            pltpu.VMEM((N_HEADS * TQ, MIX_W), F32),
            pltpu.VMEM((N_HEADS * TQ, 1), F32),
        ],
        compiler_params=_cparams("parallel", "arbitrary"),
        name="sb_attn",
    )(qkv, qkv, qkv, u)


def _fox_kernel(q_ref, k_ref, v_ref, cum_ref, o_ref, acc_ref, m_ref, l_ref):
    i = pl.program_id(1)
    qs = _stack_heads(q_ref[0, 0])
    acc_ref[...] = jnp.zeros_like(acc_ref)
    l_ref[...] = jnp.zeros_like(l_ref)
    m_ref[...] = jnp.full_like(m_ref, NEG_BIG)
    n_kb = (i * TQ) // TK + 1

    def body(j, carry):
        ks = pl.multiple_of((n_kb - 1 - j) * TK, TK)
        z = _dot_nt(qs, k_ref[0, 0, pl.ds(ks, TK), :])
        cum = cum_ref[0, :, pl.ds(ks, TK)]
        decay = jnp.concatenate(
            [jnp.broadcast_to(cum[h:h + 1], (TQ, TK)) for h in range(N_HEADS)], axis=0)
        row, col = _block_positions(i, ks)
        z = jnp.where(col <= row, z - decay, NEG_BIG)
        m_old = m_ref[...]
        m_new = jnp.maximum(m_old, jnp.max(z, axis=-1, keepdims=True))
        p = jnp.exp(z - m_new)
        alpha = jnp.exp(m_old - m_new)
        l_ref[...] = alpha * l_ref[...] + jnp.sum(p, axis=-1, keepdims=True)
        acc_ref[...] = alpha * acc_ref[...] + _dot(p.astype(BF16), v_ref[0, 0, pl.ds(ks, TK), :])
        m_ref[...] = m_new
        return carry

    lax.fori_loop(0, n_kb, body, 0)
    o_ref[0] = _unstack_heads(acc_ref[...] / l_ref[...], TQ).astype(o_ref.dtype)


def _fox_attention(qkv, cum):
    _, b, s, _ = qkv.shape
    return pl.pallas_call(
        _fox_kernel,
        grid=(b, s // TQ),
        in_specs=[
            pl.BlockSpec((1, 1, TQ, MIX_W), lambda bi, i: (0, bi, i, 0)),
            pl.BlockSpec((1, 1, s, MIX_W), lambda bi, i: (1, bi, 0, 0)),
            pl.BlockSpec((1, 1, s, MIX_W), lambda bi, i: (2, bi, 0, 0)),
            pl.BlockSpec((1, 8, s), lambda bi, i: (bi, 0, 0)),
        ],
        out_specs=pl.BlockSpec((1, TQ, MIX_W), lambda bi, i: (bi, i, 0)),
        out_shape=jax.ShapeDtypeStruct((b, s, MIX_W), BF16),
        scratch_shapes=[
            pltpu.VMEM((N_HEADS * TQ, MIX_W), F32),
            pltpu.VMEM((N_HEADS * TQ, 1), F32),
            pltpu.VMEM((N_HEADS * TQ, 1), F32),
        ],
        compiler_params=_cparams("parallel", "arbitrary"),
        name="fox_attn",
    )(qkv, qkv, qkv, cum)


def _t5_bucket(dist):
    n = jnp.maximum(dist, 0)
    max_exact = NUM_BUCKETS // 2
    nf = jnp.maximum(n, 1).astype(F32)
    large = max_exact + (jnp.log(nf / max_exact) / math.log(MAX_DISTANCE / max_exact)
                         * (NUM_BUCKETS - max_exact)).astype(jnp.int32)
    large = jnp.minimum(large, NUM_BUCKETS - 1)
    return jnp.where(n < max_exact, n, large)


def _bias_kernel(idx_ref, rb_ref, o_ref):
    for d in range(len(DILATIONS)):
        idx = idx_ref[d]
        for h in range(N_HEADS):
            acc = jnp.full(idx.shape, NEG_BIG, F32)
            for bucket in range(NUM_BUCKETS):
                acc = jnp.where(idx == bucket, rb_ref[bucket, h], acc)
            o_ref[d, h] = acc


def _bias_tables(rel_bias):
    t = jnp.arange(TQ)[:, None]
    c = jnp.arange(2 * TQ)[None, :]
    delta = TQ + t - c
    valid = (delta >= 0) & (delta <= DIL_STEPS)
    idx = jnp.stack([jnp.where(valid, _t5_bucket(delta * dil), -1) for dil in DILATIONS])
    out = pl.pallas_call(
        _bias_kernel,
        in_specs=[
            pl.BlockSpec(memory_space=pltpu.VMEM),
            pl.BlockSpec(memory_space=pltpu.SMEM),
        ],
        out_specs=pl.BlockSpec(memory_space=pltpu.VMEM),
        out_shape=jax.ShapeDtypeStruct((len(DILATIONS), N_HEADS, TQ, 2 * TQ), F32),
        name="dil_bias",
    )(idx.astype(jnp.int32), rel_bias)
    return out.reshape(len(DILATIONS), N_HEADS * TQ, 2 * TQ)


def _dil_kernel(q_ref, k_ref, v_ref, t_ref, o_ref, lse_ref, *, n_blocks):
    n = pl.program_id(2)
    qs = _stack_heads(q_ref[0, 0])

    def finish(scores, v_win):
        m = jnp.max(scores, axis=-1, keepdims=True)
        p = jnp.exp(scores - m)
        l = jnp.sum(p, axis=-1, keepdims=True)
        o = _dot(p.astype(BF16), v_win) / l
        o_ref[0] = _unstack_heads(o, TQ)
        lse_ref[0] = _unstack_cols(m + jnp.log(l), TQ)

    @pl.when(n == 0)
    def _():
        scores = _dot_nt(qs, k_ref[0, 0, 0:TQ, :]) + t_ref[0, :, TQ:2 * TQ]
        finish(scores, v_ref[0, 0, 0:TQ, :])

    if n_blocks > 1:
        @pl.when(n > 0)
        def _():
            st = pl.multiple_of((n - 1) * TQ, TQ)
            scores = _dot_nt(qs, k_ref[0, 0, pl.ds(st, 2 * TQ), :]) + t_ref[0]
            finish(scores, v_ref[0, 0, pl.ds(st, 2 * TQ), :])


def _dil_branch(qkv, tables, branch):
    _, b, s, _ = qkv.shape
    dil = DILATIONS[branch]
    cls_len = s // dil
    n_blocks = cls_len // TQ
    view = qkv.reshape(3, b, cls_len, dil * MIX_W)
    out_shape = jax.ShapeDtypeStruct((b, cls_len, dil * MIX_W), F32)
    out_spec = pl.BlockSpec((1, TQ, MIX_W), lambda bi, r, n: (bi, n, r))
    o, lse = pl.pallas_call(
        functools.partial(_dil_kernel, n_blocks=n_blocks),
        grid=(b, dil, n_blocks),
        in_specs=[
            pl.BlockSpec((1, 1, TQ, MIX_W), lambda bi, r, n: (0, bi, n, r)),
            pl.BlockSpec((1, 1, cls_len, MIX_W), lambda bi, r, n: (1, bi, 0, r)),
            pl.BlockSpec((1, 1, cls_len, MIX_W), lambda bi, r, n: (2, bi, 0, r)),
            pl.BlockSpec((1, N_HEADS * TQ, 2 * TQ), lambda bi, r, n: (branch, 0, 0)),
        ],
        out_specs=[out_spec, out_spec],
        out_shape=[out_shape, out_shape],
        compiler_params=_cparams("parallel", "parallel", "arbitrary"),
        name=f"dil_attn_{dil}",
    )(view, view, view, tables)
    return o.reshape(b, s, MIX_W), lse.reshape(b, s, MIX_W)


def _dil_combine_kernel(o1, l1, o2, l2, o3, l3, out_ref):
    a1, a2, a3 = l1[0], l2[0], l3[0]
    m = jnp.maximum(jnp.maximum(a1, a2), a3)
    e1, e2, e3 = jnp.exp(a1 - m), jnp.exp(a2 - m), jnp.exp(a3 - m)
    tot = e1 + e2 + e3
    out = (e1 / tot) * o1[0] + (e2 / tot) * o2[0] + (e3 / tot) * o3[0]
    out_ref[0] = out.astype(out_ref.dtype)


def _dil_combine(branches, ts=512):
    b, s, w = branches[0][0].shape
    spec = pl.BlockSpec((1, ts, w), lambda bi, i: (bi, i, 0))
    flat = [a for pair in branches for a in pair]
    return pl.pallas_call(
        _dil_combine_kernel,
        grid=(b, s // ts),
        in_specs=[spec] * 6,
        out_specs=spec,
        out_shape=jax.ShapeDtypeStruct((b, s, w), BF16),
        compiler_params=_cparams("parallel", "parallel"),
        name="dil_combine",
    )(*flat)


def _shift_rows(x, k, fill):
    row = lax.broadcasted_iota(jnp.int32, x.shape, 0)
    return jnp.where(row >= k, pltpu.roll(x, k, 0), fill)


def _lru_kernel(xg_ref, cw_ref, cb_ref, wa_ref, ba_ref, wx_ref, bx_ref, lam_ref, o_ref):
    x = xg_ref[0, 0]
    gate = xg_ref[1, 0]
    s = x.shape[0]
    kw = cw_ref.shape[0]
    xc = cb_ref[...] + cw_ref[kw - 1:kw, :] * x
    for j in range(kw - 1):
        xc = xc + cw_ref[j:j + 1, :] * _shift_rows(x, kw - 1 - j, 0.0)
    xcb = xc.astype(BF16)
    r = jax.nn.sigmoid(_dot(xcb, wa_ref[...]) + ba_ref[...])
    i_gate = jax.nn.sigmoid(_dot(xcb, wx_ref[...]) + bx_ref[...])
    log_a = -LRU_C * r * _softplus(-lam_ref[...])
    a = jnp.exp(log_a)
    h = jnp.sqrt(1.0 - jnp.exp(2.0 * log_a)) * (i_gate * xc)
    k = 1
    while k < s:
        h = a * _shift_rows(h, k, 0.0) + h
        a = a * _shift_rows(a, k, 1.0)
        k *= 2
    o_ref[0] = (h * _gelu(gate)).astype(o_ref.dtype)


def _lru(xg, conv_w, conv_b, wa_bd, b_a, wx_bd, b_x, lam):
    _, b, s, w = xg.shape
    full = lambda a: pl.BlockSpec(a.shape, lambda bi: (0,) * a.ndim)
    args = (conv_w, conv_b, wa_bd, b_a, wx_bd, b_x, lam)
    return pl.pallas_call(
        _lru_kernel,
        grid=(b,),
        in_specs=[pl.BlockSpec((2, 1, s, w), lambda bi: (0, bi, 0, 0))] + [full(a) for a in args],
        out_specs=pl.BlockSpec((1, s, w), lambda bi: (bi, 0, 0)),
        out_shape=jax.ShapeDtypeStruct((b, s, w), BF16),
        compiler_params=_cparams("parallel"),
        name="rg_lru",
    )(xg, *args)


def _memkv_kernel(mem_ref, g_ref, wk_ref, wv_ref, k_ref, v_ref):
    mn = _rms(mem_ref[0], g_ref[...]).astype(BF16)
    k_ref[0] = _dot(mn, wk_ref[...]).astype(BF16)
    v_ref[0] = _dot(mn, wv_ref[...]).astype(BF16)


def _memkv(mem, g, w_ck, w_cv):
    b, m, d = mem.shape
    out = jax.ShapeDtypeStruct((b, m, MIX_W), BF16)
    return pl.pallas_call(
        _memkv_kernel,
        grid=(b,),
        in_specs=[
            pl.BlockSpec((1, m, d), lambda bi: (bi, 0, 0)),
            pl.BlockSpec((1, d), lambda bi: (0, 0)),
            pl.BlockSpec((d, MIX_W), lambda bi: (0, 0)),
            pl.BlockSpec((d, MIX_W), lambda bi: (0, 0)),
        ],
        out_specs=[pl.BlockSpec((1, m, MIX_W), lambda bi: (bi, 0, 0))] * 2,
        out_shape=[out, out],
        compiler_params=_cparams("parallel"),
        name="mem_kv",
    )(mem, g, w_ck, w_cv)


def _mix_cross_kernel(x_ref, sb_ref, fox_ref, dil_ref, lru_ref, wo_ref, g_ref, wq_ref,
                      km_ref, vm_ref, wco_ref, o_ref):
    x = x_ref[0]
    for j, ref in enumerate((sb_ref, fox_ref, dil_ref, lru_ref)):
        x = x + _dot(ref[0], wo_ref[j])
    t = x.shape[0]
    hn = _rms(x, g_ref[...]).astype(BF16)
    qs = _stack_heads(_dot(hn, wq_ref[...]))
    scores = _dot_nt(qs, km_ref[0])
    m = jnp.max(scores, axis=-1, keepdims=True)
    p = jnp.exp(scores - m)
    p = p / jnp.sum(p, axis=-1, keepdims=True)
    o = _unstack_heads(_dot(p.astype(BF16), vm_ref[0]), t)
    o_ref[0] = x + _dot(o.astype(BF16), wco_ref[...])


def _mix_cross(x, mixers, w_out, g, w_cq, k_mem, v_mem, w_co, ts=256):
    b, s, d = x.shape
    m = k_mem.shape[1]
    mix_spec = pl.BlockSpec((1, ts, MIX_W), lambda bi, i: (bi, i, 0))
    const = lambda a: pl.BlockSpec(a.shape, lambda bi, i: (0,) * a.ndim)
    mem_spec = pl.BlockSpec((1, m, MIX_W), lambda bi, i: (bi, 0, 0))
    return pl.pallas_call(
        _mix_cross_kernel,
        grid=(b, s // ts),
        in_specs=[pl.BlockSpec((1, ts, d), lambda bi, i: (bi, i, 0))] + [mix_spec] * 4
                 + [const(w_out), const(g), const(w_cq), mem_spec, mem_spec, const(w_co)],
        out_specs=pl.BlockSpec((1, ts, d), lambda bi, i: (bi, i, 0)),
        out_shape=jax.ShapeDtypeStruct((b, s, d), F32),
        compiler_params=_cparams("parallel", "parallel"),
        name="mix_cross",
    )(x, *mixers, w_out, g, w_cq, k_mem, v_mem, w_co)


HALO = 8


def _ffn_kernel(x_ref, halo_ref, g_ref, wup_ref, cw_ref, cb_ref, wdn_ref, gf_ref, o_ref,
                ext_ref, acc_ref, *, final_norm):
    i = pl.program_id(1)
    x = x_ref[0]
    t = x.shape[0]
    g = g_ref[...]
    halo = jnp.where(i > 0, _rms(halo_ref[0], g), 0.0)
    hn = jnp.concatenate([halo, _rms(x, g)], axis=0).astype(BF16)
    acc_ref[...] = jnp.zeros_like(acc_ref)

    def conv(c):
        ext_ref[...] = _dot(hn, wup_ref[c])
        w = cw_ref[c]
        return (cb_ref[c] + w[2:3] * ext_ref[pl.ds(HALO, t), :]
                + w[1:2] * ext_ref[pl.ds(HALO - 1, t), :]
                + w[0:1] * ext_ref[pl.ds(HALO - 2, t), :])

    def body(c, carry):
        up = conv(c)
        gate = conv(N_FF_CHUNKS + c)
        act = (_gelu(gate) * up).astype(BF16)
        acc_ref[...] += _dot(act, wdn_ref[c])
        return carry

    lax.fori_loop(0, N_FF_CHUNKS, body, 0)
    out = x + acc_ref[...]
    if final_norm:
        out = _rms(out, gf_ref[...])
    o_ref[0] = out


def _ffn(x, g, w_up, conv_w, conv_b, w_down, g_final, final_norm, ts=256):
    b, s, d = x.shape
    per_tile = ts // HALO
    const = lambda a: pl.BlockSpec(a.shape, lambda bi, i: (0,) * a.ndim)
    return pl.pallas_call(
        functools.partial(_ffn_kernel, final_norm=final_norm),
        grid=(b, s // ts),
        in_specs=[
            pl.BlockSpec((1, ts, d), lambda bi, i: (bi, i, 0)),
            pl.BlockSpec((1, HALO, d), lambda bi, i: (bi, jnp.maximum(i * per_tile - 1, 0), 0)),
            const(g), const(w_up), const(conv_w), const(conv_b), const(w_down), const(g_final),
        ],
        out_specs=pl.BlockSpec((1, ts, d), lambda bi, i: (bi, i, 0)),
        out_shape=jax.ShapeDtypeStruct((b, s, d), F32),
        scratch_shapes=[
            pltpu.VMEM((HALO + ts, FF_CHUNK), F32),
            pltpu.VMEM((ts, d), F32),
        ],
        compiler_params=_cparams("parallel", "arbitrary"),
        name="conv_geglu",
    )(x, x, g, w_up, conv_w, conv_b, w_down, g_final)


def _block_diag(w):
    g, n, _ = w.shape
    eye = jnp.eye(g, dtype=w.dtype)
    return (eye[:, None, :, None] * w[:, :, None, :]).reshape(g * n, g * n)


def kernel(x, mem, norm_mix_g, w_in, b_forget, lru_conv_w, lru_conv_b, lru_w_a, lru_b_a, lru_w_x, lru_b_x, lru_lambda, w_out, norm_cross_g, norm_mem_g, w_cq, w_ck, w_cv, w_co, norm_ffn_g, w_up, ffn_conv_w, ffn_conv_b, w_down, rel_bias, final_norm_g):
    depth = w_in.shape[0]
    qkv_w = 3 * MIX_W
    q_scale = HEAD_DIM ** -0.5
    col_scale = jnp.ones((3, 3, MIX_W), F32).at[:, 0, :].set(q_scale).reshape(-1)
    col_scale = jnp.concatenate([col_scale, jnp.ones((2 * MIX_W,), F32)])
    tables = _bias_tables(rel_bias)
    row = lambda v: v.reshape(1, -1)

    for l in range(depth):
        w = w_in[l]
        n_f = N_HEADS
        w_main = jnp.concatenate([w[:, :2 * qkv_w], w[:, 2 * qkv_w + n_f:]], axis=1)
        w_main = (w_main * col_scale).astype(BF16)
        w_f = jnp.zeros((8, D_MODEL), F32).at[:n_f].set(w[:, 2 * qkv_w:2 * qkv_w + n_f].T).astype(BF16)
        b_f = jnp.zeros((8, 1), F32).at[:n_f, 0].set(b_forget[l])

        sb_qkv, fox_qkv, dil_qkv, lru_xg, f_t = _inproj(x, row(norm_mix_g[l]), w_main, w_f)
        o_sb = _sb_attention(sb_qkv)
        o_fox = _fox_attention(fox_qkv, _decay(f_t, b_f))
        o_dil = _dil_combine([_dil_branch(dil_qkv, tables, br) for br in range(len(DILATIONS))])
        o_lru = _lru(lru_xg, lru_conv_w[l], row(lru_conv_b[l]),
                     _block_diag(lru_w_a[l]).astype(BF16), row(lru_b_a[l]),
                     _block_diag(lru_w_x[l]).astype(BF16), row(lru_b_x[l]), row(lru_lambda[l]))

        k_mem, v_mem = _memkv(mem, row(norm_mem_g[l]), w_ck[l].astype(BF16), w_cv[l].astype(BF16))
        x = _mix_cross(x, (o_sb, o_fox, o_dil, o_lru),
                       w_out[l].reshape(4, MIX_W, D_MODEL).astype(BF16), row(norm_cross_g[l]),
                       (w_cq[l] * q_scale).astype(BF16), k_mem, v_mem, w_co[l].astype(BF16))

        wu = w_up[l].reshape(D_MODEL, 2 * N_FF_CHUNKS, FF_CHUNK).transpose(1, 0, 2).astype(BF16)
        cw = ffn_conv_w[l].reshape(-1, 2 * N_FF_CHUNKS, FF_CHUNK).transpose(1, 0, 2)
        cb = ffn_conv_b[l].reshape(2 * N_FF_CHUNKS, 1, FF_CHUNK)
        wd = w_down[l].reshape(N_FF_CHUNKS, FF_CHUNK, D_MODEL).astype(BF16)
        x = _ffn(x, row(norm_ffn_g[l]), wu, cw, cb, wd, row(final_norm_g), l == depth - 1)
    return x
```

```python
import functools
import math

import jax
import jax.numpy as jnp
from jax import lax
from jax.experimental import pallas as pl
from jax.experimental.pallas import tpu as pltpu

F32 = jnp.float32
BF16 = jnp.bfloat16

LANES = 128
D_MODEL = 1024
HEAD_DIM = 64
N_HEADS = 4
MIX_W = N_HEADS * HEAD_DIM
D_FF = 2816
FF_CHUNK = 256
N_FF_CHUNKS = D_FF // FF_CHUNK
N_MEM = 256
NUM_BUCKETS = 32
MAX_DISTANCE = 2048
DILATIONS = (1, 4, 16)
DIL_STEPS = 128
LRU_C = 8.0
EPS = 1e-6
NEG_BIG = -1e30

TQ = 128
TK = 256
VMEM_LIMIT = 56 * 1024 * 1024

NT_DIMS = (((1,), (1,)), ((), ()))


def _cparams(*sem):
    return pltpu.CompilerParams(dimension_semantics=sem, vmem_limit_bytes=VMEM_LIMIT)


def _rms(x, g):
    return x * lax.rsqrt(jnp.mean(x * x, axis=-1, keepdims=True) + EPS) * g


def _softplus(x):
    return jnp.maximum(x, 0.0) + jnp.log(1.0 + jnp.exp(-jnp.abs(x)))


def _gelu(x):
    return 0.5 * x * (1.0 + lax.erf(x * math.sqrt(0.5)))


def _dot(a, b):
    return jnp.dot(a, b, preferred_element_type=F32)


def _dot_nt(a, b):
    return lax.dot_general(a, b, NT_DIMS, preferred_element_type=F32)


def _split2(x):
    hi = x.astype(BF16)
    lo = (x - hi.astype(F32)).astype(BF16)
    return hi, lo


def _head_of_lane(rows):
    return lax.broadcasted_iota(jnp.int32, (rows, MIX_W), 1) // HEAD_DIM


def _stack_heads(q):
    t = q.shape[0]
    head = _head_of_lane(t)
    qf = q.astype(F32)
    return jnp.concatenate([jnp.where(head == h, qf, 0.0) for h in range(N_HEADS)], axis=0).astype(BF16)


def _unstack_heads(o, t):
    head = _head_of_lane(t)
    out = jnp.where(head == 0, o[0:t], 0.0)
    for h in range(1, N_HEADS):
        out = jnp.where(head == h, o[h * t:(h + 1) * t], out)
    return out


def _unstack_cols(c, t):
    head = _head_of_lane(t)
    out = jnp.broadcast_to(c[0:t], (t, MIX_W))
    for h in range(1, N_HEADS):
        out = jnp.where(head == h, jnp.broadcast_to(c[h * t:(h + 1) * t], (t, MIX_W)), out)
    return out


def _inproj_kernel(x_ref, g_ref, w_ref, wf_ref, sb_ref, fox_ref, d1_ref, d4_ref, d16_ref,
                   lru_ref, f_ref, half_ref):
    hn = _rms(x_ref[0], g_ref[...]).astype(BF16)
    ts = hn.shape[0]
    for m, ref in enumerate((sb_ref, fox_ref)):
        for j in range(3):
            c = (3 * m + j) * MIX_W
            ref[j, 0] = _dot(hn, w_ref[:, c:c + MIX_W]).astype(BF16)
    for j in range(3):
        c = (6 + j) * MIX_W
        res = _dot(hn, w_ref[:, c:c + MIX_W])
        d1_ref[j, 0, 0] = res.astype(BF16)
        for h in range(2):
            half_ref[h] = res[:, h * LANES:(h + 1) * LANES]
        for ref, dil in ((d4_ref, DILATIONS[1]), (d16_ref, DILATIONS[2])):
            for r in range(dil):
                for h in range(2):
                    ref[j, 0, r, :, h * LANES:(h + 1) * LANES] = (
                        half_ref[h, pl.ds(r, ts // dil, stride=dil), :].astype(BF16))
    for j in range(2):
        c = (9 + j) * MIX_W
        lru_ref[j, 0] = _dot(hn, w_ref[:, c:c + MIX_W])
    f_ref[0] = _dot_nt(wf_ref[...], hn)


def _inproj(x, g, w_main, w_f, ts=512):
    b, s, d = x.shape
    qkv_shape = jax.ShapeDtypeStruct((3, b, s, MIX_W), BF16)
    qkv_spec = pl.BlockSpec((3, 1, ts, MIX_W), lambda bi, i: (0, bi, i, 0))
    cls_shapes = [jax.ShapeDtypeStruct((3, b, dil, s // dil, MIX_W), BF16) for dil in DILATIONS]
    cls_specs = [pl.BlockSpec((3, 1, dil, ts // dil, MIX_W), lambda bi, i: (0, bi, 0, i, 0))
                 for dil in DILATIONS]
    return pl.pallas_call(
        _inproj_kernel,
        grid=(b, s // ts),
        in_specs=[
            pl.BlockSpec((1, ts, d), lambda bi, i: (bi, i, 0)),
            pl.BlockSpec((1, d), lambda bi, i: (0, 0)),
            pl.BlockSpec(w_main.shape, lambda bi, i: (0, 0)),
            pl.BlockSpec(w_f.shape, lambda bi, i: (0, 0)),
        ],
        out_specs=[
            qkv_spec, qkv_spec, *cls_specs,
            pl.BlockSpec((2, 1, ts, MIX_W), lambda bi, i: (0, bi, i, 0)),
            pl.BlockSpec((1, 8, ts), lambda bi, i: (bi, 0, i)),
        ],
        out_shape=[
            qkv_shape, qkv_shape, *cls_shapes,
            jax.ShapeDtypeStruct((2, b, s, MIX_W), F32),
            jax.ShapeDtypeStruct((b, 8, s), F32),
        ],
        scratch_shapes=[pltpu.VMEM((2, ts, LANES), F32)],
        compiler_params=_cparams("parallel", "parallel"),
        name="inproj",
    )(x, g, w_main, w_f)


def _decay_kernel(f_ref, bf_ref, tri_ref, o_ref):
    s = f_ref.shape[2]
    tri = tri_ref[...]
    carry = jnp.zeros((8, 1), F32)
    for c in range(s // 128):
        sl = slice(c * 128, (c + 1) * 128)
        log_f = -_softplus(-(f_ref[0, :, sl] + bf_ref[...]))
        p1 = log_f.astype(BF16)
        r1 = log_f - p1.astype(F32)
        p2 = r1.astype(BF16)
        p3 = (r1 - p2.astype(F32)).astype(BF16)
        inc = _dot(p1, tri) + _dot(p2, tri) + _dot(p3, tri)
        o_ref[0, :, sl] = inc + carry
        carry = carry + jnp.sum(log_f, axis=-1, keepdims=True)


def _decay(f_t, b_f):
    b, _, s = f_t.shape
    tri = (lax.broadcasted_iota(jnp.int32, (128, 128), 0)
           <= lax.broadcasted_iota(jnp.int32, (128, 128), 1)).astype(BF16)
    return pl.pallas_call(
        _decay_kernel,
        grid=(b,),
        in_specs=[
            pl.BlockSpec((1, 8, s), lambda bi: (bi, 0, 0)),
            pl.BlockSpec((8, 1), lambda bi: (0, 0)),
            pl.BlockSpec((128, 128), lambda bi: (0, 0)),
        ],
        out_specs=pl.BlockSpec((1, 8, s), lambda bi: (bi, 0, 0)),
        out_shape=jax.ShapeDtypeStruct((b, 8, s), F32),
        compiler_params=_cparams("parallel"),
        name="fox_decay",
    )(f_t, b_f, tri)


def _block_positions(i, ks):
    row = lax.broadcasted_iota(jnp.int32, (N_HEADS * TQ, TK), 0) % TQ + i * TQ
    col = lax.broadcasted_iota(jnp.int32, (N_HEADS * TQ, TK), 1) + ks
    return row, col


SB_SATURATED = 110.0


def _sb_kernel(q_ref, k_ref, v_ref, uo_ref, o_ref, acc_ref, car_ref):
    i = pl.program_id(1)
    qs = _stack_heads(q_ref[0, 0])
    k_diag = (i * TQ) // TK

    def block(kb, diagonal):
        ks = pl.multiple_of(kb * TK, TK)
        z = _dot_nt(qs, k_ref[0, 0, pl.ds(ks, TK), :])
        drop = _softplus(z)
        if diagonal:
            row, col = _block_positions(i, ks)
            strict = col < row
            cum_in = jnp.where(strict, drop, 0.0).astype(BF16)
        else:
            cum_in = drop.astype(BF16)
        sums = _dot(cum_in, uo_ref[...])
        if diagonal:
            att = jnp.where(strict, jnp.exp(z - drop - sums[:, :TK]), 0.0)
            car_ref[...] = sums[:, TK:]
            acc_ref[...] = _dot(att.astype(BF16), v_ref[0, 0, pl.ds(ks, TK), :])
        else:
            car = car_ref[...]
            att = jnp.exp(z - drop - sums[:, :TK] - car)
            car_ref[...] = car + sums[:, TK:]
            acc_ref[...] += _dot(att.astype(BF16), v_ref[0, 0, pl.ds(ks, TK), :])
        return jnp.min(car_ref[...]) < SB_SATURATED

    def cond(state):
        kb, live = state
        return jnp.logical_and(kb >= 0, live)

    def body(state):
        kb, _ = state
        return kb - 1, block(kb, False)

    lax.while_loop(cond, body, (k_diag - 1, block(k_diag, True)))
    o_ref[0] = _unstack_heads(acc_ref[...], TQ).astype(o_ref.dtype)


def _sb_attention(qkv):
    _, b, s, _ = qkv.shape
    later = (lax.broadcasted_iota(jnp.int32, (TK, TK), 0)
             > lax.broadcasted_iota(jnp.int32, (TK, TK), 1))
    u = jnp.concatenate([later, jnp.ones((TK, TK), bool)], axis=1).astype(BF16)
    return pl.pallas_call(
        _sb_kernel,
        grid=(b, s // TQ),
        in_specs=[
            pl.BlockSpec((1, 1, TQ, MIX_W), lambda bi, i: (0, bi, i, 0)),
            pl.BlockSpec((1, 1, s, MIX_W), lambda bi, i: (1, bi, 0, 0)),
            pl.BlockSpec((1, 1, s, MIX_W), lambda bi, i: (2, bi, 0, 0)),
            pl.BlockSpec((TK, 2 * TK), lambda bi, i: (0, 0)),
        ],
        out_specs=pl.BlockSpec((1, TQ, MIX_W), lambda bi, i: (bi, i, 0)),
        out_shape=jax.ShapeDtypeStruct((b, s, MIX_W), BF16),
        scratch_shapes=[
            pltpu.VMEM((N_HEADS * TQ, MIX_W), F32),
            pltpu.VMEM((N_HEADS * TQ, TK), F32),
        ],
        compiler_params=_cparams("parallel", "arbitrary"),
        name="sb_attn",
    )(qkv, qkv, qkv, u)


def _fox_kernel(q_ref, k_ref, v_ref, cum_ref, o_ref, z_ref, mx_ref, l_ref, acc_ref):
    i = pl.program_id(1)
    qs = _stack_heads(q_ref[0, 0])
    k_diag = (i * TQ) // TK

    def logits(kb):
        ks = pl.multiple_of(kb * TK, TK)
        cum = cum_ref[0, :, pl.ds(ks, TK)]
        decay = jnp.concatenate(
            [jnp.broadcast_to(cum[h:h + 1], (TQ, TK)) for h in range(N_HEADS)], axis=0)
        return _dot_nt(qs, k_ref[0, 0, pl.ds(ks, TK), :]) - decay, ks

    z, ks = logits(k_diag)
    row, col = _block_positions(i, ks)
    z = jnp.where(col <= row, z, NEG_BIG)
    z_ref[k_diag] = z
    mx_ref[...] = z

    def pass1(kb, carry):
        z, _ = logits(kb)
        z_ref[kb] = z
        mx_ref[...] = jnp.maximum(mx_ref[...], z)
        return carry

    lax.fori_loop(0, k_diag, pass1, 0)
    m = jnp.max(mx_ref[...], axis=-1, keepdims=True)
    mx_ref[...] = jnp.broadcast_to(m, mx_ref.shape)

    l_ref[...] = jnp.zeros_like(l_ref)
    acc_ref[...] = jnp.zeros_like(acc_ref)

    def pass2(kb, carry):
        ks = pl.multiple_of(kb * TK, TK)
        p = jnp.exp(z_ref[kb] - mx_ref[...])
        l_ref[...] += p
        acc_ref[...] += _dot(p.astype(BF16), v_ref[0, 0, pl.ds(ks, TK), :])
        return carry

    lax.fori_loop(0, k_diag + 1, pass2, 0)
    l = jnp.sum(l_ref[...], axis=-1, keepdims=True)
    o_ref[0] = _unstack_heads(acc_ref[...] / l, TQ).astype(o_ref.dtype)


def _fox_attention(qkv, cum):
    _, b, s, _ = qkv.shape
    blk = (N_HEADS * TQ, TK)
    return pl.pallas_call(
        _fox_kernel,
        grid=(b, s // TQ),
        in_specs=[
            pl.BlockSpec((1, 1, TQ, MIX_W), lambda bi, i: (0, bi, i, 0)),
            pl.BlockSpec((1, 1, s, MIX_W), lambda bi, i: (1, bi, 0, 0)),
            pl.BlockSpec((1, 1, s, MIX_W), lambda bi, i: (2, bi, 0, 0)),
            pl.BlockSpec((1, 8, s), lambda bi, i: (bi, 0, 0)),
        ],
        out_specs=pl.BlockSpec((1, TQ, MIX_W), lambda bi, i: (bi, i, 0)),
        out_shape=jax.ShapeDtypeStruct((b, s, MIX_W), BF16),
        scratch_shapes=[
            pltpu.VMEM((s // TK,) + blk, F32),
            pltpu.VMEM(blk, F32),
            pltpu.VMEM(blk, F32),
            pltpu.VMEM((N_HEADS * TQ, MIX_W), F32),
        ],
        compiler_params=_cparams("parallel", "arbitrary"),
        name="fox_attn",
    )(qkv, qkv, qkv, cum)


def _t5_bucket(dist):
    n = jnp.maximum(dist, 0)
    max_exact = NUM_BUCKETS // 2
    nf = jnp.maximum(n, 1).astype(F32)
    large = max_exact + (jnp.log(nf / max_exact) / math.log(MAX_DISTANCE / max_exact)
                         * (NUM_BUCKETS - max_exact)).astype(jnp.int32)
    large = jnp.minimum(large, NUM_BUCKETS - 1)
    return jnp.where(n < max_exact, n, large)


def _bias_kernel(idx_ref, rb_ref, o_ref):
    for d in range(len(DILATIONS)):
        idx = idx_ref[d]
        for h in range(N_HEADS):
            acc = jnp.full(idx.shape, NEG_BIG, F32)
            for bucket in range(NUM_BUCKETS):
                acc = jnp.where(idx == bucket, rb_ref[bucket, h], acc)
            o_ref[d, h] = acc


def _bias_tables(rel_bias):
    t = jnp.arange(TQ)[:, None]
    c = jnp.arange(2 * TQ)[None, :]
    delta = TQ + t - c
    valid = (delta >= 0) & (delta <= DIL_STEPS)
    idx = jnp.stack([jnp.where(valid, _t5_bucket(delta * dil), -1) for dil in DILATIONS])
    out = pl.pallas_call(
        _bias_kernel,
        in_specs=[
            pl.BlockSpec(memory_space=pltpu.VMEM),
            pl.BlockSpec(memory_space=pltpu.SMEM),
        ],
        out_specs=pl.BlockSpec(memory_space=pltpu.VMEM),
        out_shape=jax.ShapeDtypeStruct((len(DILATIONS), N_HEADS, TQ, 2 * TQ), F32),
        name="dil_bias",
    )(idx.astype(jnp.int32), rel_bias)
    return out.reshape(len(DILATIONS), N_HEADS * TQ, 2 * TQ)


DIL_GROUP = 4


def _dil_kernel(q_ref, k_ref, v_ref, t_ref, o_ref, lse_ref, *, by_class, n_groups):
    g = pl.program_id(2)

    def attend(jj, qs, k_win, v_win, bias):
        scores = _dot_nt(qs, k_win) + bias
        m = jnp.max(scores, axis=-1, keepdims=True)
        p = jnp.exp(scores - m)
        l = jnp.sum(p, axis=-1, keepdims=True)
        o = _unstack_heads(_dot(p.astype(BF16), v_win) / l, TQ).astype(o_ref.dtype)
        lse = _unstack_cols(m + jnp.log(l), TQ)
        if by_class:
            o_ref[0, jj] = o
            lse_ref[0, jj] = lse
        else:
            o_ref[0, 0, jj * TQ:(jj + 1) * TQ] = o
            lse_ref[0, 0, jj * TQ:(jj + 1) * TQ] = lse

    def first_block(jj, qs, c):
        attend(jj, qs, k_ref[0, 0, c, 0:TQ, :], v_ref[0, 0, c, 0:TQ, :], t_ref[0, :, TQ:2 * TQ])

    def later_block(jj, qs, start):
        attend(jj, qs, k_ref[0, 0, 0, pl.ds(start, 2 * TQ), :],
               v_ref[0, 0, 0, pl.ds(start, 2 * TQ), :], t_ref[0])

    for jj in range(DIL_GROUP):
        if by_class:
            first_block(jj, _stack_heads(q_ref[0, 0, jj]), jj)
            continue
        qs = _stack_heads(q_ref[0, 0, 0, jj * TQ:(jj + 1) * TQ])
        if n_groups == 1:
            if jj == 0:
                first_block(jj, qs, 0)
            else:
                later_block(jj, qs, (jj - 1) * TQ)
        elif jj == 0:
            pl.when(g == 0)(functools.partial(first_block, jj, qs, 0))
            pl.when(g > 0)(functools.partial(
                later_block, jj, qs, pl.multiple_of((g * DIL_GROUP - 1) * TQ, TQ)))
        else:
            later_block(jj, qs, pl.multiple_of((g * DIL_GROUP + jj - 1) * TQ, TQ))


def _dil_branch(qkv, tables, branch):
    _, b, dil, cls_len, _ = qkv.shape
    n_blocks = cls_len // TQ
    by_class = n_blocks == 1
    if by_class:
        grid = (b, dil // DIL_GROUP, 1)
        qkv_specs = [pl.BlockSpec((1, 1, DIL_GROUP, TQ, MIX_W), functools.partial(
            lambda bi, c, g, j: (j, bi, c, 0, 0), j=j)) for j in range(3)]
        out_spec = pl.BlockSpec((1, DIL_GROUP, TQ, MIX_W), lambda bi, c, g: (bi, c, 0, 0))
    else:
        grid = (b, dil, n_blocks // DIL_GROUP)
        qkv_specs = [pl.BlockSpec((1, 1, 1, DIL_GROUP * TQ, MIX_W), lambda bi, r, g: (0, bi, r, g, 0))]
        qkv_specs += [pl.BlockSpec((1, 1, 1, cls_len, MIX_W), functools.partial(
            lambda bi, r, g, j: (j, bi, r, 0, 0), j=j)) for j in (1, 2)]
        out_spec = pl.BlockSpec((1, 1, DIL_GROUP * TQ, MIX_W), lambda bi, r, g: (bi, r, g, 0))
    o, lse = pl.pallas_call(
        functools.partial(_dil_kernel, by_class=by_class, n_groups=grid[2]),
        grid=grid,
        in_specs=qkv_specs + [
            pl.BlockSpec((1, N_HEADS * TQ, 2 * TQ), lambda bi, r, g: (branch, 0, 0))],
        out_specs=[out_spec, out_spec],
        out_shape=[jax.ShapeDtypeStruct((b, dil, cls_len, MIX_W), BF16),
                   jax.ShapeDtypeStruct((b, dil, cls_len, MIX_W), F32)],
        compiler_params=_cparams("parallel", "parallel", "arbitrary"),
        name=f"dil_attn_{dil}",
    )(qkv, qkv, qkv, tables)
    to_seq = lambda a: a.transpose(0, 2, 1, 3).reshape(b, dil * cls_len, MIX_W)
    return to_seq(o), to_seq(lse)


def _dil_combine_kernel(o1, l1, o2, l2, o3, l3, out_ref):
    a1, a2, a3 = l1[0], l2[0], l3[0]
    m = jnp.maximum(jnp.maximum(a1, a2), a3)
    e1, e2, e3 = jnp.exp(a1 - m), jnp.exp(a2 - m), jnp.exp(a3 - m)
    tot = e1 + e2 + e3
    out = (e1 / tot) * o1[0] + (e2 / tot) * o2[0] + (e3 / tot) * o3[0]
    out_ref[0] = out.astype(out_ref.dtype)


def _dil_combine(branches, ts=512):
    b, s, w = branches[0][0].shape
    spec = pl.BlockSpec((1, ts, w), lambda bi, i: (bi, i, 0))
    flat = [a for pair in branches for a in pair]
    return pl.pallas_call(
        _dil_combine_kernel,
        grid=(b, s // ts),
        in_specs=[spec] * 6,
        out_specs=spec,
        out_shape=jax.ShapeDtypeStruct((b, s, w), BF16),
        compiler_params=_cparams("parallel", "parallel"),
        name="dil_combine",
    )(*flat)


def _shift_rows(x, k, fill):
    row = lax.broadcasted_iota(jnp.int32, x.shape, 0)
    return jnp.where(row >= k, pltpu.roll(x, k, 0), fill)


def _lru_kernel(xg_ref, cw_ref, cb_ref, wa_ref, ba_ref, wx_ref, bx_ref, lam_ref, o_ref):
    x = xg_ref[0, 0]
    gate = xg_ref[1, 0]
    s = x.shape[0]
    kw = cw_ref.shape[0]
    xc = cb_ref[...] + cw_ref[kw - 1:kw, :] * x
    for j in range(kw - 1):
        xc = xc + cw_ref[j:j + 1, :] * _shift_rows(x, kw - 1 - j, 0.0)
    xcb = xc.astype(BF16)
    r = jax.nn.sigmoid(_dot(xcb, wa_ref[...]) + ba_ref[...])
    i_gate = jax.nn.sigmoid(_dot(xcb, wx_ref[...]) + bx_ref[...])
    log_a = -LRU_C * r * _softplus(-lam_ref[...])
    a = jnp.exp(log_a)
    h = jnp.sqrt(1.0 - jnp.exp(2.0 * log_a)) * (i_gate * xc)
    k = 1
    while k < s:
        h = a * _shift_rows(h, k, 0.0) + h
        a = a * _shift_rows(a, k, 1.0)
        k *= 2
    o_ref[0] = (h * _gelu(gate)).astype(o_ref.dtype)


def _lru(xg, conv_w, conv_b, wa_bd, b_a, wx_bd, b_x, lam):
    _, b, s, w = xg.shape
    full = lambda a: pl.BlockSpec(a.shape, lambda bi: (0,) * a.ndim)
    args = (conv_w, conv_b, wa_bd, b_a, wx_bd, b_x, lam)
    return pl.pallas_call(
        _lru_kernel,
        grid=(b,),
        in_specs=[pl.BlockSpec((2, 1, s, w), lambda bi: (0, bi, 0, 0))] + [full(a) for a in args],
        out_specs=pl.BlockSpec((1, s, w), lambda bi: (bi, 0, 0)),
        out_shape=jax.ShapeDtypeStruct((b, s, w), BF16),
        compiler_params=_cparams("parallel"),
        name="rg_lru",
    )(xg, *args)


def _memkv_kernel(mem_ref, g_ref, wk_ref, wv_ref, k_ref, v_ref):
    mn = _rms(mem_ref[0], g_ref[...]).astype(BF16)
    k_ref[0] = _dot(mn, wk_ref[...]).astype(BF16)
    v_ref[0] = _dot(mn, wv_ref[...]).astype(BF16)


def _memkv(mem, g, w_ck, w_cv):
    b, m, d = mem.shape
    out = jax.ShapeDtypeStruct((b, m, MIX_W), BF16)
    return pl.pallas_call(
        _memkv_kernel,
        grid=(b,),
        in_specs=[
            pl.BlockSpec((1, m, d), lambda bi: (bi, 0, 0)),
            pl.BlockSpec((1, d), lambda bi: (0, 0)),
            pl.BlockSpec((d, MIX_W), lambda bi: (0, 0)),
            pl.BlockSpec((d, MIX_W), lambda bi: (0, 0)),
        ],
        out_specs=[pl.BlockSpec((1, m, MIX_W), lambda bi: (bi, 0, 0))] * 2,
        out_shape=[out, out],
        compiler_params=_cparams("parallel"),
        name="mem_kv",
    )(mem, g, w_ck, w_cv)


def _mix_cross_kernel(x_ref, sb_ref, fox_ref, dil_ref, lru_ref, wo_ref, g_ref, wq_ref,
                      km_ref, vm_ref, wco_ref, o_ref):
    x = x_ref[0]
    for j, ref in enumerate((sb_ref, fox_ref, dil_ref, lru_ref)):
        x = x + _dot(ref[0], wo_ref[j])
    t = x.shape[0]
    hn = _rms(x, g_ref[...]).astype(BF16)
    qs = _stack_heads(_dot(hn, wq_ref[...]))
    scores = _dot_nt(qs, km_ref[0])
    m = jnp.max(scores, axis=-1, keepdims=True)
    p = jnp.exp(scores - m)
    p = p / jnp.sum(p, axis=-1, keepdims=True)
    o = _unstack_heads(_dot(p.astype(BF16), vm_ref[0]), t)
    o_ref[0] = x + _dot(o.astype(BF16), wco_ref[...])


def _mix_cross(x, mixers, w_out, g, w_cq, k_mem, v_mem, w_co, ts=256):
    b, s, d = x.shape
    m = k_mem.shape[1]
    mix_spec = pl.BlockSpec((1, ts, MIX_W), lambda bi, i: (bi, i, 0))
    const = lambda a: pl.BlockSpec(a.shape, lambda bi, i: (0,) * a.ndim)
    mem_spec = pl.BlockSpec((1, m, MIX_W), lambda bi, i: (bi, 0, 0))
    return pl.pallas_call(
        _mix_cross_kernel,
        grid=(b, s // ts),
        in_specs=[pl.BlockSpec((1, ts, d), lambda bi, i: (bi, i, 0))] + [mix_spec] * 4
                 + [const(w_out), const(g), const(w_cq), mem_spec, mem_spec, const(w_co)],
        out_specs=pl.BlockSpec((1, ts, d), lambda bi, i: (bi, i, 0)),
        out_shape=jax.ShapeDtypeStruct((b, s, d), F32),
        compiler_params=_cparams("parallel", "parallel"),
        name="mix_cross",
    )(x, *mixers, w_out, g, w_cq, k_mem, v_mem, w_co)


HALO = 8


def _ffn_kernel(x_ref, halo_ref, g_ref, wup_ref, cw_ref, cb_ref, wdn_ref, gf_ref, o_ref,
                ext_ref, acc_ref, *, final_norm):
    i = pl.program_id(1)
    x = x_ref[0]
    t = x.shape[0]
    g = g_ref[...]
    halo = jnp.where(i > 0, _rms(halo_ref[0], g), 0.0)
    hn = jnp.concatenate([halo, _rms(x, g)], axis=0).astype(BF16)
    acc_ref[...] = jnp.zeros_like(acc_ref)

    def conv(c, slot):
        ext = ext_ref.at[slot]
        ext[...] = _dot(hn, wup_ref[c])
        w = cw_ref[c]
        return (cb_ref[c] + w[2:3] * ext[pl.ds(HALO, t), :]
                + w[1:2] * ext[pl.ds(HALO - 1, t), :]
                + w[0:1] * ext[pl.ds(HALO - 2, t), :])

    for c in range(N_FF_CHUNKS):
        up = conv(c, 2 * (c % 2))
        gate = conv(N_FF_CHUNKS + c, 2 * (c % 2) + 1)
        act = (_gelu(gate) * up).astype(BF16)
        acc_ref[...] += _dot(act, wdn_ref[c])
    out = x + acc_ref[...]
    if final_norm:
        out = _rms(out, gf_ref[...])
    o_ref[0] = out


def _ffn(x, g, w_up, conv_w, conv_b, w_down, g_final, final_norm, ts=256):
    b, s, d = x.shape
    per_tile = ts // HALO
    const = lambda a: pl.BlockSpec(a.shape, lambda bi, i: (0,) * a.ndim)
    return pl.pallas_call(
        functools.partial(_ffn_kernel, final_norm=final_norm),
        grid=(b, s // ts),
        in_specs=[
            pl.BlockSpec((1, ts, d), lambda bi, i: (bi, i, 0)),
            pl.BlockSpec((1, HALO, d), lambda bi, i: (bi, jnp.maximum(i * per_tile - 1, 0), 0)),
            const(g), const(w_up), const(conv_w), const(conv_b), const(w_down), const(g_final),
        ],
        out_specs=pl.BlockSpec((1, ts, d), lambda bi, i: (bi, i, 0)),
        out_shape=jax.ShapeDtypeStruct((b, s, d), F32),
        scratch_shapes=[
            pltpu.VMEM((4, HALO + ts, FF_CHUNK), F32),
            pltpu.VMEM((ts, d), F32),
        ],
        compiler_params=_cparams("parallel", "arbitrary"),
        name="conv_geglu",
    )(x, x, g, w_up, conv_w, conv_b, w_down, g_final)


def _block_diag(w):
    g, n, _ = w.shape
    eye = jnp.eye(g, dtype=w.dtype)
    return (eye[:, None, :, None] * w[:, :, None, :]).reshape(g * n, g * n)


def kernel(x, mem, norm_mix_g, w_in, b_forget, lru_conv_w, lru_conv_b, lru_w_a, lru_b_a, lru_w_x, lru_b_x, lru_lambda, w_out, norm_cross_g, norm_mem_g, w_cq, w_ck, w_cv, w_co, norm_ffn_g, w_up, ffn_conv_w, ffn_conv_b, w_down, rel_bias, final_norm_g):
    depth = w_in.shape[0]
    qkv_w = 3 * MIX_W
    q_scale = HEAD_DIM ** -0.5
    col_scale = jnp.ones((3, 3, MIX_W), F32).at[:, 0, :].set(q_scale).reshape(-1)
    col_scale = jnp.concatenate([col_scale, jnp.ones((2 * MIX_W,), F32)])
    tables = _bias_tables(rel_bias)
    row = lambda v: v.reshape(1, -1)

    for l in range(depth):
        w = w_in[l]
        n_f = N_HEADS
        w_main = jnp.concatenate([w[:, :2 * qkv_w], w[:, 2 * qkv_w + n_f:]], axis=1)
        w_main = (w_main * col_scale).astype(BF16)
        w_f = jnp.zeros((8, D_MODEL), F32).at[:n_f].set(w[:, 2 * qkv_w:2 * qkv_w + n_f].T).astype(BF16)
        b_f = jnp.zeros((8, 1), F32).at[:n_f, 0].set(b_forget[l])

        sb_qkv, fox_qkv, *dil_qkvs, lru_xg, f_t = _inproj(x, row(norm_mix_g[l]), w_main, w_f)
        o_sb = _sb_attention(sb_qkv)
        o_fox = _fox_attention(fox_qkv, _decay(f_t, b_f))
        o_dil = _dil_combine([_dil_branch(qkv, tables, br) for br, qkv in enumerate(dil_qkvs)])
        o_lru = _lru(lru_xg, lru_conv_w[l], row(lru_conv_b[l]),
                     _block_diag(lru_w_a[l]).astype(BF16), row(lru_b_a[l]),
                     _block_diag(lru_w_x[l]).astype(BF16), row(lru_b_x[l]), row(lru_lambda[l]))

        k_mem, v_mem = _memkv(mem, row(norm_mem_g[l]), w_ck[l].astype(BF16), w_cv[l].astype(BF16))
        x = _mix_cross(x, (o_sb, o_fox, o_dil, o_lru),
                       w_out[l].reshape(4, MIX_W, D_MODEL).astype(BF16), row(norm_cross_g[l]),
                       (w_cq[l] * q_scale).astype(BF16), k_mem, v_mem, w_co[l].astype(BF16))

        wu = w_up[l].reshape(D_MODEL, 2 * N_FF_CHUNKS, FF_CHUNK).transpose(1, 0, 2).astype(BF16)
        cw = ffn_conv_w[l].reshape(-1, 2 * N_FF_CHUNKS, FF_CHUNK).transpose(1, 0, 2)
        cb = ffn_conv_b[l].reshape(2 * N_FF_CHUNKS, 1, FF_CHUNK)
        wd = w_down[l].reshape(N_FF_CHUNKS, FF_CHUNK, D_MODEL).astype(BF16)
        x = _ffn(x, row(norm_ffn_g[l]), wu, cw, cb, wd, row(final_norm_g), l == depth - 1)
    return x
```

```python
import functools
import math

import jax
import jax.numpy as jnp
from jax import lax
from jax.experimental import pallas as pl
from jax.experimental.pallas import tpu as pltpu

F32 = jnp.float32
BF16 = jnp.bfloat16

LANES = 128
D_MODEL = 1024
HEAD_DIM = 64
N_HEADS = 4
MIX_W = N_HEADS * HEAD_DIM
D_FF = 2816
FF_CHUNK = 256
N_FF_CHUNKS = D_FF // FF_CHUNK
N_MEM = 256
NUM_BUCKETS = 32
MAX_DISTANCE = 2048
DILATIONS = (1, 4, 16)
DIL_STEPS = 128
LRU_C = 8.0
EPS = 1e-6
NEG_BIG = -1e30

TQ = 128
TQC = 256
TK = 256
VMEM_LIMIT = 56 * 1024 * 1024

NT_DIMS = (((1,), (1,)), ((), ()))


def _cparams(*sem):
    return pltpu.CompilerParams(dimension_semantics=sem, vmem_limit_bytes=VMEM_LIMIT)


def _rms(x, g):
    return x * lax.rsqrt(jnp.mean(x * x, axis=-1, keepdims=True) + EPS) * g


def _softplus(x):
    return jnp.maximum(x, 0.0) + jnp.log(1.0 + jnp.exp(-jnp.abs(x)))


def _gelu(x):
    return 0.5 * x * (1.0 + lax.erf(x * math.sqrt(0.5)))


def _dot(a, b):
    return jnp.dot(a, b, preferred_element_type=F32)


def _dot_nt(a, b):
    return lax.dot_general(a, b, NT_DIMS, preferred_element_type=F32)


def _split2(x):
    hi = x.astype(BF16)
    lo = (x - hi.astype(F32)).astype(BF16)
    return hi, lo


def _head_of_lane(rows):
    return lax.broadcasted_iota(jnp.int32, (rows, MIX_W), 1) // HEAD_DIM


def _stack_heads(q):
    t = q.shape[0]
    head = _head_of_lane(t)
    qf = q.astype(F32)
    return jnp.concatenate([jnp.where(head == h, qf, 0.0) for h in range(N_HEADS)], axis=0).astype(BF16)


def _unstack_heads(o, t):
    head = _head_of_lane(t)
    out = jnp.where(head == 0, o[0:t], 0.0)
    for h in range(1, N_HEADS):
        out = jnp.where(head == h, o[h * t:(h + 1) * t], out)
    return out


def _unstack_cols(c, t):
    head = _head_of_lane(t)
    out = jnp.broadcast_to(c[0:t], (t, MIX_W))
    for h in range(1, N_HEADS):
        out = jnp.where(head == h, jnp.broadcast_to(c[h * t:(h + 1) * t], (t, MIX_W)), out)
    return out


def _inproj_kernel(x_ref, g_ref, w_ref, wf_ref, sb_ref, fox_ref, d1_ref, d4_ref, d16_ref,
                   lru_ref, f_ref, half_ref):
    hn = _rms(x_ref[0], g_ref[...]).astype(BF16)
    ts = hn.shape[0]
    for m, ref in enumerate((sb_ref, fox_ref)):
        for j in range(3):
            c = (3 * m + j) * MIX_W
            ref[j, 0] = _dot(hn, w_ref[:, c:c + MIX_W]).astype(BF16)
    for j in range(3):
        c = (6 + j) * MIX_W
        res = _dot(hn, w_ref[:, c:c + MIX_W])
        d1_ref[j, 0, 0] = res.astype(BF16)
        for h in range(2):
            half_ref[h] = res[:, h * LANES:(h + 1) * LANES]
        for ref, dil in ((d4_ref, DILATIONS[1]), (d16_ref, DILATIONS[2])):
            for r in range(dil):
                for h in range(2):
                    ref[j, 0, r, :, h * LANES:(h + 1) * LANES] = (
                        half_ref[h, pl.ds(r, ts // dil, stride=dil), :].astype(BF16))
    for j in range(2):
        c = (9 + j) * MIX_W
        lru_ref[j, 0] = _dot(hn, w_ref[:, c:c + MIX_W])
    f_ref[0] = _dot_nt(wf_ref[...], hn)


def _inproj(x, g, w_main, w_f, ts=512):
    b, s, d = x.shape
    qkv_shape = jax.ShapeDtypeStruct((3, b, s, MIX_W), BF16)
    qkv_spec = pl.BlockSpec((3, 1, ts, MIX_W), lambda bi, i: (0, bi, i, 0))
    cls_shapes = [jax.ShapeDtypeStruct((3, b, dil, s // dil, MIX_W), BF16) for dil in DILATIONS]
    cls_specs = [pl.BlockSpec((3, 1, dil, ts // dil, MIX_W), lambda bi, i: (0, bi, 0, i, 0))
                 for dil in DILATIONS]
    return pl.pallas_call(
        _inproj_kernel,
        grid=(b, s // ts),
        in_specs=[
            pl.BlockSpec((1, ts, d), lambda bi, i: (bi, i, 0)),
            pl.BlockSpec((1, d), lambda bi, i: (0, 0)),
            pl.BlockSpec(w_main.shape, lambda bi, i: (0, 0)),
            pl.BlockSpec(w_f.shape, lambda bi, i: (0, 0)),
        ],
        out_specs=[
            qkv_spec, qkv_spec, *cls_specs,
            pl.BlockSpec((2, 1, ts, MIX_W), lambda bi, i: (0, bi, i, 0)),
            pl.BlockSpec((1, 8, ts), lambda bi, i: (bi, 0, i)),
        ],
        out_shape=[
            qkv_shape, qkv_shape, *cls_shapes,
            jax.ShapeDtypeStruct((2, b, s, MIX_W), F32),
            jax.ShapeDtypeStruct((b, 8, s), F32),
        ],
        scratch_shapes=[pltpu.VMEM((2, ts, LANES), F32)],
        compiler_params=_cparams("parallel", "parallel"),
        name="inproj",
    )(x, g, w_main, w_f)


def _decay_kernel(f_ref, bf_ref, tri_ref, o_ref):
    s = f_ref.shape[2]
    tri = tri_ref[...]
    carry = jnp.zeros((8, 1), F32)
    for c in range(s // 128):
        sl = slice(c * 128, (c + 1) * 128)
        log_f = -_softplus(-(f_ref[0, :, sl] + bf_ref[...]))
        p1 = log_f.astype(BF16)
        r1 = log_f - p1.astype(F32)
        p2 = r1.astype(BF16)
        p3 = (r1 - p2.astype(F32)).astype(BF16)
        inc = _dot(p1, tri) + _dot(p2, tri) + _dot(p3, tri)
        o_ref[0, :, sl] = inc + carry
        carry = carry + jnp.sum(log_f, axis=-1, keepdims=True)


def _decay(f_t, b_f):
    b, _, s = f_t.shape
    tri = (lax.broadcasted_iota(jnp.int32, (128, 128), 0)
           <= lax.broadcasted_iota(jnp.int32, (128, 128), 1)).astype(BF16)
    return pl.pallas_call(
        _decay_kernel,
        grid=(b,),
        in_specs=[
            pl.BlockSpec((1, 8, s), lambda bi: (bi, 0, 0)),
            pl.BlockSpec((8, 1), lambda bi: (0, 0)),
            pl.BlockSpec((128, 128), lambda bi: (0, 0)),
        ],
        out_specs=pl.BlockSpec((1, 8, s), lambda bi: (bi, 0, 0)),
        out_shape=jax.ShapeDtypeStruct((b, 8, s), F32),
        compiler_params=_cparams("parallel"),
        name="fox_decay",
    )(f_t, b_f, tri)


def _block_positions(i, ks):
    row = lax.broadcasted_iota(jnp.int32, (N_HEADS * TQC, TK), 0) % TQC + i * TQC
    col = lax.broadcasted_iota(jnp.int32, (N_HEADS * TQC, TK), 1) + ks
    return row, col


SB_SATURATED = 110.0


def _sb_kernel(q_ref, k_ref, v_ref, uo_ref, o_ref, acc_ref, car_ref):
    i = pl.program_id(1)
    qs = _stack_heads(q_ref[0, 0])
    k_diag = (i * TQC) // TK

    def block(kb, diagonal):
        ks = pl.multiple_of(kb * TK, TK)
        z = _dot_nt(qs, k_ref[0, 0, pl.ds(ks, TK), :])
        drop = _softplus(z)
        if diagonal:
            row, col = _block_positions(i, ks)
            strict = col < row
            cum_in = jnp.where(strict, drop, 0.0).astype(BF16)
        else:
            cum_in = drop.astype(BF16)
        sums = _dot(cum_in, uo_ref[...])
        if diagonal:
            att = jnp.where(strict, jnp.exp(z - drop - sums[:, :TK]), 0.0)
            car_ref[...] = sums[:, TK:]
            acc_ref[...] = _dot(att.astype(BF16), v_ref[0, 0, pl.ds(ks, TK), :])
        else:
            car = car_ref[...]
            att = jnp.exp(z - drop - sums[:, :TK] - car)
            car_ref[...] = car + sums[:, TK:]
            acc_ref[...] += _dot(att.astype(BF16), v_ref[0, 0, pl.ds(ks, TK), :])
        return jnp.min(car_ref[...]) < SB_SATURATED

    def cond(state):
        kb, live = state
        return jnp.logical_and(kb >= 0, live)

    def body(state):
        kb, _ = state
        return kb - 1, block(kb, False)

    lax.while_loop(cond, body, (k_diag - 1, block(k_diag, True)))
    o_ref[0] = _unstack_heads(acc_ref[...], TQC).astype(o_ref.dtype)


def _sb_attention(qkv):
    _, b, s, _ = qkv.shape
    later = (lax.broadcasted_iota(jnp.int32, (TK, TK), 0)
             > lax.broadcasted_iota(jnp.int32, (TK, TK), 1))
    u = jnp.concatenate([later, jnp.ones((TK, TK), bool)], axis=1).astype(BF16)
    return pl.pallas_call(
        _sb_kernel,
        grid=(b, s // TQC),
        in_specs=[
            pl.BlockSpec((1, 1, TQC, MIX_W), lambda bi, i: (0, bi, i, 0)),
            pl.BlockSpec((1, 1, s, MIX_W), lambda bi, i: (1, bi, 0, 0)),
            pl.BlockSpec((1, 1, s, MIX_W), lambda bi, i: (2, bi, 0, 0)),
            pl.BlockSpec((TK, 2 * TK), lambda bi, i: (0, 0)),
        ],
        out_specs=pl.BlockSpec((1, TQC, MIX_W), lambda bi, i: (bi, i, 0)),
        out_shape=jax.ShapeDtypeStruct((b, s, MIX_W), BF16),
        scratch_shapes=[
            pltpu.VMEM((N_HEADS * TQC, MIX_W), F32),
            pltpu.VMEM((N_HEADS * TQC, TK), F32),
        ],
        compiler_params=_cparams("parallel", "arbitrary"),
        name="sb_attn",
    )(qkv, qkv, qkv, u)


def _fox_kernel(q_ref, k_ref, v_ref, cum_ref, o_ref, z_ref, mx_ref, l_ref, acc_ref):
    i = pl.program_id(1)
    qs = _stack_heads(q_ref[0, 0])
    k_diag = (i * TQC) // TK

    def logits(kb):
        ks = pl.multiple_of(kb * TK, TK)
        cum = cum_ref[0, :, pl.ds(ks, TK)]
        decay = jnp.concatenate(
            [jnp.broadcast_to(cum[h:h + 1], (TQC, TK)) for h in range(N_HEADS)], axis=0)
        return _dot_nt(qs, k_ref[0, 0, pl.ds(ks, TK), :]) - decay, ks

    z, ks = logits(k_diag)
    row, col = _block_positions(i, ks)
    z = jnp.where(col <= row, z, NEG_BIG)
    z_ref[k_diag] = z
    mx_ref[...] = z

    def pass1(kb, carry):
        z, _ = logits(kb)
        z_ref[kb] = z
        mx_ref[...] = jnp.maximum(mx_ref[...], z)
        return carry

    lax.fori_loop(0, k_diag, pass1, 0)
    m = jnp.max(mx_ref[...], axis=-1, keepdims=True)
    mx_ref[...] = jnp.broadcast_to(m, mx_ref.shape)

    l_ref[...] = jnp.zeros_like(l_ref)
    acc_ref[...] = jnp.zeros_like(acc_ref)

    def pass2(kb, carry):
        ks = pl.multiple_of(kb * TK, TK)
        p = jnp.exp(z_ref[kb] - mx_ref[...])
        l_ref[...] += p
        acc_ref[...] += _dot(p.astype(BF16), v_ref[0, 0, pl.ds(ks, TK), :])
        return carry

    lax.fori_loop(0, k_diag + 1, pass2, 0)
    l = jnp.sum(l_ref[...], axis=-1, keepdims=True)
    o_ref[0] = _unstack_heads(acc_ref[...] / l, TQC).astype(o_ref.dtype)


def _fox_attention(qkv, cum):
    _, b, s, _ = qkv.shape
    blk = (N_HEADS * TQC, TK)
    return pl.pallas_call(
        _fox_kernel,
        grid=(b, s // TQC),
        in_specs=[
            pl.BlockSpec((1, 1, TQC, MIX_W), lambda bi, i: (0, bi, i, 0)),
            pl.BlockSpec((1, 1, s, MIX_W), lambda bi, i: (1, bi, 0, 0)),
            pl.BlockSpec((1, 1, s, MIX_W), lambda bi, i: (2, bi, 0, 0)),
            pl.BlockSpec((1, 8, s), lambda bi, i: (bi, 0, 0)),
        ],
        out_specs=pl.BlockSpec((1, TQC, MIX_W), lambda bi, i: (bi, i, 0)),
        out_shape=jax.ShapeDtypeStruct((b, s, MIX_W), BF16),
        scratch_shapes=[
            pltpu.VMEM((s // TK,) + blk, F32),
            pltpu.VMEM(blk, F32),
            pltpu.VMEM(blk, F32),
            pltpu.VMEM((N_HEADS * TQC, MIX_W), F32),
        ],
        compiler_params=_cparams("parallel", "arbitrary"),
        name="fox_attn",
    )(qkv, qkv, qkv, cum)


def _t5_bucket(dist):
    n = jnp.maximum(dist, 0)
    max_exact = NUM_BUCKETS // 2
    nf = jnp.maximum(n, 1).astype(F32)
    large = max_exact + (jnp.log(nf / max_exact) / math.log(MAX_DISTANCE / max_exact)
                         * (NUM_BUCKETS - max_exact)).astype(jnp.int32)
    large = jnp.minimum(large, NUM_BUCKETS - 1)
    return jnp.where(n < max_exact, n, large)


def _bias_kernel(idx_ref, rb_ref, o_ref):
    for d in range(len(DILATIONS)):
        for variant in range(2):
            idx = idx_ref[d, variant]
            for h in range(N_HEADS):
                acc = jnp.full(idx.shape, NEG_BIG, F32)
                for bucket in range(NUM_BUCKETS):
                    acc = jnp.where(idx == bucket, rb_ref[bucket, h], acc)
                o_ref[d, variant, h] = acc


def _bias_tables(rel_bias):
    t = jnp.arange(TQ)[:, None]
    c = jnp.arange(2 * TQ)[None, :]
    delta = TQ + t - c
    valid = (delta >= 0) & (delta <= DIL_STEPS)
    idx = jnp.stack([jnp.where(valid, _t5_bucket(delta * dil), -1) for dil in DILATIONS])
    idx_first = jnp.concatenate([idx[:, :, TQ:], jnp.full_like(idx[:, :, TQ:], -1)], axis=-1)
    idx = jnp.stack([idx, idx_first], axis=1)
    out = pl.pallas_call(
        _bias_kernel,
        in_specs=[
            pl.BlockSpec(memory_space=pltpu.VMEM),
            pl.BlockSpec(memory_space=pltpu.SMEM),
        ],
        out_specs=pl.BlockSpec(memory_space=pltpu.VMEM),
        out_shape=jax.ShapeDtypeStruct((len(DILATIONS), 2, N_HEADS, TQ, 2 * TQ), F32),
        name="dil_bias",
    )(idx.astype(jnp.int32), rel_bias)
    return out.reshape(len(DILATIONS), 2, N_HEADS * TQ, 2 * TQ)


DIL_GROUP = 4


def _dil_kernel(q_ref, k_ref, v_ref, t_ref, o_ref, lse_ref, *, by_class, n_groups):
    g = pl.program_id(2)

    def attend(jj, scores, v_win):
        m = jnp.max(scores, axis=-1, keepdims=True)
        p = jnp.exp(scores - m)
        l = jnp.sum(p, axis=-1, keepdims=True)
        o = _unstack_heads(_dot(p.astype(BF16), v_win) / l, TQ).astype(o_ref.dtype)
        lse = _unstack_cols(m + jnp.log(l), TQ)
        if by_class:
            o_ref[0, jj] = o
            lse_ref[0, jj] = lse
        else:
            o_ref[0, 0, jj * TQ:(jj + 1) * TQ] = o
            lse_ref[0, 0, jj * TQ:(jj + 1) * TQ] = lse

    pending = []
    for jj in range(DIL_GROUP):
        if by_class:
            scores = (_dot_nt(_stack_heads(q_ref[0, 0, jj]), k_ref[0, 0, jj])
                      + t_ref[0, 0, :, TQ:2 * TQ])
            pending.append((scores, v_ref[0, 0, jj]))
            continue
        qs = _stack_heads(q_ref[0, 0, 0, jj * TQ:(jj + 1) * TQ])
        if n_groups == 1 or jj > 0:
            n = g * DIL_GROUP + jj if n_groups > 1 else jj
            start = (n - 1) * TQ if jj > 0 else 0
            bias = t_ref[0, 0 if jj > 0 else 1]
        else:
            start = jnp.maximum(g * DIL_GROUP - 1, 0) * TQ
            bias = t_ref[0, (g == 0).astype(jnp.int32)]
        if not isinstance(start, int):
            start = pl.multiple_of(start, TQ)
        window = pl.ds(start, 2 * TQ)
        pending.append((_dot_nt(qs, k_ref[0, 0, 0, window, :]) + bias, v_ref[0, 0, 0, window, :]))
    for jj, (scores, v_win) in enumerate(pending):
        attend(jj, scores, v_win)


def _dil_branch(qkv, tables, branch):
    _, b, dil, cls_len, _ = qkv.shape
    n_blocks = cls_len // TQ
    by_class = n_blocks == 1
    if by_class:
        grid = (b, dil // DIL_GROUP, 1)
        qkv_specs = [pl.BlockSpec((1, 1, DIL_GROUP, TQ, MIX_W), functools.partial(
            lambda bi, c, g, j: (j, bi, c, 0, 0), j=j)) for j in range(3)]
        out_spec = pl.BlockSpec((1, DIL_GROUP, TQ, MIX_W), lambda bi, c, g: (bi, c, 0, 0))
    else:
        grid = (b, dil, n_blocks // DIL_GROUP)
        qkv_specs = [pl.BlockSpec((1, 1, 1, DIL_GROUP * TQ, MIX_W), lambda bi, r, g: (0, bi, r, g, 0))]
        qkv_specs += [pl.BlockSpec((1, 1, 1, cls_len, MIX_W), functools.partial(
            lambda bi, r, g, j: (j, bi, r, 0, 0), j=j)) for j in (1, 2)]
        out_spec = pl.BlockSpec((1, 1, DIL_GROUP * TQ, MIX_W), lambda bi, r, g: (bi, r, g, 0))
    o, lse = pl.pallas_call(
        functools.partial(_dil_kernel, by_class=by_class, n_groups=grid[2]),
        grid=grid,
        in_specs=qkv_specs + [
            pl.BlockSpec((1, 2, N_HEADS * TQ, 2 * TQ), lambda bi, r, g: (branch, 0, 0, 0))],
        out_specs=[out_spec, out_spec],
        out_shape=[jax.ShapeDtypeStruct((b, dil, cls_len, MIX_W), BF16),
                   jax.ShapeDtypeStruct((b, dil, cls_len, MIX_W), F32)],
        compiler_params=_cparams("parallel", "parallel", "arbitrary"),
        name=f"dil_attn_{dil}",
    )(qkv, qkv, qkv, tables)
    to_seq = lambda a: a.transpose(0, 2, 1, 3).reshape(b, dil * cls_len, MIX_W)
    return to_seq(o), to_seq(lse)


def _dil_combine_kernel(o1, l1, o2, l2, o3, l3, out_ref):
    a1, a2, a3 = l1[0], l2[0], l3[0]
    m = jnp.maximum(jnp.maximum(a1, a2), a3)
    e1, e2, e3 = jnp.exp(a1 - m), jnp.exp(a2 - m), jnp.exp(a3 - m)
    tot = e1 + e2 + e3
    out = (e1 / tot) * o1[0] + (e2 / tot) * o2[0] + (e3 / tot) * o3[0]
    out_ref[0] = out.astype(out_ref.dtype)


def _dil_combine(branches, ts=512):
    b, s, w = branches[0][0].shape
    spec = pl.BlockSpec((1, ts, w), lambda bi, i: (bi, i, 0))
    flat = [a for pair in branches for a in pair]
    return pl.pallas_call(
        _dil_combine_kernel,
        grid=(b, s // ts),
        in_specs=[spec] * 6,
        out_specs=spec,
        out_shape=jax.ShapeDtypeStruct((b, s, w), BF16),
        compiler_params=_cparams("parallel", "parallel"),
        name="dil_combine",
    )(*flat)


def _shift_rows(x, k, fill):
    row = lax.broadcasted_iota(jnp.int32, x.shape, 0)
    return jnp.where(row >= k, pltpu.roll(x, k, 0), fill)


def _lru_kernel(xg_ref, cw_ref, cb_ref, wa_ref, ba_ref, wx_ref, bx_ref, lam_ref, o_ref):
    x = xg_ref[0, 0]
    gate = xg_ref[1, 0]
    s = x.shape[0]
    kw = cw_ref.shape[0]
    xc = cb_ref[...] + cw_ref[kw - 1:kw, :] * x
    for j in range(kw - 1):
        xc = xc + cw_ref[j:j + 1, :] * _shift_rows(x, kw - 1 - j, 0.0)
    xcb = xc.astype(BF16)
    r = jax.nn.sigmoid(_dot(xcb, wa_ref[...]) + ba_ref[...])
    i_gate = jax.nn.sigmoid(_dot(xcb, wx_ref[...]) + bx_ref[...])
    log_a = -LRU_C * r * _softplus(-lam_ref[...])
    a = jnp.exp(log_a)
    h = jnp.sqrt(1.0 - jnp.exp(2.0 * log_a)) * (i_gate * xc)
    k = 1
    while k < s:
        h = a * _shift_rows(h, k, 0.0) + h
        a = a * _shift_rows(a, k, 1.0)
        k *= 2
    o_ref[0] = (h * _gelu(gate)).astype(o_ref.dtype)


def _lru(xg, conv_w, conv_b, wa_bd, b_a, wx_bd, b_x, lam):
    _, b, s, w = xg.shape
    full = lambda a: pl.BlockSpec(a.shape, lambda bi: (0,) * a.ndim)
    args = (conv_w, conv_b, wa_bd, b_a, wx_bd, b_x, lam)
    return pl.pallas_call(
        _lru_kernel,
        grid=(b,),
        in_specs=[pl.BlockSpec((2, 1, s, w), lambda bi: (0, bi, 0, 0))] + [full(a) for a in args],
        out_specs=pl.BlockSpec((1, s, w), lambda bi: (bi, 0, 0)),
        out_shape=jax.ShapeDtypeStruct((b, s, w), BF16),
        compiler_params=_cparams("parallel"),
        name="rg_lru",
    )(xg, *args)


def _memkv_kernel(mem_ref, g_ref, wk_ref, wv_ref, k_ref, v_ref):
    mn = _rms(mem_ref[0], g_ref[...]).astype(BF16)
    k_ref[0] = _dot(mn, wk_ref[...]).astype(BF16)
    v_ref[0] = _dot(mn, wv_ref[...]).astype(BF16)


def _memkv(mem, g, w_ck, w_cv):
    b, m, d = mem.shape
    out = jax.ShapeDtypeStruct((b, m, MIX_W), BF16)
    return pl.pallas_call(
        _memkv_kernel,
        grid=(b,),
        in_specs=[
            pl.BlockSpec((1, m, d), lambda bi: (bi, 0, 0)),
            pl.BlockSpec((1, d), lambda bi: (0, 0)),
            pl.BlockSpec((d, MIX_W), lambda bi: (0, 0)),
            pl.BlockSpec((d, MIX_W), lambda bi: (0, 0)),
        ],
        out_specs=[pl.BlockSpec((1, m, MIX_W), lambda bi: (bi, 0, 0))] * 2,
        out_shape=[out, out],
        compiler_params=_cparams("parallel"),
        name="mem_kv",
    )(mem, g, w_ck, w_cv)


def _mix_cross_kernel(x_ref, sb_ref, fox_ref, dil_ref, lru_ref, wo_ref, g_ref, wq_ref,
                      km_ref, vm_ref, wco_ref, o_ref):
    t = MIX_SUB
    subs = [slice(u * t, (u + 1) * t) for u in range(x_ref.shape[1] // t)]
    xs = []
    for r in subs:
        x = x_ref[0, r]
        for j, ref in enumerate((sb_ref, fox_ref, dil_ref, lru_ref)):
            x = x + _dot(ref[0, r], wo_ref[j])
        xs.append(x)
    qs = [_stack_heads(_dot(_rms(x, g_ref[...]).astype(BF16), wq_ref[...])) for x in xs]
    scores = [_dot_nt(q, km_ref[0]) for q in qs]
    outs = []
    for sc in scores:
        p = jnp.exp(sc - jnp.max(sc, axis=-1, keepdims=True))
        p = p / jnp.sum(p, axis=-1, keepdims=True)
        outs.append(_unstack_heads(_dot(p.astype(BF16), vm_ref[0]), t))
    for r, x, o in zip(subs, xs, outs):
        o_ref[0, r] = x + _dot(o.astype(BF16), wco_ref[...])


MIX_SUB = 256


def _mix_cross(x, mixers, w_out, g, w_cq, k_mem, v_mem, w_co, ts=512):
    b, s, d = x.shape
    m = k_mem.shape[1]
    mix_spec = pl.BlockSpec((1, ts, MIX_W), lambda bi, i: (bi, i, 0))
    const = lambda a: pl.BlockSpec(a.shape, lambda bi, i: (0,) * a.ndim)
    mem_spec = pl.BlockSpec((1, m, MIX_W), lambda bi, i: (bi, 0, 0))
    return pl.pallas_call(
        _mix_cross_kernel,
        grid=(b, s // ts),
        in_specs=[pl.BlockSpec((1, ts, d), lambda bi, i: (bi, i, 0))] + [mix_spec] * 4
                 + [const(w_out), const(g), const(w_cq), mem_spec, mem_spec, const(w_co)],
        out_specs=pl.BlockSpec((1, ts, d), lambda bi, i: (bi, i, 0)),
        out_shape=jax.ShapeDtypeStruct((b, s, d), F32),
        compiler_params=_cparams("parallel", "parallel"),
        name="mix_cross",
    )(x, *mixers, w_out, g, w_cq, k_mem, v_mem, w_co)


HALO = 8


def _ffn_kernel(x_ref, halo_ref, g_ref, wup_ref, cw_ref, cb_ref, wdn_ref, gf_ref, o_ref,
                ext_ref, acc_ref, *, final_norm):
    i = pl.program_id(1)
    x = x_ref[0]
    t = x.shape[0]
    g = g_ref[...]
    halo = jnp.where(i > 0, _rms(halo_ref[0], g), 0.0)
    hn = jnp.concatenate([halo, _rms(x, g)], axis=0).astype(BF16)
    acc_ref[...] = jnp.zeros_like(acc_ref)

    def project(c):
        ext_ref[2 * (c % 2)] = _dot(hn, wup_ref[c])
        ext_ref[2 * (c % 2) + 1] = _dot(hn, wup_ref[N_FF_CHUNKS + c])

    def conv(c, slot):
        ext = ext_ref.at[slot]
        w = cw_ref[c]
        return (cb_ref[c] + w[2:3] * ext[pl.ds(HALO, t), :]
                + w[1:2] * ext[pl.ds(HALO - 1, t), :]
                + w[0:1] * ext[pl.ds(HALO - 2, t), :])

    project(0)
    for c in range(N_FF_CHUNKS):
        if c + 1 < N_FF_CHUNKS:
            project(c + 1)
        up = conv(c, 2 * (c % 2))
        gate = conv(N_FF_CHUNKS + c, 2 * (c % 2) + 1)
        act = (_gelu(gate) * up).astype(BF16)
        acc_ref[...] += _dot(act, wdn_ref[c])
    out = x + acc_ref[...]
    if final_norm:
        out = _rms(out, gf_ref[...])
    o_ref[0] = out


def _ffn(x, g, w_up, conv_w, conv_b, w_down, g_final, final_norm, ts=256):
    b, s, d = x.shape
    per_tile = ts // HALO
    const = lambda a: pl.BlockSpec(a.shape, lambda bi, i: (0,) * a.ndim)
    return pl.pallas_call(
        functools.partial(_ffn_kernel, final_norm=final_norm),
        grid=(b, s // ts),
        in_specs=[
            pl.BlockSpec((1, ts, d), lambda bi, i: (bi, i, 0)),
            pl.BlockSpec((1, HALO, d), lambda bi, i: (bi, jnp.maximum(i * per_tile - 1, 0), 0)),
            const(g), const(w_up), const(conv_w), const(conv_b), const(w_down), const(g_final),
        ],
        out_specs=pl.BlockSpec((1, ts, d), lambda bi, i: (bi, i, 0)),
        out_shape=jax.ShapeDtypeStruct((b, s, d), F32),
        scratch_shapes=[
            pltpu.VMEM((4, HALO + ts, FF_CHUNK), F32),
            pltpu.VMEM((ts, d), F32),
        ],
        compiler_params=_cparams("parallel", "arbitrary"),
        name="conv_geglu",
    )(x, x, g, w_up, conv_w, conv_b, w_down, g_final)


def _block_diag(w):
    g, n, _ = w.shape
    eye = jnp.eye(g, dtype=w.dtype)
    return (eye[:, None, :, None] * w[:, :, None, :]).reshape(g * n, g * n)


def kernel(x, mem, norm_mix_g, w_in, b_forget, lru_conv_w, lru_conv_b, lru_w_a, lru_b_a, lru_w_x, lru_b_x, lru_lambda, w_out, norm_cross_g, norm_mem_g, w_cq, w_ck, w_cv, w_co, norm_ffn_g, w_up, ffn_conv_w, ffn_conv_b, w_down, rel_bias, final_norm_g):
    depth = w_in.shape[0]
    qkv_w = 3 * MIX_W
    q_scale = HEAD_DIM ** -0.5
    col_scale = jnp.ones((3, 3, MIX_W), F32).at[:, 0, :].set(q_scale).reshape(-1)
    col_scale = jnp.concatenate([col_scale, jnp.ones((2 * MIX_W,), F32)])
    tables = _bias_tables(rel_bias)
    row = lambda v: v.reshape(1, -1)

    for l in range(depth):
        w = w_in[l]
        n_f = N_HEADS
        w_main = jnp.concatenate([w[:, :2 * qkv_w], w[:, 2 * qkv_w + n_f:]], axis=1)
        w_main = (w_main * col_scale).astype(BF16)
        w_f = jnp.zeros((8, D_MODEL), F32).at[:n_f].set(w[:, 2 * qkv_w:2 * qkv_w + n_f].T).astype(BF16)
        b_f = jnp.zeros((8, 1), F32).at[:n_f, 0].set(b_forget[l])

        sb_qkv, fox_qkv, *dil_qkvs, lru_xg, f_t = _inproj(x, row(norm_mix_g[l]), w_main, w_f)
        o_sb = _sb_attention(sb_qkv)
        o_fox = _fox_attention(fox_qkv, _decay(f_t, b_f))
        o_dil = _dil_combine([_dil_branch(qkv, tables, br) for br, qkv in enumerate(dil_qkvs)])
        o_lru = _lru(lru_xg, lru_conv_w[l], row(lru_conv_b[l]),
                     _block_diag(lru_w_a[l]).astype(BF16), row(lru_b_a[l]),
                     _block_diag(lru_w_x[l]).astype(BF16), row(lru_b_x[l]), row(lru_lambda[l]))

        k_mem, v_mem = _memkv(mem, row(norm_mem_g[l]), w_ck[l].astype(BF16), w_cv[l].astype(BF16))
        x = _mix_cross(x, (o_sb, o_fox, o_dil, o_lru),
                       w_out[l].reshape(4, MIX_W, D_MODEL).astype(BF16), row(norm_cross_g[l]),
                       (w_cq[l] * q_scale).astype(BF16), k_mem, v_mem, w_co[l].astype(BF16))

        wu = w_up[l].reshape(D_MODEL, 2 * N_FF_CHUNKS, FF_CHUNK).transpose(1, 0, 2).astype(BF16)
        cw = ffn_conv_w[l].reshape(-1, 2 * N_FF_CHUNKS, FF_CHUNK).transpose(1, 0, 2)
        cb = ffn_conv_b[l].reshape(2 * N_FF_CHUNKS, 1, FF_CHUNK)
        wd = w_down[l].reshape(N_FF_CHUNKS, FF_CHUNK, D_MODEL).astype(BF16)
        x = _ffn(x, row(norm_ffn_g[l]), wu, cw, cb, wd, row(final_norm_g), l == depth - 1)
    return x
```

```python
import functools
import math

import jax
import jax.numpy as jnp
from jax import lax
from jax.experimental import pallas as pl
from jax.experimental.pallas import tpu as pltpu

F32 = jnp.float32
BF16 = jnp.bfloat16

LANES = 128
SUBLANES = 8
D_MODEL = 1024
HEAD_DIM = 64
N_HEADS = 4
MIX_W = N_HEADS * HEAD_DIM
D_FF = 2816
FF_CHUNK = 256
N_FF_CHUNKS = D_FF // FF_CHUNK
N_MEM = 256
NUM_BUCKETS = 32
MAX_DISTANCE = 2048
DILATIONS = (1, 4, 16)
DIL_STEPS = 128
LRU_C = 8.0
EPS = 1e-6
NEG_BIG = -1e30

TQ = 128
TQC = 256
TK = 256
VMEM_LIMIT = 56 * 1024 * 1024

NT_DIMS = (((1,), (1,)), ((), ()))


def _cparams(*sem):
    return pltpu.CompilerParams(dimension_semantics=sem, vmem_limit_bytes=VMEM_LIMIT)


def _rms(x, g):
    return x * lax.rsqrt(jnp.mean(x * x, axis=-1, keepdims=True) + EPS) * g


def _softplus(x):
    return jnp.maximum(x, 0.0) + jnp.log(1.0 + jnp.exp(-jnp.abs(x)))


def _gelu(x):
    return 0.5 * x * (1.0 + lax.erf(x * math.sqrt(0.5)))


def _dot(a, b):
    return jnp.dot(a, b, preferred_element_type=F32)


def _dot_nt(a, b):
    return lax.dot_general(a, b, NT_DIMS, preferred_element_type=F32)


def _split2(x):
    hi = x.astype(BF16)
    lo = (x - hi.astype(F32)).astype(BF16)
    return hi, lo


def _head_of_lane(rows):
    return lax.broadcasted_iota(jnp.int32, (rows, MIX_W), 1) // HEAD_DIM


def _stack_heads(q):
    t = q.shape[0]
    head = _head_of_lane(t)
    qf = q.astype(F32)
    return jnp.concatenate([jnp.where(head == h, qf, 0.0) for h in range(N_HEADS)], axis=0).astype(BF16)


def _unstack_heads(o, t):
    head = _head_of_lane(t)
    out = jnp.where(head == 0, o[0:t], 0.0)
    for h in range(1, N_HEADS):
        out = jnp.where(head == h, o[h * t:(h + 1) * t], out)
    return out


def _unstack_cols(c, t):
    head = _head_of_lane(t)
    out = jnp.broadcast_to(c[0:t], (t, MIX_W))
    for h in range(1, N_HEADS):
        out = jnp.where(head == h, jnp.broadcast_to(c[h * t:(h + 1) * t], (t, MIX_W)), out)
    return out


def _inproj_kernel(x_ref, g_ref, w_ref, wf_ref, sb_ref, fox_ref, d1_ref, d4_ref, d16_ref,
                   lru_ref, f_ref, half_ref):
    hn = _rms(x_ref[0], g_ref[...]).astype(BF16)
    ts = hn.shape[0]
    for m, ref in enumerate((sb_ref, fox_ref)):
        for j in range(3):
            c = (3 * m + j) * MIX_W
            ref[j, 0] = _dot(hn, w_ref[:, c:c + MIX_W]).astype(BF16)
    for j in range(3):
        c = (6 + j) * MIX_W
        res = _dot(hn, w_ref[:, c:c + MIX_W])
        d1_ref[j, 0, 0] = res.astype(BF16)
        for h in range(2):
            half_ref[h] = res[:, h * LANES:(h + 1) * LANES]
        for ref, dil in ((d4_ref, DILATIONS[1]), (d16_ref, DILATIONS[2])):
            for r in range(dil):
                for h in range(2):
                    ref[j, 0, r, :, h * LANES:(h + 1) * LANES] = (
                        half_ref[h, pl.ds(r, ts // dil, stride=dil), :].astype(BF16))
    for j in range(2):
        c = (9 + j) * MIX_W
        lru_ref[j, 0] = _dot(hn, w_ref[:, c:c + MIX_W])
    f_ref[0] = _dot_nt(wf_ref[...], hn)


def _inproj(x, g, w_main, w_f, ts=512):
    b, s, d = x.shape
    qkv_shape = jax.ShapeDtypeStruct((3, b, s, MIX_W), BF16)
    qkv_spec = pl.BlockSpec((3, 1, ts, MIX_W), lambda bi, i: (0, bi, i, 0))
    cls_shapes = [jax.ShapeDtypeStruct((3, b, dil, s // dil, MIX_W), BF16) for dil in DILATIONS]
    cls_specs = [pl.BlockSpec((3, 1, dil, ts // dil, MIX_W), lambda bi, i: (0, bi, 0, i, 0))
                 for dil in DILATIONS]
    return pl.pallas_call(
        _inproj_kernel,
        grid=(b, s // ts),
        in_specs=[
            pl.BlockSpec((1, ts, d), lambda bi, i: (bi, i, 0)),
            pl.BlockSpec((1, d), lambda bi, i: (0, 0)),
            pl.BlockSpec(w_main.shape, lambda bi, i: (0, 0)),
            pl.BlockSpec(w_f.shape, lambda bi, i: (0, 0)),
        ],
        out_specs=[
            qkv_spec, qkv_spec, *cls_specs,
            pl.BlockSpec((2, 1, ts, MIX_W), lambda bi, i: (0, bi, i, 0)),
            pl.BlockSpec((1, 8, ts), lambda bi, i: (bi, 0, i)),
        ],
        out_shape=[
            qkv_shape, qkv_shape, *cls_shapes,
            jax.ShapeDtypeStruct((2, b, s, MIX_W), F32),
            jax.ShapeDtypeStruct((b, 8, s), F32),
        ],
        scratch_shapes=[pltpu.VMEM((2, ts, LANES), F32)],
        compiler_params=_cparams("parallel", "parallel"),
        name="inproj",
    )(x, g, w_main, w_f)


def _decay_kernel(f_ref, bf_ref, tri_ref, o_ref):
    s = f_ref.shape[2]
    tri = tri_ref[...]
    carry = jnp.zeros((8, 1), F32)
    for c in range(s // 128):
        sl = slice(c * 128, (c + 1) * 128)
        log_f = -_softplus(-(f_ref[0, :, sl] + bf_ref[...]))
        p1 = log_f.astype(BF16)
        r1 = log_f - p1.astype(F32)
        p2 = r1.astype(BF16)
        p3 = (r1 - p2.astype(F32)).astype(BF16)
        inc = _dot(p1, tri) + _dot(p2, tri) + _dot(p3, tri)
        o_ref[0, :, sl] = inc + carry
        carry = carry + jnp.sum(log_f, axis=-1, keepdims=True)


def _decay(f_t, b_f):
    b, _, s = f_t.shape
    tri = (lax.broadcasted_iota(jnp.int32, (128, 128), 0)
           <= lax.broadcasted_iota(jnp.int32, (128, 128), 1)).astype(BF16)
    return pl.pallas_call(
        _decay_kernel,
        grid=(b,),
        in_specs=[
            pl.BlockSpec((1, 8, s), lambda bi: (bi, 0, 0)),
            pl.BlockSpec((8, 1), lambda bi: (0, 0)),
            pl.BlockSpec((128, 128), lambda bi: (0, 0)),
        ],
        out_specs=pl.BlockSpec((1, 8, s), lambda bi: (bi, 0, 0)),
        out_shape=jax.ShapeDtypeStruct((b, 8, s), F32),
        compiler_params=_cparams("parallel"),
        name="fox_decay",
    )(f_t, b_f, tri)


def _block_positions(i, ks):
    row = lax.broadcasted_iota(jnp.int32, (N_HEADS * TQC, TK), 0) % TQC + i * TQC
    col = lax.broadcasted_iota(jnp.int32, (N_HEADS * TQC, TK), 1) + ks
    return row, col


SB_SATURATED = 110.0


def _sb_kernel(q_ref, k_ref, v_ref, uo_ref, o_ref, acc_ref, car_ref):
    i = pl.program_id(1)
    qs = _stack_heads(q_ref[0, 0])
    k_diag = (i * TQC) // TK

    half_rows = N_HEADS * TQC // 2
    halves = [slice(0, half_rows), slice(half_rows, 2 * half_rows)]

    def block(kb, diagonal):
        ks = pl.multiple_of(kb * TK, TK)
        k_blk = k_ref[0, 0, pl.ds(ks, TK), :]
        v_blk = v_ref[0, 0, pl.ds(ks, TK), :]
        if diagonal:
            row, col = _block_positions(i, ks)
            strict = (col < row)[:half_rows]
        zs = [_dot_nt(qs[r], k_blk) for r in halves]
        drops, sums = [], []
        for z in zs:
            drop = _softplus(z)
            cum_in = jnp.where(strict, drop, 0.0) if diagonal else drop
            sums.append(_dot(cum_in.astype(BF16), uo_ref[...]))
            drops.append(drop)
        for r, z, drop, sm in zip(halves, zs, drops, sums):
            if diagonal:
                att = jnp.where(strict, jnp.exp(z - drop - sm[:, :TK]), 0.0)
                car_ref[r] = sm[:, TK:]
                acc_ref[r] = _dot(att.astype(BF16), v_blk)
            else:
                car = car_ref[r]
                att = jnp.exp(z - drop - sm[:, :TK] - car)
                car_ref[r] = car + sm[:, TK:]
                acc_ref[r] += _dot(att.astype(BF16), v_blk)
        return jnp.min(car_ref[:, :LANES]) < SB_SATURATED

    def cond(state):
        kb, live = state
        return jnp.logical_and(kb >= 0, live)

    def body(state):
        kb, _ = state
        return kb - 1, block(kb, False)

    lax.while_loop(cond, body, (k_diag - 1, block(k_diag, True)))
    o_ref[0] = _unstack_heads(acc_ref[...], TQC).astype(o_ref.dtype)


def _sb_attention(qkv):
    _, b, s, _ = qkv.shape
    later = (lax.broadcasted_iota(jnp.int32, (TK, TK), 0)
             > lax.broadcasted_iota(jnp.int32, (TK, TK), 1))
    u = jnp.concatenate([later, jnp.ones((TK, TK), bool)], axis=1).astype(BF16)
    return pl.pallas_call(
        _sb_kernel,
        grid=(b, s // TQC),
        in_specs=[
            pl.BlockSpec((1, 1, TQC, MIX_W), lambda bi, i: (0, bi, i, 0)),
            pl.BlockSpec((1, 1, s, MIX_W), lambda bi, i: (1, bi, 0, 0)),
            pl.BlockSpec((1, 1, s, MIX_W), lambda bi, i: (2, bi, 0, 0)),
            pl.BlockSpec((TK, 2 * TK), lambda bi, i: (0, 0)),
        ],
        out_specs=pl.BlockSpec((1, TQC, MIX_W), lambda bi, i: (bi, i, 0)),
        out_shape=jax.ShapeDtypeStruct((b, s, MIX_W), BF16),
        scratch_shapes=[
            pltpu.VMEM((N_HEADS * TQC, MIX_W), F32),
            pltpu.VMEM((N_HEADS * TQC, TK), F32),
        ],
        compiler_params=_cparams("parallel", "arbitrary"),
        name="sb_attn",
    )(qkv, qkv, qkv, u)


def _fox_kernel(q_ref, k_ref, v_ref, cum_ref, o_ref, z_ref, mx_ref, l_ref, acc_ref):
    i = pl.program_id(1)
    qs = _stack_heads(q_ref[0, 0])
    k_diag = (i * TQC) // TK

    def logits(kb):
        ks = pl.multiple_of(kb * TK, TK)
        cum = cum_ref[0, :, pl.ds(ks, TK)]
        decay = jnp.concatenate(
            [jnp.broadcast_to(cum[h:h + 1], (TQC, TK)) for h in range(N_HEADS)], axis=0)
        return _dot_nt(qs, k_ref[0, 0, pl.ds(ks, TK), :]) - decay, ks

    z, ks = logits(k_diag)
    row, col = _block_positions(i, ks)
    z = jnp.where(col <= row, z, NEG_BIG)
    z_ref[k_diag] = z
    mx_ref[...] = z

    def pass1(kb, carry):
        z, _ = logits(kb)
        z_ref[kb] = z
        mx_ref[...] = jnp.maximum(mx_ref[...], z)
        return carry

    lax.fori_loop(0, k_diag, pass1, 0)
    m = jnp.max(mx_ref[...], axis=-1, keepdims=True)
    mx_ref[...] = jnp.broadcast_to(m, mx_ref.shape)

    l_ref[...] = jnp.zeros_like(l_ref)
    acc_ref[...] = jnp.zeros_like(acc_ref)

    def pass2(kb, carry):
        ks = pl.multiple_of(kb * TK, TK)
        p = jnp.exp(z_ref[kb] - mx_ref[...])
        l_ref[...] += p
        acc_ref[...] += _dot(p.astype(BF16), v_ref[0, 0, pl.ds(ks, TK), :])
        return carry

    lax.fori_loop(0, k_diag + 1, pass2, 0)
    l = jnp.sum(l_ref[...], axis=-1, keepdims=True)
    o_ref[0] = _unstack_heads(acc_ref[...] / l, TQC).astype(o_ref.dtype)


def _fox_attention(qkv, cum):
    _, b, s, _ = qkv.shape
    blk = (N_HEADS * TQC, TK)
    return pl.pallas_call(
        _fox_kernel,
        grid=(b, s // TQC),
        in_specs=[
            pl.BlockSpec((1, 1, TQC, MIX_W), lambda bi, i: (0, bi, i, 0)),
            pl.BlockSpec((1, 1, s, MIX_W), lambda bi, i: (1, bi, 0, 0)),
            pl.BlockSpec((1, 1, s, MIX_W), lambda bi, i: (2, bi, 0, 0)),
            pl.BlockSpec((1, 8, s), lambda bi, i: (bi, 0, 0)),
        ],
        out_specs=pl.BlockSpec((1, TQC, MIX_W), lambda bi, i: (bi, i, 0)),
        out_shape=jax.ShapeDtypeStruct((b, s, MIX_W), BF16),
        scratch_shapes=[
            pltpu.VMEM((s // TK,) + blk, F32),
            pltpu.VMEM(blk, F32),
            pltpu.VMEM(blk, F32),
            pltpu.VMEM((N_HEADS * TQC, MIX_W), F32),
        ],
        compiler_params=_cparams("parallel", "arbitrary"),
        name="fox_attn",
    )(qkv, qkv, qkv, cum)


def _t5_bucket(dist):
    n = jnp.maximum(dist, 0)
    max_exact = NUM_BUCKETS // 2
    nf = jnp.maximum(n, 1).astype(F32)
    large = max_exact + (jnp.log(nf / max_exact) / math.log(MAX_DISTANCE / max_exact)
                         * (NUM_BUCKETS - max_exact)).astype(jnp.int32)
    large = jnp.minimum(large, NUM_BUCKETS - 1)
    return jnp.where(n < max_exact, n, large)


def _bias_kernel(idx_ref, rb_ref, o_ref):
    for d in range(len(DILATIONS)):
        for variant in range(2):
            idx = idx_ref[d, variant]
            for h in range(N_HEADS):
                acc = jnp.full(idx.shape, NEG_BIG, F32)
                for bucket in range(NUM_BUCKETS):
                    acc = jnp.where(idx == bucket, rb_ref[bucket, h], acc)
                o_ref[d, variant, h] = acc


def _bias_tables(rel_bias):
    t = jnp.arange(TQ)[:, None]
    c = jnp.arange(2 * TQ)[None, :]
    delta = TQ + t - c
    valid = (delta >= 0) & (delta <= DIL_STEPS)
    idx = jnp.stack([jnp.where(valid, _t5_bucket(delta * dil), -1) for dil in DILATIONS])
    idx_first = jnp.concatenate([idx[:, :, TQ:], jnp.full_like(idx[:, :, TQ:], -1)], axis=-1)
    idx = jnp.stack([idx, idx_first], axis=1)
    out = pl.pallas_call(
        _bias_kernel,
        in_specs=[
            pl.BlockSpec(memory_space=pltpu.VMEM),
            pl.BlockSpec(memory_space=pltpu.SMEM),
        ],
        out_specs=pl.BlockSpec(memory_space=pltpu.VMEM),
        out_shape=jax.ShapeDtypeStruct((len(DILATIONS), 2, N_HEADS, TQ, 2 * TQ), F32),
        name="dil_bias",
    )(idx.astype(jnp.int32), rel_bias)
    return out.reshape(len(DILATIONS), 2, N_HEADS * TQ, 2 * TQ)


DIL_GROUP = 4


def _dil_kernel(q_ref, k_ref, v_ref, t_ref, o_ref, lse_ref, *, by_class, n_groups):
    g = pl.program_id(2)

    def attend(jj, scores, v_win):
        m = jnp.max(scores, axis=-1, keepdims=True)
        p = jnp.exp(scores - m)
        l = jnp.sum(p, axis=-1, keepdims=True)
        o = _unstack_heads(_dot(p.astype(BF16), v_win) / l, TQ).astype(o_ref.dtype)
        lse = _unstack_cols(m + jnp.log(l), TQ)
        if by_class:
            o_ref[0, jj] = o
            lse_ref[0, jj] = lse
        else:
            o_ref[0, 0, jj * TQ:(jj + 1) * TQ] = o
            lse_ref[0, 0, jj * TQ:(jj + 1) * TQ] = lse

    pending = []
    for jj in range(DIL_GROUP):
        if by_class:
            scores = (_dot_nt(_stack_heads(q_ref[0, 0, jj]), k_ref[0, 0, jj])
                      + t_ref[0, 0, :, TQ:2 * TQ])
            pending.append((scores, v_ref[0, 0, jj]))
            continue
        qs = _stack_heads(q_ref[0, 0, 0, jj * TQ:(jj + 1) * TQ])
        if n_groups == 1 or jj > 0:
            n = g * DIL_GROUP + jj if n_groups > 1 else jj
            start = (n - 1) * TQ if jj > 0 else 0
            bias = t_ref[0, 0 if jj > 0 else 1]
        else:
            start = jnp.maximum(g * DIL_GROUP - 1, 0) * TQ
            bias = t_ref[0, (g == 0).astype(jnp.int32)]
        if not isinstance(start, int):
            start = pl.multiple_of(start, TQ)
        window = pl.ds(start, 2 * TQ)
        pending.append((_dot_nt(qs, k_ref[0, 0, 0, window, :]) + bias, v_ref[0, 0, 0, window, :]))
    for jj, (scores, v_win) in enumerate(pending):
        attend(jj, scores, v_win)


def _dil_branch(qkv, tables, branch):
    _, b, dil, cls_len, _ = qkv.shape
    n_blocks = cls_len // TQ
    by_class = n_blocks == 1
    if by_class:
        grid = (b, dil // DIL_GROUP, 1)
        qkv_specs = [pl.BlockSpec((1, 1, DIL_GROUP, TQ, MIX_W), functools.partial(
            lambda bi, c, g, j: (j, bi, c, 0, 0), j=j)) for j in range(3)]
        out_spec = pl.BlockSpec((1, DIL_GROUP, TQ, MIX_W), lambda bi, c, g: (bi, c, 0, 0))
    else:
        grid = (b, dil, n_blocks // DIL_GROUP)
        qkv_specs = [pl.BlockSpec((1, 1, 1, DIL_GROUP * TQ, MIX_W), lambda bi, r, g: (0, bi, r, g, 0))]
        qkv_specs += [pl.BlockSpec((1, 1, 1, cls_len, MIX_W), functools.partial(
            lambda bi, r, g, j: (j, bi, r, 0, 0), j=j)) for j in (1, 2)]
        out_spec = pl.BlockSpec((1, 1, DIL_GROUP * TQ, MIX_W), lambda bi, r, g: (bi, r, g, 0))
    o, lse = pl.pallas_call(
        functools.partial(_dil_kernel, by_class=by_class, n_groups=grid[2]),
        grid=grid,
        in_specs=qkv_specs + [
            pl.BlockSpec((1, 2, N_HEADS * TQ, 2 * TQ), lambda bi, r, g: (branch, 0, 0, 0))],
        out_specs=[out_spec, out_spec],
        out_shape=[jax.ShapeDtypeStruct((b, dil, cls_len, MIX_W), BF16),
                   jax.ShapeDtypeStruct((b, dil, cls_len, MIX_W), F32)],
        compiler_params=_cparams("parallel", "parallel", "arbitrary"),
        name=f"dil_attn_{dil}",
    )(qkv, qkv, qkv, tables)
    to_seq = lambda a: a.transpose(0, 2, 1, 3).reshape(b, dil * cls_len, MIX_W)
    return to_seq(o), to_seq(lse)


def _dil_combine_kernel(o1, l1, o2, l2, o3, l3, out_ref):
    a1, a2, a3 = l1[0], l2[0], l3[0]
    m = jnp.maximum(jnp.maximum(a1, a2), a3)
    e1, e2, e3 = jnp.exp(a1 - m), jnp.exp(a2 - m), jnp.exp(a3 - m)
    tot = e1 + e2 + e3
    out = (e1 / tot) * o1[0] + (e2 / tot) * o2[0] + (e3 / tot) * o3[0]
    out_ref[0] = out.astype(out_ref.dtype)


def _dil_combine(branches, ts=512):
    b, s, w = branches[0][0].shape
    spec = pl.BlockSpec((1, ts, w), lambda bi, i: (bi, i, 0))
    flat = [a for pair in branches for a in pair]
    return pl.pallas_call(
        _dil_combine_kernel,
        grid=(b, s // ts),
        in_specs=[spec] * 6,
        out_specs=spec,
        out_shape=jax.ShapeDtypeStruct((b, s, w), BF16),
        compiler_params=_cparams("parallel", "parallel"),
        name="dil_combine",
    )(*flat)


def _shift_rows(x, k, fill):
    row = lax.broadcasted_iota(jnp.int32, x.shape, 0)
    return jnp.where(row >= k, pltpu.roll(x, k, 0), fill)


def _lru_kernel(xg_ref, cw_ref, cb_ref, wa_ref, ba_ref, wx_ref, bx_ref, lam_ref, o_ref, h_ref):
    x = xg_ref[0, 0]
    gate = xg_ref[1, 0]
    s = x.shape[0]
    kw = cw_ref.shape[0]
    xc = cb_ref[...] + cw_ref[kw - 1:kw, :] * x
    for j in range(kw - 1):
        xc = xc + cw_ref[j:j + 1, :] * _shift_rows(x, kw - 1 - j, 0.0)
    xcb = xc.astype(BF16)
    r = jax.nn.sigmoid(_dot(xcb, wa_ref[...]) + ba_ref[...])
    i_gate = jax.nn.sigmoid(_dot(xcb, wx_ref[...]) + bx_ref[...])
    a = jnp.exp(-LRU_C * r * _softplus(-lam_ref[...]))
    h = jnp.sqrt(1.0 - a * a) * (i_gate * xc)
    in_group = lax.broadcasted_iota(jnp.int32, x.shape, 0) % SUBLANES
    k = 1
    while k < SUBLANES:
        prev = in_group >= k
        h = a * jnp.where(prev, pltpu.roll(h, k, 0), 0.0) + h
        a = a * jnp.where(prev, pltpu.roll(a, k, 0), 1.0)
        k *= 2
    carry = jnp.zeros((1, x.shape[1]), F32)
    for c in range(s // SUBLANES):
        rows = slice(c * SUBLANES, (c + 1) * SUBLANES)
        hc = h[rows] + a[rows] * carry
        h_ref[rows] = hc
        carry = hc[SUBLANES - 1:SUBLANES]
    o_ref[0] = (h_ref[...] * _gelu(gate)).astype(o_ref.dtype)


def _lru(xg, conv_w, conv_b, wa_bd, b_a, wx_bd, b_x, lam):
    _, b, s, w = xg.shape
    full = lambda a: pl.BlockSpec(a.shape, lambda bi: (0,) * a.ndim)
    args = (conv_w, conv_b, wa_bd, b_a, wx_bd, b_x, lam)
    return pl.pallas_call(
        _lru_kernel,
        grid=(b,),
        in_specs=[pl.BlockSpec((2, 1, s, w), lambda bi: (0, bi, 0, 0))] + [full(a) for a in args],
        out_specs=pl.BlockSpec((1, s, w), lambda bi: (bi, 0, 0)),
        out_shape=jax.ShapeDtypeStruct((b, s, w), BF16),
        scratch_shapes=[pltpu.VMEM((s, w), F32)],
        compiler_params=_cparams("parallel"),
        name="rg_lru",
    )(xg, *args)


def _memkv_kernel(mem_ref, g_ref, wk_ref, wv_ref, k_ref, v_ref):
    mn = _rms(mem_ref[0], g_ref[...]).astype(BF16)
    k_ref[0] = _dot(mn, wk_ref[...]).astype(BF16)
    v_ref[0] = _dot(mn, wv_ref[...]).astype(BF16)


def _memkv(mem, g, w_ck, w_cv):
    b, m, d = mem.shape
    out = jax.ShapeDtypeStruct((b, m, MIX_W), BF16)
    return pl.pallas_call(
        _memkv_kernel,
        grid=(b,),
        in_specs=[
            pl.BlockSpec((1, m, d), lambda bi: (bi, 0, 0)),
            pl.BlockSpec((1, d), lambda bi: (0, 0)),
            pl.BlockSpec((d, MIX_W), lambda bi: (0, 0)),
            pl.BlockSpec((d, MIX_W), lambda bi: (0, 0)),
        ],
        out_specs=[pl.BlockSpec((1, m, MIX_W), lambda bi: (bi, 0, 0))] * 2,
        out_shape=[out, out],
        compiler_params=_cparams("parallel"),
        name="mem_kv",
    )(mem, g, w_ck, w_cv)


def _mix_cross_kernel(x_ref, sb_ref, fox_ref, dil_ref, lru_ref, wo_ref, g_ref, wq_ref,
                      km_ref, vm_ref, wco_ref, o_ref):
    t = MIX_SUB
    subs = [slice(u * t, (u + 1) * t) for u in range(x_ref.shape[1] // t)]
    xs = []
    for r in subs:
        x = x_ref[0, r]
        for j, ref in enumerate((sb_ref, fox_ref, dil_ref, lru_ref)):
            x = x + _dot(ref[0, r], wo_ref[j])
        xs.append(x)
    qs = [_stack_heads(_dot(_rms(x, g_ref[...]).astype(BF16), wq_ref[...])) for x in xs]
    scores = [_dot_nt(q, km_ref[0]) for q in qs]
    outs = []
    for sc in scores:
        p = jnp.exp(sc - jnp.max(sc, axis=-1, keepdims=True))
        p = p / jnp.sum(p, axis=-1, keepdims=True)
        outs.append(_unstack_heads(_dot(p.astype(BF16), vm_ref[0]), t))
    for r, x, o in zip(subs, xs, outs):
        o_ref[0, r] = x + _dot(o.astype(BF16), wco_ref[...])


MIX_SUB = 256


def _mix_cross(x, mixers, w_out, g, w_cq, k_mem, v_mem, w_co, ts=512):
    b, s, d = x.shape
    m = k_mem.shape[1]
    mix_spec = pl.BlockSpec((1, ts, MIX_W), lambda bi, i: (bi, i, 0))
    const = lambda a: pl.BlockSpec(a.shape, lambda bi, i: (0,) * a.ndim)
    mem_spec = pl.BlockSpec((1, m, MIX_W), lambda bi, i: (bi, 0, 0))
    return pl.pallas_call(
        _mix_cross_kernel,
        grid=(b, s // ts),
        in_specs=[pl.BlockSpec((1, ts, d), lambda bi, i: (bi, i, 0))] + [mix_spec] * 4
                 + [const(w_out), const(g), const(w_cq), mem_spec, mem_spec, const(w_co)],
        out_specs=pl.BlockSpec((1, ts, d), lambda bi, i: (bi, i, 0)),
        out_shape=jax.ShapeDtypeStruct((b, s, d), F32),
        compiler_params=_cparams("parallel", "parallel"),
        name="mix_cross",
    )(x, *mixers, w_out, g, w_cq, k_mem, v_mem, w_co)


HALO = 8


def _ffn_kernel(x_ref, halo_ref, g_ref, wup_ref, cw_ref, cb_ref, wdn_ref, gf_ref, o_ref,
                ext_ref, act_ref, acc_ref, *, final_norm):
    i = pl.program_id(1)
    x = x_ref[0]
    t = x.shape[0]
    g = g_ref[...]
    halo = jnp.where(i > 0, _rms(halo_ref[0], g), 0.0)
    hn = jnp.concatenate([halo, _rms(x, g)], axis=0).astype(BF16)
    acc_ref[...] = jnp.zeros_like(acc_ref)

    def project(c):
        for half in range(2):
            col = (half * N_FF_CHUNKS + c) * FF_CHUNK
            ext_ref[2 * (c % 2) + half] = _dot(hn, wup_ref[:, col:col + FF_CHUNK])

    def conv(c, slot):
        ext = ext_ref.at[slot]
        w = cw_ref[c]
        return (cb_ref[c] + w[2:3] * ext[pl.ds(HALO, t), :]
                + w[1:2] * ext[pl.ds(HALO - 1, t), :]
                + w[0:1] * ext[pl.ds(HALO - 2, t), :])

    def down(c):
        acc_ref[...] += _dot(act_ref[c % 2], wdn_ref[c])

    project(0)
    for c in range(N_FF_CHUNKS):
        if c + 1 < N_FF_CHUNKS:
            project(c + 1)
        if c >= 1:
            down(c - 1)
        up = conv(c, 2 * (c % 2))
        gate = conv(N_FF_CHUNKS + c, 2 * (c % 2) + 1)
        act_ref[c % 2] = (_gelu(gate) * up).astype(BF16)
    down(N_FF_CHUNKS - 1)
    out = x + acc_ref[...]
    if final_norm:
        out = _rms(out, gf_ref[...])
    o_ref[0] = out


def _ffn(x, g, w_up, conv_w, conv_b, w_down, g_final, final_norm, ts=256):
    b, s, d = x.shape
    per_tile = ts // HALO
    const = lambda a: pl.BlockSpec(a.shape, lambda bi, i: (0,) * a.ndim)
    return pl.pallas_call(
        functools.partial(_ffn_kernel, final_norm=final_norm),
        grid=(b, s // ts),
        in_specs=[
            pl.BlockSpec((1, ts, d), lambda bi, i: (bi, i, 0)),
            pl.BlockSpec((1, HALO, d), lambda bi, i: (bi, jnp.maximum(i * per_tile - 1, 0), 0)),
            const(g), const(w_up), const(conv_w), const(conv_b), const(w_down), const(g_final),
        ],
        out_specs=pl.BlockSpec((1, ts, d), lambda bi, i: (bi, i, 0)),
        out_shape=jax.ShapeDtypeStruct((b, s, d), F32),
        scratch_shapes=[
            pltpu.VMEM((4, HALO + ts, FF_CHUNK), F32),
            pltpu.VMEM((2, ts, FF_CHUNK), BF16),
            pltpu.VMEM((ts, d), F32),
        ],
        compiler_params=_cparams("parallel", "arbitrary"),
        name="conv_geglu",
    )(x, x, g, w_up, conv_w, conv_b, w_down, g_final)


def _block_diag(w):
    g, n, _ = w.shape
    eye = jnp.eye(g, dtype=w.dtype)
    return (eye[:, None, :, None] * w[:, :, None, :]).reshape(g * n, g * n)


def kernel(x, mem, norm_mix_g, w_in, b_forget, lru_conv_w, lru_conv_b, lru_w_a, lru_b_a, lru_w_x, lru_b_x, lru_lambda, w_out, norm_cross_g, norm_mem_g, w_cq, w_ck, w_cv, w_co, norm_ffn_g, w_up, ffn_conv_w, ffn_conv_b, w_down, rel_bias, final_norm_g):
    depth = w_in.shape[0]
    qkv_w = 3 * MIX_W
    q_scale = HEAD_DIM ** -0.5
    col_scale = jnp.ones((3, 3, MIX_W), F32).at[:, 0, :].set(q_scale).reshape(-1)
    col_scale = jnp.concatenate([col_scale, jnp.ones((2 * MIX_W,), F32)])
    tables = _bias_tables(rel_bias)
    row = lambda v: v.reshape(1, -1)

    for l in range(depth):
        w = w_in[l]
        n_f = N_HEADS
        w_main = jnp.concatenate([w[:, :2 * qkv_w], w[:, 2 * qkv_w + n_f:]], axis=1)
        w_main = (w_main * col_scale).astype(BF16)
        w_f = jnp.zeros((8, D_MODEL), F32).at[:n_f].set(w[:, 2 * qkv_w:2 * qkv_w + n_f].T).astype(BF16)
        b_f = jnp.zeros((8, 1), F32).at[:n_f, 0].set(b_forget[l])

        sb_qkv, fox_qkv, *dil_qkvs, lru_xg, f_t = _inproj(x, row(norm_mix_g[l]), w_main, w_f)
        o_sb = _sb_attention(sb_qkv)
        o_fox = _fox_attention(fox_qkv, _decay(f_t, b_f))
        o_dil = _dil_combine([_dil_branch(qkv, tables, br) for br, qkv in enumerate(dil_qkvs)])
        o_lru = _lru(lru_xg, lru_conv_w[l], row(lru_conv_b[l]),
                     _block_diag(lru_w_a[l]).astype(BF16), row(lru_b_a[l]),
                     _block_diag(lru_w_x[l]).astype(BF16), row(lru_b_x[l]), row(lru_lambda[l]))

        k_mem, v_mem = _memkv(mem, row(norm_mem_g[l]), w_ck[l].astype(BF16), w_cv[l].astype(BF16))
        x = _mix_cross(x, (o_sb, o_fox, o_dil, o_lru),
                       w_out[l].reshape(4, MIX_W, D_MODEL).astype(BF16), row(norm_cross_g[l]),
                       (w_cq[l] * q_scale).astype(BF16), k_mem, v_mem, w_co[l].astype(BF16))

        wu = w_up[l].astype(BF16)
        cw = ffn_conv_w[l].reshape(-1, 2 * N_FF_CHUNKS, FF_CHUNK).transpose(1, 0, 2)
        cb = ffn_conv_b[l].reshape(2 * N_FF_CHUNKS, 1, FF_CHUNK)
        wd = w_down[l].reshape(N_FF_CHUNKS, FF_CHUNK, D_MODEL).astype(BF16)
        x = _ffn(x, row(norm_ffn_g[l]), wu, cw, cb, wd, row(final_norm_g), l == depth - 1)
    return x
```

```python
import functools
import math

import jax
import jax.numpy as jnp
from jax import lax
from jax.experimental import pallas as pl
from jax.experimental.pallas import tpu as pltpu

F32 = jnp.float32
BF16 = jnp.bfloat16

LANES = 128
SUBLANES = 8
D_MODEL = 1024
HEAD_DIM = 64
N_HEADS = 4
MIX_W = N_HEADS * HEAD_DIM
D_FF = 2816
FF_CHUNK = 256
N_FF_CHUNKS = D_FF // FF_CHUNK
N_MEM = 256
NUM_BUCKETS = 32
MAX_DISTANCE = 2048
DILATIONS = (1, 4, 16)
DIL_STEPS = 128
LRU_C = 8.0
EPS = 1e-6
NEG_BIG = -1e30

TQ = 128
TQC = 256
TK = 256
VMEM_LIMIT = 56 * 1024 * 1024

NT_DIMS = (((1,), (1,)), ((), ()))


def _cparams(*sem):
    return pltpu.CompilerParams(dimension_semantics=sem, vmem_limit_bytes=VMEM_LIMIT)


def _rms(x, g):
    return x * lax.rsqrt(jnp.mean(x * x, axis=-1, keepdims=True) + EPS) * g


def _softplus(x):
    return jnp.maximum(x, 0.0) + jnp.log(1.0 + jnp.exp(-jnp.abs(x)))


def _gelu(x):
    return 0.5 * x * (1.0 + lax.erf(x * math.sqrt(0.5)))


def _dot(a, b):
    return jnp.dot(a, b, preferred_element_type=F32)


def _dot_nt(a, b):
    return lax.dot_general(a, b, NT_DIMS, preferred_element_type=F32)


def _split2(x):
    hi = x.astype(BF16)
    lo = (x - hi.astype(F32)).astype(BF16)
    return hi, lo


def _head_of_lane(rows):
    return lax.broadcasted_iota(jnp.int32, (rows, MIX_W), 1) // HEAD_DIM


def _stack_heads(q):
    t = q.shape[0]
    head = _head_of_lane(t)
    qf = q.astype(F32)
    return jnp.concatenate([jnp.where(head == h, qf, 0.0) for h in range(N_HEADS)], axis=0).astype(BF16)


def _unstack_heads(o, t):
    head = _head_of_lane(t)
    out = jnp.where(head == 0, o[0:t], 0.0)
    for h in range(1, N_HEADS):
        out = jnp.where(head == h, o[h * t:(h + 1) * t], out)
    return out


def _unstack_cols(c, t):
    head = _head_of_lane(t)
    out = jnp.broadcast_to(c[0:t], (t, MIX_W))
    for h in range(1, N_HEADS):
        out = jnp.where(head == h, jnp.broadcast_to(c[h * t:(h + 1) * t], (t, MIX_W)), out)
    return out


def _inproj_kernel(x_ref, g_ref, w_ref, wf_ref, sb_ref, fox_ref, d1_ref, d4_ref, d16_ref,
                   lru_ref, f_ref, half_ref):
    hn = _rms(x_ref[0], g_ref[...]).astype(BF16)
    ts = hn.shape[0]
    for m, ref in enumerate((sb_ref, fox_ref)):
        for j in range(3):
            c = (3 * m + j) * MIX_W
            ref[j, 0] = _dot(hn, w_ref[:, c:c + MIX_W]).astype(BF16)
    for j in range(3):
        c = (6 + j) * MIX_W
        res = _dot(hn, w_ref[:, c:c + MIX_W])
        d1_ref[j, 0, 0] = res.astype(BF16)
        for h in range(2):
            half_ref[h] = res[:, h * LANES:(h + 1) * LANES]
        for ref, dil in ((d4_ref, DILATIONS[1]), (d16_ref, DILATIONS[2])):
            for r in range(dil):
                for h in range(2):
                    ref[j, 0, r, :, h * LANES:(h + 1) * LANES] = (
                        half_ref[h, pl.ds(r, ts // dil, stride=dil), :].astype(BF16))
    for j in range(2):
        c = (9 + j) * MIX_W
        lru_ref[j, 0] = _dot(hn, w_ref[:, c:c + MIX_W])
    f_ref[0] = _dot_nt(wf_ref[...], hn)


def _inproj(x, g, w_main, w_f, ts=512):
    b, s, d = x.shape
    qkv_shape = jax.ShapeDtypeStruct((3, b, s, MIX_W), BF16)
    qkv_spec = pl.BlockSpec((3, 1, ts, MIX_W), lambda bi, i: (0, bi, i, 0))
    cls_shapes = [jax.ShapeDtypeStruct((3, b, dil, s // dil, MIX_W), BF16) for dil in DILATIONS]
    cls_specs = [pl.BlockSpec((3, 1, dil, ts // dil, MIX_W), lambda bi, i: (0, bi, 0, i, 0))
                 for dil in DILATIONS]
    return pl.pallas_call(
        _inproj_kernel,
        grid=(b, s // ts),
        in_specs=[
            pl.BlockSpec((1, ts, d), lambda bi, i: (bi, i, 0)),
            pl.BlockSpec((1, d), lambda bi, i: (0, 0)),
            pl.BlockSpec(w_main.shape, lambda bi, i: (0, 0)),
            pl.BlockSpec(w_f.shape, lambda bi, i: (0, 0)),
        ],
        out_specs=[
            qkv_spec, qkv_spec, *cls_specs,
            pl.BlockSpec((2, 1, ts, MIX_W), lambda bi, i: (0, bi, i, 0)),
            pl.BlockSpec((1, 8, ts), lambda bi, i: (bi, 0, i)),
        ],
        out_shape=[
            qkv_shape, qkv_shape, *cls_shapes,
            jax.ShapeDtypeStruct((2, b, s, MIX_W), F32),
            jax.ShapeDtypeStruct((b, 8, s), F32),
        ],
        scratch_shapes=[pltpu.VMEM((2, ts, LANES), F32)],
        compiler_params=_cparams("parallel", "parallel"),
        name="inproj",
    )(x, g, w_main, w_f)


def _decay_kernel(f_ref, bf_ref, tri_ref, o_ref):
    s = f_ref.shape[2]
    tri = tri_ref[...]
    carry = jnp.zeros((8, 1), F32)
    for c in range(s // 128):
        sl = slice(c * 128, (c + 1) * 128)
        log_f = -_softplus(-(f_ref[0, :, sl] + bf_ref[...]))
        p1 = log_f.astype(BF16)
        r1 = log_f - p1.astype(F32)
        p2 = r1.astype(BF16)
        p3 = (r1 - p2.astype(F32)).astype(BF16)
        inc = _dot(p1, tri) + _dot(p2, tri) + _dot(p3, tri)
        o_ref[0, :, sl] = inc + carry
        carry = carry + jnp.sum(log_f, axis=-1, keepdims=True)


def _decay(f_t, b_f):
    b, _, s = f_t.shape
    tri = (lax.broadcasted_iota(jnp.int32, (128, 128), 0)
           <= lax.broadcasted_iota(jnp.int32, (128, 128), 1)).astype(BF16)
    return pl.pallas_call(
        _decay_kernel,
        grid=(b,),
        in_specs=[
            pl.BlockSpec((1, 8, s), lambda bi: (bi, 0, 0)),
            pl.BlockSpec((8, 1), lambda bi: (0, 0)),
            pl.BlockSpec((128, 128), lambda bi: (0, 0)),
        ],
        out_specs=pl.BlockSpec((1, 8, s), lambda bi: (bi, 0, 0)),
        out_shape=jax.ShapeDtypeStruct((b, 8, s), F32),
        compiler_params=_cparams("parallel"),
        name="fox_decay",
    )(f_t, b_f, tri)


def _block_positions(i, ks):
    row = lax.broadcasted_iota(jnp.int32, (N_HEADS * TQC, TK), 0) % TQC + i * TQC
    col = lax.broadcasted_iota(jnp.int32, (N_HEADS * TQC, TK), 1) + ks
    return row, col


SB_SATURATED = 110.0


def _sb_kernel(q_ref, k_ref, v_ref, uo_ref, o_ref, acc_ref, car_ref):
    i = pl.program_id(1)
    qs = _stack_heads(q_ref[0, 0])
    k_diag = (i * TQC) // TK

    half_rows = N_HEADS * TQC // 2
    halves = [slice(0, half_rows), slice(half_rows, 2 * half_rows)]

    def block(kb, diagonal):
        ks = pl.multiple_of(kb * TK, TK)
        k_blk = k_ref[0, 0, pl.ds(ks, TK), :]
        v_blk = v_ref[0, 0, pl.ds(ks, TK), :]
        if diagonal:
            row, col = _block_positions(i, ks)
            strict = (col < row)[:half_rows]
        zs = [_dot_nt(qs[r], k_blk) for r in halves]
        drops, sums = [], []
        for z in zs:
            drop = _softplus(z)
            cum_in = jnp.where(strict, drop, 0.0) if diagonal else drop
            sums.append(_dot(cum_in.astype(BF16), uo_ref[...]))
            drops.append(drop)
        for r, z, drop, sm in zip(halves, zs, drops, sums):
            if diagonal:
                att = jnp.where(strict, jnp.exp(z - drop - sm[:, :TK]), 0.0)
                car_ref[r] = sm[:, TK:]
                acc_ref[r] = _dot(att.astype(BF16), v_blk)
            else:
                car = car_ref[r]
                att = jnp.exp(z - drop - sm[:, :TK] - car)
                car_ref[r] = car + sm[:, TK:]
                acc_ref[r] += _dot(att.astype(BF16), v_blk)
        return jnp.min(car_ref[:, :LANES]) < SB_SATURATED

    def cond(state):
        kb, live = state
        return jnp.logical_and(kb >= 0, live)

    def body(state):
        kb, _ = state
        return kb - 1, block(kb, False)

    lax.while_loop(cond, body, (k_diag - 1, block(k_diag, True)))
    o_ref[0] = _unstack_heads(acc_ref[...], TQC).astype(o_ref.dtype)


def _sb_attention(qkv):
    _, b, s, _ = qkv.shape
    later = (lax.broadcasted_iota(jnp.int32, (TK, TK), 0)
             > lax.broadcasted_iota(jnp.int32, (TK, TK), 1))
    u = jnp.concatenate([later, jnp.ones((TK, TK), bool)], axis=1).astype(BF16)
    return pl.pallas_call(
        _sb_kernel,
        grid=(b, s // TQC),
        in_specs=[
            pl.BlockSpec((1, 1, TQC, MIX_W), lambda bi, i: (0, bi, i, 0)),
            pl.BlockSpec((1, 1, s, MIX_W), lambda bi, i: (1, bi, 0, 0)),
            pl.BlockSpec((1, 1, s, MIX_W), lambda bi, i: (2, bi, 0, 0)),
            pl.BlockSpec((TK, 2 * TK), lambda bi, i: (0, 0)),
        ],
        out_specs=pl.BlockSpec((1, TQC, MIX_W), lambda bi, i: (bi, i, 0)),
        out_shape=jax.ShapeDtypeStruct((b, s, MIX_W), BF16),
        scratch_shapes=[
            pltpu.VMEM((N_HEADS * TQC, MIX_W), F32),
            pltpu.VMEM((N_HEADS * TQC, TK), F32),
        ],
        compiler_params=_cparams("parallel", "arbitrary"),
        name="sb_attn",
    )(qkv, qkv, qkv, u)


def _fox_kernel(q_ref, k_ref, v_ref, cum_ref, o_ref, za_ref, zb_ref, ma_ref, mb_ref, l_ref, acc_ref):
    i = pl.program_id(1)
    n_q = pl.num_programs(1) - 1

    def front_block(qs, kb, diagonal, z_ref, mx_ref):
        ks = pl.multiple_of(kb * TK, TK)
        cum = cum_ref[0, :, pl.ds(ks, TK)]
        decay = jnp.concatenate(
            [jnp.broadcast_to(cum[h:h + 1], (TQC, TK)) for h in range(N_HEADS)], axis=0)
        z = _dot_nt(qs, k_ref[0, 0, pl.ds(ks, TK), :]) - decay
        if diagonal:
            row, col = _block_positions(i, ks)
            z = jnp.where(col <= row, z, NEG_BIG)
        z_ref[kb] = z
        zmax = jnp.maximum(z[:, :LANES], z[:, LANES:])
        mx_ref[...] = zmax if diagonal else jnp.maximum(mx_ref[...], zmax)

    def back_block(kb, z_ref, mx_ref):
        ks = pl.multiple_of(kb * TK, TK)
        m = mx_ref[...]
        p = jnp.exp(z_ref[kb] - jnp.concatenate([m, m], axis=1))
        l_ref[...] += p[:, :LANES] + p[:, LANES:]
        acc_ref[...] += _dot(p.astype(BF16), v_ref[0, 0, pl.ds(ks, TK), :])

    def step(z_cur, m_cur, z_prev, m_prev):
        @pl.when(i < n_q)
        def _():
            qs = _stack_heads(q_ref[0, 0])
            front_block(qs, i, True, z_cur, m_cur)

            def both(kb, carry):
                front_block(qs, kb, False, z_cur, m_cur)
                back_block(kb, z_prev, m_prev)
                return carry

            lax.fori_loop(0, i, both, 0)
            m = jnp.max(m_cur[...], axis=-1, keepdims=True)
            m_cur[...] = jnp.broadcast_to(m, m_cur.shape)

        @pl.when(i == n_q)
        def _():
            def back_only(kb, carry):
                back_block(kb, z_prev, m_prev)
                return carry

            lax.fori_loop(0, i, back_only, 0)

    l_ref[...] = jnp.zeros_like(l_ref)
    acc_ref[...] = jnp.zeros_like(acc_ref)
    pl.when(i % 2 == 0)(functools.partial(step, za_ref, ma_ref, zb_ref, mb_ref))
    pl.when(i % 2 == 1)(functools.partial(step, zb_ref, mb_ref, za_ref, ma_ref))

    @pl.when(i > 0)
    def _():
        l = jnp.sum(l_ref[...], axis=-1, keepdims=True)
        o_ref[0] = _unstack_heads(acc_ref[...] / l, TQC).astype(o_ref.dtype)


def _fox_attention(qkv, cum):
    _, b, s, _ = qkv.shape
    assert TQC == TK
    n_q = s // TQC
    rows = N_HEADS * TQC
    return pl.pallas_call(
        _fox_kernel,
        grid=(b, n_q + 1),
        in_specs=[
            pl.BlockSpec((1, 1, TQC, MIX_W), lambda bi, i: (0, bi, jnp.minimum(i, n_q - 1), 0)),
            pl.BlockSpec((1, 1, s, MIX_W), lambda bi, i: (1, bi, 0, 0)),
            pl.BlockSpec((1, 1, s, MIX_W), lambda bi, i: (2, bi, 0, 0)),
            pl.BlockSpec((1, 8, s), lambda bi, i: (bi, 0, 0)),
        ],
        out_specs=pl.BlockSpec((1, TQC, MIX_W), lambda bi, i: (bi, jnp.maximum(i - 1, 0), 0)),
        out_shape=jax.ShapeDtypeStruct((b, s, MIX_W), BF16),
        scratch_shapes=[
            pltpu.VMEM((s // TK, rows, TK), F32),
            pltpu.VMEM((s // TK, rows, TK), F32),
            pltpu.VMEM((rows, LANES), F32),
            pltpu.VMEM((rows, LANES), F32),
            pltpu.VMEM((rows, LANES), F32),
            pltpu.VMEM((rows, MIX_W), F32),
        ],
        compiler_params=_cparams("parallel", "arbitrary"),
        name="fox_attn",
    )(qkv, qkv, qkv, cum)


def _t5_bucket(dist):
    n = jnp.maximum(dist, 0)
    max_exact = NUM_BUCKETS // 2
    nf = jnp.maximum(n, 1).astype(F32)
    large = max_exact + (jnp.log(nf / max_exact) / math.log(MAX_DISTANCE / max_exact)
                         * (NUM_BUCKETS - max_exact)).astype(jnp.int32)
    large = jnp.minimum(large, NUM_BUCKETS - 1)
    return jnp.where(n < max_exact, n, large)


def _bias_kernel(idx_ref, rb_ref, o_ref):
    for d in range(len(DILATIONS)):
        for variant in range(2):
            idx = idx_ref[d, variant]
            for h in range(N_HEADS):
                acc = jnp.full(idx.shape, NEG_BIG, F32)
                for bucket in range(NUM_BUCKETS):
                    acc = jnp.where(idx == bucket, rb_ref[bucket, h], acc)
                o_ref[d, variant, h] = acc


def _bias_tables(rel_bias):
    t = jnp.arange(TQ)[:, None]
    c = jnp.arange(2 * TQ)[None, :]
    delta = TQ + t - c
    valid = (delta >= 0) & (delta <= DIL_STEPS)
    idx = jnp.stack([jnp.where(valid, _t5_bucket(delta * dil), -1) for dil in DILATIONS])
    idx_first = jnp.concatenate([idx[:, :, TQ:], jnp.full_like(idx[:, :, TQ:], -1)], axis=-1)
    idx = jnp.stack([idx, idx_first], axis=1)
    out = pl.pallas_call(
        _bias_kernel,
        in_specs=[
            pl.BlockSpec(memory_space=pltpu.VMEM),
            pl.BlockSpec(memory_space=pltpu.SMEM),
        ],
        out_specs=pl.BlockSpec(memory_space=pltpu.VMEM),
        out_shape=jax.ShapeDtypeStruct((len(DILATIONS), 2, N_HEADS, TQ, 2 * TQ), F32),
        name="dil_bias",
    )(idx.astype(jnp.int32), rel_bias)
    return out.reshape(len(DILATIONS), 2, N_HEADS * TQ, 2 * TQ)


DIL_GROUP = 4


def _dil_kernel(q_ref, k_ref, v_ref, t_ref, o_ref, lse_ref, *, by_class, n_groups):
    g = pl.program_id(2)

    def attend(jj, scores, v_win):
        m = jnp.max(scores, axis=-1, keepdims=True)
        p = jnp.exp(scores - m)
        l = jnp.sum(p, axis=-1, keepdims=True)
        o = _unstack_heads(_dot(p.astype(BF16), v_win) / l, TQ).astype(o_ref.dtype)
        lse = _unstack_cols(m + jnp.log(l), TQ)
        if by_class:
            o_ref[0, jj] = o
            lse_ref[0, jj] = lse
        else:
            o_ref[0, 0, jj * TQ:(jj + 1) * TQ] = o
            lse_ref[0, 0, jj * TQ:(jj + 1) * TQ] = lse

    pending = []
    for jj in range(DIL_GROUP):
        if by_class:
            scores = (_dot_nt(_stack_heads(q_ref[0, 0, jj]), k_ref[0, 0, jj])
                      + t_ref[0, 0, :, TQ:2 * TQ])
            pending.append((scores, v_ref[0, 0, jj]))
            continue
        qs = _stack_heads(q_ref[0, 0, 0, jj * TQ:(jj + 1) * TQ])
        if n_groups == 1 or jj > 0:
            n = g * DIL_GROUP + jj if n_groups > 1 else jj
            start = (n - 1) * TQ if jj > 0 else 0
            bias = t_ref[0, 0 if jj > 0 else 1]
        else:
            start = jnp.maximum(g * DIL_GROUP - 1, 0) * TQ
            bias = t_ref[0, (g == 0).astype(jnp.int32)]
        if not isinstance(start, int):
            start = pl.multiple_of(start, TQ)
        window = pl.ds(start, 2 * TQ)
        pending.append((_dot_nt(qs, k_ref[0, 0, 0, window, :]) + bias, v_ref[0, 0, 0, window, :]))
    for jj, (scores, v_win) in enumerate(pending):
        attend(jj, scores, v_win)


def _dil_branch(qkv, tables, branch):
    _, b, dil, cls_len, _ = qkv.shape
    n_blocks = cls_len // TQ
    by_class = n_blocks == 1
    if by_class:
        grid = (b, dil // DIL_GROUP, 1)
        qkv_specs = [pl.BlockSpec((1, 1, DIL_GROUP, TQ, MIX_W), functools.partial(
            lambda bi, c, g, j: (j, bi, c, 0, 0), j=j)) for j in range(3)]
        out_spec = pl.BlockSpec((1, DIL_GROUP, TQ, MIX_W), lambda bi, c, g: (bi, c, 0, 0))
    else:
        grid = (b, dil, n_blocks // DIL_GROUP)
        qkv_specs = [pl.BlockSpec((1, 1, 1, DIL_GROUP * TQ, MIX_W), lambda bi, r, g: (0, bi, r, g, 0))]
        qkv_specs += [pl.BlockSpec((1, 1, 1, cls_len, MIX_W), functools.partial(
            lambda bi, r, g, j: (j, bi, r, 0, 0), j=j)) for j in (1, 2)]
        out_spec = pl.BlockSpec((1, 1, DIL_GROUP * TQ, MIX_W), lambda bi, r, g: (bi, r, g, 0))
    o, lse = pl.pallas_call(
        functools.partial(_dil_kernel, by_class=by_class, n_groups=grid[2]),
        grid=grid,
        in_specs=qkv_specs + [
            pl.BlockSpec((1, 2, N_HEADS * TQ, 2 * TQ), lambda bi, r, g: (branch, 0, 0, 0))],
        out_specs=[out_spec, out_spec],
        out_shape=[jax.ShapeDtypeStruct((b, dil, cls_len, MIX_W), BF16),
                   jax.ShapeDtypeStruct((b, dil, cls_len, MIX_W), F32)],
        compiler_params=_cparams("parallel", "parallel", "arbitrary"),
        name=f"dil_attn_{dil}",
    )(qkv, qkv, qkv, tables)
    to_seq = lambda a: a.transpose(0, 2, 1, 3).reshape(b, dil * cls_len, MIX_W)
    return to_seq(o), to_seq(lse)


def _dil_combine_kernel(o1, l1, o2, l2, o3, l3, out_ref):
    a1, a2, a3 = l1[0], l2[0], l3[0]
    m = jnp.maximum(jnp.maximum(a1, a2), a3)
    e1, e2, e3 = jnp.exp(a1 - m), jnp.exp(a2 - m), jnp.exp(a3 - m)
    tot = e1 + e2 + e3
    out = (e1 / tot) * o1[0] + (e2 / tot) * o2[0] + (e3 / tot) * o3[0]
    out_ref[0] = out.astype(out_ref.dtype)


def _dil_combine(branches, ts=512):
    b, s, w = branches[0][0].shape
    spec = pl.BlockSpec((1, ts, w), lambda bi, i: (bi, i, 0))
    flat = [a for pair in branches for a in pair]
    return pl.pallas_call(
        _dil_combine_kernel,
        grid=(b, s // ts),
        in_specs=[spec] * 6,
        out_specs=spec,
        out_shape=jax.ShapeDtypeStruct((b, s, w), BF16),
        compiler_params=_cparams("parallel", "parallel"),
        name="dil_combine",
    )(*flat)


def _shift_rows(x, k, fill):
    row = lax.broadcasted_iota(jnp.int32, x.shape, 0)
    return jnp.where(row >= k, pltpu.roll(x, k, 0), fill)


def _lru_kernel(xg_ref, cw_ref, cb_ref, wa_ref, ba_ref, wx_ref, bx_ref, lam_ref, o_ref, h_ref):
    x = xg_ref[0, 0]
    gate = xg_ref[1, 0]
    s = x.shape[0]
    kw = cw_ref.shape[0]
    xc = cb_ref[...] + cw_ref[kw - 1:kw, :] * x
    for j in range(kw - 1):
        xc = xc + cw_ref[j:j + 1, :] * _shift_rows(x, kw - 1 - j, 0.0)
    xcb = xc.astype(BF16)
    r = jax.nn.sigmoid(_dot(xcb, wa_ref[...]) + ba_ref[...])
    i_gate = jax.nn.sigmoid(_dot(xcb, wx_ref[...]) + bx_ref[...])
    a = jnp.exp(-LRU_C * r * _softplus(-lam_ref[...]))
    h = jnp.sqrt(1.0 - a * a) * (i_gate * xc)
    in_group = lax.broadcasted_iota(jnp.int32, x.shape, 0) % SUBLANES
    k = 1
    while k < SUBLANES:
        prev = in_group >= k
        h = a * jnp.where(prev, pltpu.roll(h, k, 0), 0.0) + h
        a = a * jnp.where(prev, pltpu.roll(a, k, 0), 1.0)
        k *= 2
    carry = jnp.zeros((1, x.shape[1]), F32)
    for c in range(s // SUBLANES):
        rows = slice(c * SUBLANES, (c + 1) * SUBLANES)
        hc = h[rows] + a[rows] * carry
        h_ref[rows] = hc
        carry = hc[SUBLANES - 1:SUBLANES]
    o_ref[0] = (h_ref[...] * _gelu(gate)).astype(o_ref.dtype)


def _lru(xg, conv_w, conv_b, wa_bd, b_a, wx_bd, b_x, lam):
    _, b, s, w = xg.shape
    full = lambda a: pl.BlockSpec(a.shape, lambda bi: (0,) * a.ndim)
    args = (conv_w, conv_b, wa_bd, b_a, wx_bd, b_x, lam)
    return pl.pallas_call(
        _lru_kernel,
        grid=(b,),
        in_specs=[pl.BlockSpec((2, 1, s, w), lambda bi: (0, bi, 0, 0))] + [full(a) for a in args],
        out_specs=pl.BlockSpec((1, s, w), lambda bi: (bi, 0, 0)),
        out_shape=jax.ShapeDtypeStruct((b, s, w), BF16),
        scratch_shapes=[pltpu.VMEM((s, w), F32)],
        compiler_params=_cparams("parallel"),
        name="rg_lru",
    )(xg, *args)


def _memkv_kernel(mem_ref, g_ref, wk_ref, wv_ref, k_ref, v_ref):
    mn = _rms(mem_ref[0], g_ref[...]).astype(BF16)
    k_ref[0] = _dot(mn, wk_ref[...]).astype(BF16)
    v_ref[0] = _dot(mn, wv_ref[...]).astype(BF16)


def _memkv(mem, g, w_ck, w_cv):
    b, m, d = mem.shape
    out = jax.ShapeDtypeStruct((b, m, MIX_W), BF16)
    return pl.pallas_call(
        _memkv_kernel,
        grid=(b,),
        in_specs=[
            pl.BlockSpec((1, m, d), lambda bi: (bi, 0, 0)),
            pl.BlockSpec((1, d), lambda bi: (0, 0)),
            pl.BlockSpec((d, MIX_W), lambda bi: (0, 0)),
            pl.BlockSpec((d, MIX_W), lambda bi: (0, 0)),
        ],
        out_specs=[pl.BlockSpec((1, m, MIX_W), lambda bi: (bi, 0, 0))] * 2,
        out_shape=[out, out],
        compiler_params=_cparams("parallel"),
        name="mem_kv",
    )(mem, g, w_ck, w_cv)


def _mix_cross_kernel(x_ref, sb_ref, fox_ref, dil_ref, lru_ref, wo_ref, g_ref, wq_ref,
                      km_ref, vm_ref, wco_ref, o_ref):
    t = MIX_SUB
    subs = [slice(u * t, (u + 1) * t) for u in range(x_ref.shape[1] // t)]
    xs = []
    for r in subs:
        x = x_ref[0, r]
        for j, ref in enumerate((sb_ref, fox_ref, dil_ref, lru_ref)):
            x = x + _dot(ref[0, r], wo_ref[j])
        xs.append(x)
    qs = [_stack_heads(_dot(_rms(x, g_ref[...]).astype(BF16), wq_ref[...])) for x in xs]
    scores = [_dot_nt(q, km_ref[0]) for q in qs]
    outs = []
    for sc in scores:
        p = jnp.exp(sc - jnp.max(sc, axis=-1, keepdims=True))
        p = p / jnp.sum(p, axis=-1, keepdims=True)
        outs.append(_unstack_heads(_dot(p.astype(BF16), vm_ref[0]), t))
    for r, x, o in zip(subs, xs, outs):
        o_ref[0, r] = x + _dot(o.astype(BF16), wco_ref[...])


MIX_SUB = 256


def _mix_cross(x, mixers, w_out, g, w_cq, k_mem, v_mem, w_co, ts=512):
    b, s, d = x.shape
    m = k_mem.shape[1]
    mix_spec = pl.BlockSpec((1, ts, MIX_W), lambda bi, i: (bi, i, 0))
    const = lambda a: pl.BlockSpec(a.shape, lambda bi, i: (0,) * a.ndim)
    mem_spec = pl.BlockSpec((1, m, MIX_W), lambda bi, i: (bi, 0, 0))
    return pl.pallas_call(
        _mix_cross_kernel,
        grid=(b, s // ts),
        in_specs=[pl.BlockSpec((1, ts, d), lambda bi, i: (bi, i, 0))] + [mix_spec] * 4
                 + [const(w_out), const(g), const(w_cq), mem_spec, mem_spec, const(w_co)],
        out_specs=pl.BlockSpec((1, ts, d), lambda bi, i: (bi, i, 0)),
        out_shape=jax.ShapeDtypeStruct((b, s, d), F32),
        compiler_params=_cparams("parallel", "parallel"),
        name="mix_cross",
    )(x, *mixers, w_out, g, w_cq, k_mem, v_mem, w_co)


HALO = 8


def _ffn_kernel(x_ref, halo_ref, g_ref, wup_ref, cw_ref, cb_ref, wdn_ref, gf_ref, o_ref,
                ext_ref, act_ref, acc_ref, *, final_norm):
    i = pl.program_id(1)
    x = x_ref[0]
    t = x.shape[0]
    g = g_ref[...]
    halo = jnp.where(i > 0, _rms(halo_ref[0], g), 0.0)
    hn = jnp.concatenate([halo, _rms(x, g)], axis=0).astype(BF16)
    acc_ref[...] = jnp.zeros_like(acc_ref)

    def project(c):
        for half in range(2):
            col = (half * N_FF_CHUNKS + c) * FF_CHUNK
            ext_ref[2 * (c % 2) + half] = _dot(hn, wup_ref[:, col:col + FF_CHUNK])

    def conv(c, slot):
        ext = ext_ref.at[slot]
        w = cw_ref[c]
        return (cb_ref[c] + w[2:3] * ext[pl.ds(HALO, t), :]
                + w[1:2] * ext[pl.ds(HALO - 1, t), :]
                + w[0:1] * ext[pl.ds(HALO - 2, t), :])

    def down(c):
        acc_ref[...] += _dot(act_ref[c % 2], wdn_ref[c])

    project(0)
    for c in range(N_FF_CHUNKS):
        if c + 1 < N_FF_CHUNKS:
            project(c + 1)
        if c >= 1:
            down(c - 1)
        up = conv(c, 2 * (c % 2))
        gate = conv(N_FF_CHUNKS + c, 2 * (c % 2) + 1)
        act_ref[c % 2] = (_gelu(gate) * up).astype(BF16)
    down(N_FF_CHUNKS - 1)
    out = x + acc_ref[...]
    if final_norm:
        out = _rms(out, gf_ref[...])
    o_ref[0] = out


def _ffn(x, g, w_up, conv_w, conv_b, w_down, g_final, final_norm, ts=256):
    b, s, d = x.shape
    per_tile = ts // HALO
    const = lambda a: pl.BlockSpec(a.shape, lambda bi, i: (0,) * a.ndim)
    return pl.pallas_call(
        functools.partial(_ffn_kernel, final_norm=final_norm),
        grid=(b, s // ts),
        in_specs=[
            pl.BlockSpec((1, ts, d), lambda bi, i: (bi, i, 0)),
            pl.BlockSpec((1, HALO, d), lambda bi, i: (bi, jnp.maximum(i * per_tile - 1, 0), 0)),
            const(g), const(w_up), const(conv_w), const(conv_b), const(w_down), const(g_final),
        ],
        out_specs=pl.BlockSpec((1, ts, d), lambda bi, i: (bi, i, 0)),
        out_shape=jax.ShapeDtypeStruct((b, s, d), F32),
        scratch_shapes=[
            pltpu.VMEM((4, HALO + ts, FF_CHUNK), F32),
            pltpu.VMEM((2, ts, FF_CHUNK), BF16),
            pltpu.VMEM((ts, d), F32),
        ],
        compiler_params=_cparams("parallel", "arbitrary"),
        name="conv_geglu",
    )(x, x, g, w_up, conv_w, conv_b, w_down, g_final)


def _block_diag(w):
    g, n, _ = w.shape
    eye = jnp.eye(g, dtype=w.dtype)
    return (eye[:, None, :, None] * w[:, :, None, :]).reshape(g * n, g * n)


def kernel(x, mem, norm_mix_g, w_in, b_forget, lru_conv_w, lru_conv_b, lru_w_a, lru_b_a, lru_w_x, lru_b_x, lru_lambda, w_out, norm_cross_g, norm_mem_g, w_cq, w_ck, w_cv, w_co, norm_ffn_g, w_up, ffn_conv_w, ffn_conv_b, w_down, rel_bias, final_norm_g):
    depth = w_in.shape[0]
    qkv_w = 3 * MIX_W
    q_scale = HEAD_DIM ** -0.5
    col_scale = jnp.ones((3, 3, MIX_W), F32).at[:, 0, :].set(q_scale).reshape(-1)
    col_scale = jnp.concatenate([col_scale, jnp.ones((2 * MIX_W,), F32)])
    tables = _bias_tables(rel_bias)
    row = lambda v: v.reshape(1, -1)

    for l in range(depth):
        w = w_in[l]
        n_f = N_HEADS
        w_main = jnp.concatenate([w[:, :2 * qkv_w], w[:, 2 * qkv_w + n_f:]], axis=1)
        w_main = (w_main * col_scale).astype(BF16)
        w_f = jnp.zeros((8, D_MODEL), F32).at[:n_f].set(w[:, 2 * qkv_w:2 * qkv_w + n_f].T).astype(BF16)
        b_f = jnp.zeros((8, 1), F32).at[:n_f, 0].set(b_forget[l])

        sb_qkv, fox_qkv, *dil_qkvs, lru_xg, f_t = _inproj(x, row(norm_mix_g[l]), w_main, w_f)
        o_sb = _sb_attention(sb_qkv)
        o_fox = _fox_attention(fox_qkv, _decay(f_t, b_f))
        o_dil = _dil_combine([_dil_branch(qkv, tables, br) for br, qkv in enumerate(dil_qkvs)])
        o_lru = _lru(lru_xg, lru_conv_w[l], row(lru_conv_b[l]),
                     _block_diag(lru_w_a[l]).astype(BF16), row(lru_b_a[l]),
                     _block_diag(lru_w_x[l]).astype(BF16), row(lru_b_x[l]), row(lru_lambda[l]))

        k_mem, v_mem = _memkv(mem, row(norm_mem_g[l]), w_ck[l].astype(BF16), w_cv[l].astype(BF16))
        x = _mix_cross(x, (o_sb, o_fox, o_dil, o_lru),
                       w_out[l].reshape(4, MIX_W, D_MODEL).astype(BF16), row(norm_cross_g[l]),
                       (w_cq[l] * q_scale).astype(BF16), k_mem, v_mem, w_co[l].astype(BF16))

        wu = w_up[l].astype(BF16)
        cw = ffn_conv_w[l].reshape(-1, 2 * N_FF_CHUNKS, FF_CHUNK).transpose(1, 0, 2)
        cb = ffn_conv_b[l].reshape(2 * N_FF_CHUNKS, 1, FF_CHUNK)
        wd = w_down[l].reshape(N_FF_CHUNKS, FF_CHUNK, D_MODEL).astype(BF16)
        x = _ffn(x, row(norm_ffn_g[l]), wu, cw, cb, wd, row(final_norm_g), l == depth - 1)
    return x
```

```python
import functools
import math

import jax
import jax.numpy as jnp
from jax import lax
from jax.experimental import pallas as pl
from jax.experimental.pallas import tpu as pltpu

F32 = jnp.float32
BF16 = jnp.bfloat16

LANES = 128
SUBLANES = 8
D_MODEL = 1024
HEAD_DIM = 64
N_HEADS = 4
MIX_W = N_HEADS * HEAD_DIM
D_FF = 2816
FF_CHUNK = 256
N_FF_CHUNKS = D_FF // FF_CHUNK
N_MEM = 256
NUM_BUCKETS = 32
MAX_DISTANCE = 2048
DILATIONS = (1, 4, 16)
DIL_STEPS = 128
LRU_C = 8.0
EPS = 1e-6
NEG_BIG = -1e30

TQ = 128
TQC = 256
TK = 256
VMEM_LIMIT = 56 * 1024 * 1024

NT_DIMS = (((1,), (1,)), ((), ()))


def _cparams(*sem):
    return pltpu.CompilerParams(dimension_semantics=sem, vmem_limit_bytes=VMEM_LIMIT)


def _rms(x, g):
    return x * lax.rsqrt(jnp.mean(x * x, axis=-1, keepdims=True) + EPS) * g


def _softplus(x):
    return jnp.maximum(x, 0.0) + jnp.log(1.0 + jnp.exp(-jnp.abs(x)))


def _gelu(x):
    return 0.5 * x * (1.0 + lax.erf(x * math.sqrt(0.5)))


def _dot(a, b):
    return jnp.dot(a, b, preferred_element_type=F32)


def _dot_nt(a, b):
    return lax.dot_general(a, b, NT_DIMS, preferred_element_type=F32)


def _split2(x):
    hi = x.astype(BF16)
    lo = (x - hi.astype(F32)).astype(BF16)
    return hi, lo


def _head_of_lane(rows):
    return lax.broadcasted_iota(jnp.int32, (rows, MIX_W), 1) // HEAD_DIM


def _stack_heads(q):
    t = q.shape[0]
    head = _head_of_lane(t)
    qf = q.astype(F32)
    return jnp.concatenate([jnp.where(head == h, qf, 0.0) for h in range(N_HEADS)], axis=0).astype(BF16)


def _unstack_heads(o, t):
    head = _head_of_lane(t)
    out = jnp.where(head == 0, o[0:t], 0.0)
    for h in range(1, N_HEADS):
        out = jnp.where(head == h, o[h * t:(h + 1) * t], out)
    return out


def _unstack_cols(c, t):
    head = _head_of_lane(t)
    out = jnp.broadcast_to(c[0:t], (t, MIX_W))
    for h in range(1, N_HEADS):
        out = jnp.where(head == h, jnp.broadcast_to(c[h * t:(h + 1) * t], (t, MIX_W)), out)
    return out


def _inproj_kernel(x_ref, g_ref, w_ref, wf_ref, sb_ref, fox_ref, d1_ref, d4_ref, d16_ref,
                   lru_ref, f_ref, half_ref):
    hn = _rms(x_ref[0], g_ref[...]).astype(BF16)
    ts = hn.shape[0]
    for m, ref in enumerate((sb_ref, fox_ref)):
        for j in range(3):
            c = (3 * m + j) * MIX_W
            ref[j, 0] = _dot(hn, w_ref[:, c:c + MIX_W]).astype(BF16)
    for j in range(3):
        c = (6 + j) * MIX_W
        res = _dot(hn, w_ref[:, c:c + MIX_W])
        d1_ref[j, 0, 0] = res.astype(BF16)
        for h in range(2):
            half_ref[h] = res[:, h * LANES:(h + 1) * LANES]
        for ref, dil in ((d4_ref, DILATIONS[1]), (d16_ref, DILATIONS[2])):
            for r in range(dil):
                for h in range(2):
                    ref[j, 0, r, :, h * LANES:(h + 1) * LANES] = (
                        half_ref[h, pl.ds(r, ts // dil, stride=dil), :].astype(BF16))
    for j in range(2):
        c = (9 + j) * MIX_W
        lru_ref[j, 0] = _dot(hn, w_ref[:, c:c + MIX_W])
    f_ref[0] = _dot_nt(wf_ref[...], hn)


def _inproj(x, g, w_main, w_f, ts=512):
    b, s, d = x.shape
    qkv_shape = jax.ShapeDtypeStruct((3, b, s, MIX_W), BF16)
    qkv_spec = pl.BlockSpec((3, 1, ts, MIX_W), lambda bi, i: (0, bi, i, 0))
    cls_shapes = [jax.ShapeDtypeStruct((3, b, dil, s // dil, MIX_W), BF16) for dil in DILATIONS]
    cls_specs = [pl.BlockSpec((3, 1, dil, ts // dil, MIX_W), lambda bi, i: (0, bi, 0, i, 0))
                 for dil in DILATIONS]
    return pl.pallas_call(
        _inproj_kernel,
        grid=(b, s // ts),
        in_specs=[
            pl.BlockSpec((1, ts, d), lambda bi, i: (bi, i, 0)),
            pl.BlockSpec((1, d), lambda bi, i: (0, 0)),
            pl.BlockSpec(w_main.shape, lambda bi, i: (0, 0)),
            pl.BlockSpec(w_f.shape, lambda bi, i: (0, 0)),
        ],
        out_specs=[
            qkv_spec, qkv_spec, *cls_specs,
            pl.BlockSpec((2, 1, ts, MIX_W), lambda bi, i: (0, bi, i, 0)),
            pl.BlockSpec((1, 8, ts), lambda bi, i: (bi, 0, i)),
        ],
        out_shape=[
            qkv_shape, qkv_shape, *cls_shapes,
            jax.ShapeDtypeStruct((2, b, s, MIX_W), F32),
            jax.ShapeDtypeStruct((b, 8, s), F32),
        ],
        scratch_shapes=[pltpu.VMEM((2, ts, LANES), F32)],
        compiler_params=_cparams("parallel", "parallel"),
        name="inproj",
    )(x, g, w_main, w_f)


def _decay_kernel(f_ref, bf_ref, tri_ref, o_ref):
    s = f_ref.shape[2]
    tri = tri_ref[...]
    carry = jnp.zeros((8, 1), F32)
    for c in range(s // 128):
        sl = slice(c * 128, (c + 1) * 128)
        log_f = -_softplus(-(f_ref[0, :, sl] + bf_ref[...]))
        p1 = log_f.astype(BF16)
        r1 = log_f - p1.astype(F32)
        p2 = r1.astype(BF16)
        p3 = (r1 - p2.astype(F32)).astype(BF16)
        inc = _dot(p1, tri) + _dot(p2, tri) + _dot(p3, tri)
        o_ref[0, :, sl] = inc + carry
        carry = carry + jnp.sum(log_f, axis=-1, keepdims=True)


def _decay(f_t, b_f):
    b, _, s = f_t.shape
    tri = (lax.broadcasted_iota(jnp.int32, (128, 128), 0)
           <= lax.broadcasted_iota(jnp.int32, (128, 128), 1)).astype(BF16)
    return pl.pallas_call(
        _decay_kernel,
        grid=(b,),
        in_specs=[
            pl.BlockSpec((1, 8, s), lambda bi: (bi, 0, 0)),
            pl.BlockSpec((8, 1), lambda bi: (0, 0)),
            pl.BlockSpec((128, 128), lambda bi: (0, 0)),
        ],
        out_specs=pl.BlockSpec((1, 8, s), lambda bi: (bi, 0, 0)),
        out_shape=jax.ShapeDtypeStruct((b, 8, s), F32),
        compiler_params=_cparams("parallel"),
        name="fox_decay",
    )(f_t, b_f, tri)


def _block_positions(i, ks):
    row = lax.broadcasted_iota(jnp.int32, (N_HEADS * TQC, TK), 0) % TQC + i * TQC
    col = lax.broadcasted_iota(jnp.int32, (N_HEADS * TQC, TK), 1) + ks
    return row, col


SB_SATURATED = 110.0


def _sb_kernel(q_ref, k_ref, v_ref, uo_ref, o_ref, acc_ref, car_ref):
    i = pl.program_id(1)
    qs = _stack_heads(q_ref[0, 0])
    k_diag = (i * TQC) // TK

    half_rows = N_HEADS * TQC // 2
    halves = [slice(0, half_rows), slice(half_rows, 2 * half_rows)]

    def block(kb, diagonal):
        ks = pl.multiple_of(kb * TK, TK)
        k_blk = k_ref[0, 0, pl.ds(ks, TK), :]
        v_blk = v_ref[0, 0, pl.ds(ks, TK), :]
        if diagonal:
            row, col = _block_positions(i, ks)
            strict = (col < row)[:half_rows]
        zs = [_dot_nt(qs[r], k_blk) for r in halves]
        drops, sums = [], []
        for z in zs:
            drop = _softplus(z)
            cum_in = jnp.where(strict, drop, 0.0) if diagonal else drop
            sums.append(_dot(cum_in.astype(BF16), uo_ref[...]))
            drops.append(drop)
        for r, z, drop, sm in zip(halves, zs, drops, sums):
            if diagonal:
                att = jnp.where(strict, jnp.exp(z - drop - sm[:, :TK]), 0.0)
                car_ref[r] = sm[:, TK:]
                acc_ref[r] = _dot(att.astype(BF16), v_blk)
            else:
                car = car_ref[r]
                att = jnp.exp(z - drop - sm[:, :TK] - car)
                car_ref[r] = car + sm[:, TK:]
                acc_ref[r] += _dot(att.astype(BF16), v_blk)
        return jnp.min(car_ref[:, :LANES]) < SB_SATURATED

    def cond(state):
        kb, live = state
        return jnp.logical_and(kb >= 0, live)

    def body(state):
        kb, _ = state
        return kb - 1, block(kb, False)

    lax.while_loop(cond, body, (k_diag - 1, block(k_diag, True)))
    o_ref[0] = _unstack_heads(acc_ref[...], TQC).astype(o_ref.dtype)


def _sb_attention(qkv):
    _, b, s, _ = qkv.shape
    later = (lax.broadcasted_iota(jnp.int32, (TK, TK), 0)
             > lax.broadcasted_iota(jnp.int32, (TK, TK), 1))
    u = jnp.concatenate([later, jnp.ones((TK, TK), bool)], axis=1).astype(BF16)
    return pl.pallas_call(
        _sb_kernel,
        grid=(b, s // TQC),
        in_specs=[
            pl.BlockSpec((1, 1, TQC, MIX_W), lambda bi, i: (0, bi, i, 0)),
            pl.BlockSpec((1, 1, s, MIX_W), lambda bi, i: (1, bi, 0, 0)),
            pl.BlockSpec((1, 1, s, MIX_W), lambda bi, i: (2, bi, 0, 0)),
            pl.BlockSpec((TK, 2 * TK), lambda bi, i: (0, 0)),
        ],
        out_specs=pl.BlockSpec((1, TQC, MIX_W), lambda bi, i: (bi, i, 0)),
        out_shape=jax.ShapeDtypeStruct((b, s, MIX_W), BF16),
        scratch_shapes=[
            pltpu.VMEM((N_HEADS * TQC, MIX_W), F32),
            pltpu.VMEM((N_HEADS * TQC, TK), F32),
        ],
        compiler_params=_cparams("parallel", "arbitrary"),
        name="sb_attn",
    )(qkv, qkv, qkv, u)


def _fox_kernel(q_ref, k_ref, v_ref, cum_ref, o_ref, za_ref, zb_ref, ma_ref, mb_ref, l_ref, acc_ref):
    i = pl.program_id(1)
    n_q = pl.num_programs(1) - 1

    def front_block(qs, kb, diagonal, z_ref, mx_ref):
        ks = pl.multiple_of(kb * TK, TK)
        cum = cum_ref[0, :, pl.ds(ks, TK)]
        decay = jnp.concatenate(
            [jnp.broadcast_to(cum[h:h + 1], (TQC, TK)) for h in range(N_HEADS)], axis=0)
        z = _dot_nt(qs, k_ref[0, 0, pl.ds(ks, TK), :]) - decay
        if diagonal:
            row, col = _block_positions(i, ks)
            z = jnp.where(col <= row, z, NEG_BIG)
        z_ref[kb] = z
        zmax = jnp.maximum(z[:, :LANES], z[:, LANES:])
        mx_ref[...] = zmax if diagonal else jnp.maximum(mx_ref[...], zmax)

    def back_block(kb, z_ref, mx_ref):
        ks = pl.multiple_of(kb * TK, TK)
        m = mx_ref[...]
        p = jnp.exp(z_ref[kb] - jnp.concatenate([m, m], axis=1))
        l_ref[...] += p[:, :LANES] + p[:, LANES:]
        acc_ref[...] += _dot(p.astype(BF16), v_ref[0, 0, pl.ds(ks, TK), :])

    def step(z_cur, m_cur, z_prev, m_prev):
        @pl.when(i < n_q)
        def _():
            qs = _stack_heads(q_ref[0, 0])
            front_block(qs, i, True, z_cur, m_cur)

            def both(kb, carry):
                front_block(qs, kb, False, z_cur, m_cur)
                back_block(kb, z_prev, m_prev)
                return carry

            lax.fori_loop(0, i, both, 0)
            m = jnp.max(m_cur[...], axis=-1, keepdims=True)
            m_cur[...] = jnp.broadcast_to(m, m_cur.shape)

        @pl.when(i == n_q)
        def _():
            def back_only(kb, carry):
                back_block(kb, z_prev, m_prev)
                return carry

            lax.fori_loop(0, i, back_only, 0)

    l_ref[...] = jnp.zeros_like(l_ref)
    acc_ref[...] = jnp.zeros_like(acc_ref)
    pl.when(i % 2 == 0)(functools.partial(step, za_ref, ma_ref, zb_ref, mb_ref))
    pl.when(i % 2 == 1)(functools.partial(step, zb_ref, mb_ref, za_ref, ma_ref))

    @pl.when(i > 0)
    def _():
        l = jnp.sum(l_ref[...], axis=-1, keepdims=True)
        o_ref[0] = _unstack_heads(acc_ref[...] / l, TQC).astype(o_ref.dtype)


def _fox_attention(qkv, cum):
    _, b, s, _ = qkv.shape
    assert TQC == TK
    n_q = s // TQC
    rows = N_HEADS * TQC
    return pl.pallas_call(
        _fox_kernel,
        grid=(b, n_q + 1),
        in_specs=[
            pl.BlockSpec((1, 1, TQC, MIX_W), lambda bi, i: (0, bi, jnp.minimum(i, n_q - 1), 0)),
            pl.BlockSpec((1, 1, s, MIX_W), lambda bi, i: (1, bi, 0, 0)),
            pl.BlockSpec((1, 1, s, MIX_W), lambda bi, i: (2, bi, 0, 0)),
            pl.BlockSpec((1, 8, s), lambda bi, i: (bi, 0, 0)),
        ],
        out_specs=pl.BlockSpec((1, TQC, MIX_W), lambda bi, i: (bi, jnp.maximum(i - 1, 0), 0)),
        out_shape=jax.ShapeDtypeStruct((b, s, MIX_W), BF16),
        scratch_shapes=[
            pltpu.VMEM((s // TK, rows, TK), F32),
            pltpu.VMEM((s // TK, rows, TK), F32),
            pltpu.VMEM((rows, LANES), F32),
            pltpu.VMEM((rows, LANES), F32),
            pltpu.VMEM((rows, LANES), F32),
            pltpu.VMEM((rows, MIX_W), F32),
        ],
        compiler_params=_cparams("parallel", "arbitrary"),
        name="fox_attn",
    )(qkv, qkv, qkv, cum)


def _t5_bucket(dist):
    n = jnp.maximum(dist, 0)
    max_exact = NUM_BUCKETS // 2
    nf = jnp.maximum(n, 1).astype(F32)
    large = max_exact + (jnp.log(nf / max_exact) / math.log(MAX_DISTANCE / max_exact)
                         * (NUM_BUCKETS - max_exact)).astype(jnp.int32)
    large = jnp.minimum(large, NUM_BUCKETS - 1)
    return jnp.where(n < max_exact, n, large)


def _bias_kernel(idx_ref, rb_ref, o_ref):
    for d in range(len(DILATIONS)):
        for variant in range(2):
            idx = idx_ref[d, variant]
            for h in range(N_HEADS):
                acc = jnp.full(idx.shape, NEG_BIG, F32)
                for bucket in range(NUM_BUCKETS):
                    acc = jnp.where(idx == bucket, rb_ref[bucket, h], acc)
                o_ref[d, variant, h] = acc


def _bias_tables(rel_bias):
    t = jnp.arange(TQ)[:, None]
    c = jnp.arange(2 * TQ)[None, :]
    delta = TQ + t - c
    valid = (delta >= 0) & (delta <= DIL_STEPS)
    idx = jnp.stack([jnp.where(valid, _t5_bucket(delta * dil), -1) for dil in DILATIONS])
    idx_first = jnp.concatenate([idx[:, :, TQ:], jnp.full_like(idx[:, :, TQ:], -1)], axis=-1)
    idx = jnp.stack([idx, idx_first], axis=1)
    out = pl.pallas_call(
        _bias_kernel,
        in_specs=[
            pl.BlockSpec(memory_space=pltpu.VMEM),
            pl.BlockSpec(memory_space=pltpu.SMEM),
        ],
        out_specs=pl.BlockSpec(memory_space=pltpu.VMEM),
        out_shape=jax.ShapeDtypeStruct((len(DILATIONS), 2, N_HEADS, TQ, 2 * TQ), F32),
        name="dil_bias",
    )(idx.astype(jnp.int32), rel_bias)
    return out.reshape(len(DILATIONS), 2, N_HEADS * TQ, 2 * TQ)


DIL_GROUP = 4


def _dil_kernel(q_ref, k_ref, v_ref, t_ref, o_ref, lse_ref, *, by_class, n_groups):
    g = pl.program_id(2)

    def attend(jj, scores, v_win):
        m = jnp.max(scores, axis=-1, keepdims=True)
        p = jnp.exp(scores - m)
        l = jnp.sum(p, axis=-1, keepdims=True)
        o = _unstack_heads(_dot(p.astype(BF16), v_win) / l, TQ).astype(o_ref.dtype)
        lse = _unstack_cols(m + jnp.log(l), TQ)
        if by_class:
            o_ref[0, jj] = o
            lse_ref[0, jj] = lse
        else:
            o_ref[0, 0, jj * TQ:(jj + 1) * TQ] = o
            lse_ref[0, 0, jj * TQ:(jj + 1) * TQ] = lse

    pending = []
    for jj in range(DIL_GROUP):
        if by_class:
            scores = (_dot_nt(_stack_heads(q_ref[0, 0, jj]), k_ref[0, 0, jj])
                      + t_ref[0, 0, :, TQ:2 * TQ])
            pending.append((scores, v_ref[0, 0, jj]))
            continue
        qs = _stack_heads(q_ref[0, 0, 0, jj * TQ:(jj + 1) * TQ])
        if n_groups == 1 or jj > 0:
            n = g * DIL_GROUP + jj if n_groups > 1 else jj
            start = (n - 1) * TQ if jj > 0 else 0
            bias = t_ref[0, 0 if jj > 0 else 1]
        else:
            start = jnp.maximum(g * DIL_GROUP - 1, 0) * TQ
            bias = t_ref[0, (g == 0).astype(jnp.int32)]
        if not isinstance(start, int):
            start = pl.multiple_of(start, TQ)
        window = pl.ds(start, 2 * TQ)
        pending.append((_dot_nt(qs, k_ref[0, 0, 0, window, :]) + bias, v_ref[0, 0, 0, window, :]))
    for jj, (scores, v_win) in enumerate(pending):
        attend(jj, scores, v_win)


def _dil_branch(qkv, tables, branch):
    _, b, dil, cls_len, _ = qkv.shape
    n_blocks = cls_len // TQ
    by_class = n_blocks == 1
    if by_class:
        grid = (b, dil // DIL_GROUP, 1)
        qkv_specs = [pl.BlockSpec((1, 1, DIL_GROUP, TQ, MIX_W), functools.partial(
            lambda bi, c, g, j: (j, bi, c, 0, 0), j=j)) for j in range(3)]
        out_spec = pl.BlockSpec((1, DIL_GROUP, TQ, MIX_W), lambda bi, c, g: (bi, c, 0, 0))
    else:
        grid = (b, dil, n_blocks // DIL_GROUP)
        qkv_specs = [pl.BlockSpec((1, 1, 1, DIL_GROUP * TQ, MIX_W), lambda bi, r, g: (0, bi, r, g, 0))]
        qkv_specs += [pl.BlockSpec((1, 1, 1, cls_len, MIX_W), functools.partial(
            lambda bi, r, g, j: (j, bi, r, 0, 0), j=j)) for j in (1, 2)]
        out_spec = pl.BlockSpec((1, 1, DIL_GROUP * TQ, MIX_W), lambda bi, r, g: (bi, r, g, 0))
    o, lse = pl.pallas_call(
        functools.partial(_dil_kernel, by_class=by_class, n_groups=grid[2]),
        grid=grid,
        in_specs=qkv_specs + [
            pl.BlockSpec((1, 2, N_HEADS * TQ, 2 * TQ), lambda bi, r, g: (branch, 0, 0, 0))],
        out_specs=[out_spec, out_spec],
        out_shape=[jax.ShapeDtypeStruct((b, dil, cls_len, MIX_W), BF16),
                   jax.ShapeDtypeStruct((b, dil, cls_len, MIX_W), F32)],
        compiler_params=_cparams("parallel", "parallel", "arbitrary"),
        name=f"dil_attn_{dil}",
    )(qkv, qkv, qkv, tables)
    to_seq = lambda a: a.transpose(0, 2, 1, 3).reshape(b, dil * cls_len, MIX_W)
    return to_seq(o), to_seq(lse)


def _dil_mixture(outs, lses):
    m = functools.reduce(jnp.maximum, lses)
    es = [jnp.exp(a - m) for a in lses]
    tot = functools.reduce(jnp.add, es)
    return functools.reduce(jnp.add, [(e / tot) * o for e, o in zip(es, outs)])


def _shift_rows(x, k, fill):
    row = lax.broadcasted_iota(jnp.int32, x.shape, 0)
    return jnp.where(row >= k, pltpu.roll(x, k, 0), fill)


def _lru_kernel(xg_ref, cw_ref, cb_ref, wa_ref, ba_ref, wx_ref, bx_ref, lam_ref, o_ref, h_ref):
    x = xg_ref[0, 0]
    gate = xg_ref[1, 0]
    s = x.shape[0]
    kw = cw_ref.shape[0]
    xc = cb_ref[...] + cw_ref[kw - 1:kw, :] * x
    for j in range(kw - 1):
        xc = xc + cw_ref[j:j + 1, :] * _shift_rows(x, kw - 1 - j, 0.0)
    xcb = xc.astype(BF16)
    r = jax.nn.sigmoid(_dot(xcb, wa_ref[...]) + ba_ref[...])
    i_gate = jax.nn.sigmoid(_dot(xcb, wx_ref[...]) + bx_ref[...])
    a = jnp.exp(-LRU_C * r * _softplus(-lam_ref[...]))
    h = jnp.sqrt(1.0 - a * a) * (i_gate * xc)
    in_group = lax.broadcasted_iota(jnp.int32, x.shape, 0) % SUBLANES
    k = 1
    while k < SUBLANES:
        prev = in_group >= k
        h = a * jnp.where(prev, pltpu.roll(h, k, 0), 0.0) + h
        a = a * jnp.where(prev, pltpu.roll(a, k, 0), 1.0)
        k *= 2
    carry = jnp.zeros((1, x.shape[1]), F32)
    for c in range(s // SUBLANES):
        rows = slice(c * SUBLANES, (c + 1) * SUBLANES)
        hc = h[rows] + a[rows] * carry
        h_ref[rows] = hc
        carry = hc[SUBLANES - 1:SUBLANES]
    o_ref[0] = (h_ref[...] * _gelu(gate)).astype(o_ref.dtype)


def _lru(xg, conv_w, conv_b, wa_bd, b_a, wx_bd, b_x, lam):
    _, b, s, w = xg.shape
    full = lambda a: pl.BlockSpec(a.shape, lambda bi: (0,) * a.ndim)
    args = (conv_w, conv_b, wa_bd, b_a, wx_bd, b_x, lam)
    return pl.pallas_call(
        _lru_kernel,
        grid=(b,),
        in_specs=[pl.BlockSpec((2, 1, s, w), lambda bi: (0, bi, 0, 0))] + [full(a) for a in args],
        out_specs=pl.BlockSpec((1, s, w), lambda bi: (bi, 0, 0)),
        out_shape=jax.ShapeDtypeStruct((b, s, w), BF16),
        scratch_shapes=[pltpu.VMEM((s, w), F32)],
        compiler_params=_cparams("parallel"),
        name="rg_lru",
    )(xg, *args)


def _memkv_kernel(mem_ref, g_ref, wk_ref, wv_ref, k_ref, v_ref):
    mn = _rms(mem_ref[0], g_ref[...]).astype(BF16)
    k_ref[0] = _dot(mn, wk_ref[...]).astype(BF16)
    v_ref[0] = _dot(mn, wv_ref[...]).astype(BF16)


def _memkv(mem, g, w_ck, w_cv):
    b, m, d = mem.shape
    out = jax.ShapeDtypeStruct((b, m, MIX_W), BF16)
    return pl.pallas_call(
        _memkv_kernel,
        grid=(b,),
        in_specs=[
            pl.BlockSpec((1, m, d), lambda bi: (bi, 0, 0)),
            pl.BlockSpec((1, d), lambda bi: (0, 0)),
            pl.BlockSpec((d, MIX_W), lambda bi: (0, 0)),
            pl.BlockSpec((d, MIX_W), lambda bi: (0, 0)),
        ],
        out_specs=[pl.BlockSpec((1, m, MIX_W), lambda bi: (bi, 0, 0))] * 2,
        out_shape=[out, out],
        compiler_params=_cparams("parallel"),
        name="mem_kv",
    )(mem, g, w_ck, w_cv)


def _mix_cross_kernel(x_ref, sb_ref, fox_ref, lru_ref, o1_ref, l1_ref, o4_ref, l4_ref,
                      o16_ref, l16_ref, wo_ref, g_ref, wq_ref, km_ref, vm_ref, wco_ref, o_ref):
    t = MIX_SUB
    subs = [slice(u * t, (u + 1) * t) for u in range(x_ref.shape[1] // t)]
    xs = []
    for r in subs:
        dil = _dil_mixture([ref[0, r] for ref in (o1_ref, o4_ref, o16_ref)],
                           [ref[0, r] for ref in (l1_ref, l4_ref, l16_ref)]).astype(BF16)
        x = x_ref[0, r]
        for j, mixed in enumerate((sb_ref[0, r], fox_ref[0, r], dil, lru_ref[0, r])):
            x = x + _dot(mixed, wo_ref[j])
        xs.append(x)
    qs = [_stack_heads(_dot(_rms(x, g_ref[...]).astype(BF16), wq_ref[...])) for x in xs]
    scores = [_dot_nt(q, km_ref[0]) for q in qs]
    outs = []
    for sc in scores:
        p = jnp.exp(sc - jnp.max(sc, axis=-1, keepdims=True))
        p = p / jnp.sum(p, axis=-1, keepdims=True)
        outs.append(_unstack_heads(_dot(p.astype(BF16), vm_ref[0]), t))
    for r, x, o in zip(subs, xs, outs):
        o_ref[0, r] = x + _dot(o.astype(BF16), wco_ref[...])


MIX_SUB = 256


def _mix_cross(x, o_sb, o_fox, o_lru, dil_branches, w_out, g, w_cq, k_mem, v_mem, w_co, ts=512):
    b, s, d = x.shape
    m = k_mem.shape[1]
    mix_spec = pl.BlockSpec((1, ts, MIX_W), lambda bi, i: (bi, i, 0))
    const = lambda a: pl.BlockSpec(a.shape, lambda bi, i: (0,) * a.ndim)
    mem_spec = pl.BlockSpec((1, m, MIX_W), lambda bi, i: (bi, 0, 0))
    dil_flat = [a for pair in dil_branches for a in pair]
    return pl.pallas_call(
        _mix_cross_kernel,
        grid=(b, s // ts),
        in_specs=[pl.BlockSpec((1, ts, d), lambda bi, i: (bi, i, 0))] + [mix_spec] * (3 + len(dil_flat))
                 + [const(w_out), const(g), const(w_cq), mem_spec, mem_spec, const(w_co)],
        out_specs=pl.BlockSpec((1, ts, d), lambda bi, i: (bi, i, 0)),
        out_shape=jax.ShapeDtypeStruct((b, s, d), F32),
        compiler_params=_cparams("parallel", "parallel"),
        name="mix_cross",
    )(x, o_sb, o_fox, o_lru, *dil_flat, w_out, g, w_cq, k_mem, v_mem, w_co)


HALO = 8


def _ffn_kernel(x_ref, halo_ref, g_ref, wup_ref, cw_ref, cb_ref, wdn_ref, gf_ref, o_ref,
                ext_ref, act_ref, acc_ref, *, final_norm):
    i = pl.program_id(1)
    x = x_ref[0]
    t = x.shape[0]
    g = g_ref[...]
    halo = jnp.where(i > 0, _rms(halo_ref[0], g), 0.0)
    hn = jnp.concatenate([halo, _rms(x, g)], axis=0).astype(BF16)
    acc_ref[...] = jnp.zeros_like(acc_ref)

    def project(c):
        for half in range(2):
            col = (half * N_FF_CHUNKS + c) * FF_CHUNK
            ext_ref[2 * (c % 2) + half] = _dot(hn, wup_ref[:, col:col + FF_CHUNK])

    def conv(c, slot):
        ext = ext_ref.at[slot]
        w = cw_ref[c]
        return (cb_ref[c] + w[2:3] * ext[pl.ds(HALO, t), :]
                + w[1:2] * ext[pl.ds(HALO - 1, t), :]
                + w[0:1] * ext[pl.ds(HALO - 2, t), :])

    def down(c):
        acc_ref[...] += _dot(act_ref[c % 2], wdn_ref[c])

    project(0)
    for c in range(N_FF_CHUNKS):
        if c + 1 < N_FF_CHUNKS:
            project(c + 1)
        if c >= 1:
            down(c - 1)
        up = conv(c, 2 * (c % 2))
        gate = conv(N_FF_CHUNKS + c, 2 * (c % 2) + 1)
        act_ref[c % 2] = (_gelu(gate) * up).astype(BF16)
    down(N_FF_CHUNKS - 1)
    out = x + acc_ref[...]
    if final_norm:
        out = _rms(out, gf_ref[...])
    o_ref[0] = out


def _ffn(x, g, w_up, conv_w, conv_b, w_down, g_final, final_norm, ts=256):
    b, s, d = x.shape
    per_tile = ts // HALO
    const = lambda a: pl.BlockSpec(a.shape, lambda bi, i: (0,) * a.ndim)
    return pl.pallas_call(
        functools.partial(_ffn_kernel, final_norm=final_norm),
        grid=(b, s // ts),
        in_specs=[
            pl.BlockSpec((1, ts, d), lambda bi, i: (bi, i, 0)),
            pl.BlockSpec((1, HALO, d), lambda bi, i: (bi, jnp.maximum(i * per_tile - 1, 0), 0)),
            const(g), const(w_up), const(conv_w), const(conv_b), const(w_down), const(g_final),
        ],
        out_specs=pl.BlockSpec((1, ts, d), lambda bi, i: (bi, i, 0)),
        out_shape=jax.ShapeDtypeStruct((b, s, d), F32),
        scratch_shapes=[
            pltpu.VMEM((4, HALO + ts, FF_CHUNK), F32),
            pltpu.VMEM((2, ts, FF_CHUNK), BF16),
            pltpu.VMEM((ts, d), F32),
        ],
        compiler_params=_cparams("parallel", "arbitrary"),
        name="conv_geglu",
    )(x, x, g, w_up, conv_w, conv_b, w_down, g_final)


def _block_diag(w):
    g, n, _ = w.shape
    eye = jnp.eye(g, dtype=w.dtype)
    return (eye[:, None, :, None] * w[:, :, None, :]).reshape(g * n, g * n)


def kernel(x, mem, norm_mix_g, w_in, b_forget, lru_conv_w, lru_conv_b, lru_w_a, lru_b_a, lru_w_x, lru_b_x, lru_lambda, w_out, norm_cross_g, norm_mem_g, w_cq, w_ck, w_cv, w_co, norm_ffn_g, w_up, ffn_conv_w, ffn_conv_b, w_down, rel_bias, final_norm_g):
    depth = w_in.shape[0]
    qkv_w = 3 * MIX_W
    q_scale = HEAD_DIM ** -0.5
    col_scale = jnp.ones((3, 3, MIX_W), F32).at[:, 0, :].set(q_scale).reshape(-1)
    col_scale = jnp.concatenate([col_scale, jnp.ones((2 * MIX_W,), F32)])
    tables = _bias_tables(rel_bias)
    row = lambda v: v.reshape(1, -1)

    for l in range(depth):
        w = w_in[l]
        n_f = N_HEADS
        w_main = jnp.concatenate([w[:, :2 * qkv_w], w[:, 2 * qkv_w + n_f:]], axis=1)
        w_main = (w_main * col_scale).astype(BF16)
        w_f = jnp.zeros((8, D_MODEL), F32).at[:n_f].set(w[:, 2 * qkv_w:2 * qkv_w + n_f].T).astype(BF16)
        b_f = jnp.zeros((8, 1), F32).at[:n_f, 0].set(b_forget[l])

        sb_qkv, fox_qkv, *dil_qkvs, lru_xg, f_t = _inproj(x, row(norm_mix_g[l]), w_main, w_f)
        o_sb = _sb_attention(sb_qkv)
        o_fox = _fox_attention(fox_qkv, _decay(f_t, b_f))
        dil_branches = [_dil_branch(qkv, tables, br) for br, qkv in enumerate(dil_qkvs)]
        o_lru = _lru(lru_xg, lru_conv_w[l], row(lru_conv_b[l]),
                     _block_diag(lru_w_a[l]).astype(BF16), row(lru_b_a[l]),
                     _block_diag(lru_w_x[l]).astype(BF16), row(lru_b_x[l]), row(lru_lambda[l]))

        k_mem, v_mem = _memkv(mem, row(norm_mem_g[l]), w_ck[l].astype(BF16), w_cv[l].astype(BF16))
        x = _mix_cross(x, o_sb, o_fox, o_lru, dil_branches,
                       w_out[l].reshape(4, MIX_W, D_MODEL).astype(BF16), row(norm_cross_g[l]),
                       (w_cq[l] * q_scale).astype(BF16), k_mem, v_mem, w_co[l].astype(BF16))

        wu = w_up[l].astype(BF16)
        cw = ffn_conv_w[l].reshape(-1, 2 * N_FF_CHUNKS, FF_CHUNK).transpose(1, 0, 2)
        cb = ffn_conv_b[l].reshape(2 * N_FF_CHUNKS, 1, FF_CHUNK)
        wd = w_down[l].reshape(N_FF_CHUNKS, FF_CHUNK, D_MODEL).astype(BF16)
        x = _ffn(x, row(norm_ffn_g[l]), wu, cw, cb, wd, row(final_norm_g), l == depth - 1)
    return x
```

```python
import functools
import math

import jax
import jax.numpy as jnp
from jax import lax
from jax.experimental import pallas as pl
from jax.experimental.pallas import tpu as pltpu

F32 = jnp.float32
BF16 = jnp.bfloat16

LANES = 128
SUBLANES = 8
D_MODEL = 1024
HEAD_DIM = 64
N_HEADS = 4
MIX_W = N_HEADS * HEAD_DIM
D_FF = 2816
FF_CHUNK = 256
N_FF_CHUNKS = D_FF // FF_CHUNK
N_MEM = 256
NUM_BUCKETS = 32
MAX_DISTANCE = 2048
DILATIONS = (1, 4, 16)
DIL_STEPS = 128
LRU_C = 8.0
EPS = 1e-6
NEG_BIG = -1e30

TQ = 128
TQC = 256
TK = 256
VMEM_LIMIT = 56 * 1024 * 1024

NT_DIMS = (((1,), (1,)), ((), ()))


def _cparams(*sem):
    return pltpu.CompilerParams(dimension_semantics=sem, vmem_limit_bytes=VMEM_LIMIT)


def _rms(x, g):
    return x * lax.rsqrt(jnp.mean(x * x, axis=-1, keepdims=True) + EPS) * g


def _softplus(x):
    return jnp.maximum(x, 0.0) + jnp.log(1.0 + jnp.exp(-jnp.abs(x)))


def _gelu(x):
    return 0.5 * x * (1.0 + lax.erf(x * math.sqrt(0.5)))


def _dot(a, b):
    return jnp.dot(a, b, preferred_element_type=F32)


def _dot_nt(a, b):
    return lax.dot_general(a, b, NT_DIMS, preferred_element_type=F32)


def _split2(x):
    hi = x.astype(BF16)
    lo = (x - hi.astype(F32)).astype(BF16)
    return hi, lo


def _head_of_lane(rows):
    return lax.broadcasted_iota(jnp.int32, (rows, MIX_W), 1) // HEAD_DIM


def _stack_heads(q):
    t = q.shape[0]
    head = _head_of_lane(t)
    qf = q.astype(F32)
    return jnp.concatenate([jnp.where(head == h, qf, 0.0) for h in range(N_HEADS)], axis=0).astype(BF16)


def _unstack_heads(o, t):
    head = _head_of_lane(t)
    out = jnp.where(head == 0, o[0:t], 0.0)
    for h in range(1, N_HEADS):
        out = jnp.where(head == h, o[h * t:(h + 1) * t], out)
    return out


def _unstack_cols(c, t):
    head = _head_of_lane(t)
    out = jnp.broadcast_to(c[0:t], (t, MIX_W))
    for h in range(1, N_HEADS):
        out = jnp.where(head == h, jnp.broadcast_to(c[h * t:(h + 1) * t], (t, MIX_W)), out)
    return out


def _inproj_kernel(x_ref, g_ref, w_ref, wf_ref, sb_ref, fox_ref, d1_ref, d4_ref, d16_ref,
                   lru_ref, f_ref, half_ref):
    hn = _rms(x_ref[0], g_ref[...]).astype(BF16)
    ts = hn.shape[0]
    for m, ref in enumerate((sb_ref, fox_ref)):
        for j in range(3):
            c = (3 * m + j) * MIX_W
            ref[j, 0] = _dot(hn, w_ref[:, c:c + MIX_W]).astype(BF16)
    for j in range(3):
        c = (6 + j) * MIX_W
        res = _dot(hn, w_ref[:, c:c + MIX_W])
        d1_ref[j, 0, 0] = res.astype(BF16)
        for h in range(2):
            half_ref[h] = res[:, h * LANES:(h + 1) * LANES]
        for ref, dil in ((d4_ref, DILATIONS[1]), (d16_ref, DILATIONS[2])):
            for r in range(dil):
                for h in range(2):
                    ref[j, 0, r, :, h * LANES:(h + 1) * LANES] = (
                        half_ref[h, pl.ds(r, ts // dil, stride=dil), :].astype(BF16))
    for j in range(2):
        c = (9 + j) * MIX_W
        lru_ref[j, 0] = _dot(hn, w_ref[:, c:c + MIX_W])
    f_ref[0] = _dot_nt(wf_ref[...], hn)


def _inproj(x, g, w_main, w_f, ts=512):
    b, s, d = x.shape
    qkv_shape = jax.ShapeDtypeStruct((3, b, s, MIX_W), BF16)
    qkv_spec = pl.BlockSpec((3, 1, ts, MIX_W), lambda bi, i: (0, bi, i, 0))
    cls_shapes = [jax.ShapeDtypeStruct((3, b, dil, s // dil, MIX_W), BF16) for dil in DILATIONS]
    cls_specs = [pl.BlockSpec((3, 1, dil, ts // dil, MIX_W), lambda bi, i: (0, bi, 0, i, 0))
                 for dil in DILATIONS]
    return pl.pallas_call(
        _inproj_kernel,
        grid=(b, s // ts),
        in_specs=[
            pl.BlockSpec((1, ts, d), lambda bi, i: (bi, i, 0)),
            pl.BlockSpec((1, d), lambda bi, i: (0, 0)),
            pl.BlockSpec(w_main.shape, lambda bi, i: (0, 0)),
            pl.BlockSpec(w_f.shape, lambda bi, i: (0, 0)),
        ],
        out_specs=[
            qkv_spec, qkv_spec, *cls_specs,
            pl.BlockSpec((2, 1, ts, MIX_W), lambda bi, i: (0, bi, i, 0)),
            pl.BlockSpec((1, 8, ts), lambda bi, i: (bi, 0, i)),
        ],
        out_shape=[
            qkv_shape, qkv_shape, *cls_shapes,
            jax.ShapeDtypeStruct((2, b, s, MIX_W), F32),
            jax.ShapeDtypeStruct((b, 8, s), F32),
        ],
        scratch_shapes=[pltpu.VMEM((2, ts, LANES), F32)],
        compiler_params=_cparams("parallel", "parallel"),
        name="inproj",
    )(x, g, w_main, w_f)


def _decay_kernel(f_ref, bf_ref, tri_ref, o_ref):
    s = f_ref.shape[2]
    tri = tri_ref[...]
    carry = jnp.zeros((8, 1), F32)
    for c in range(s // 128):
        sl = slice(c * 128, (c + 1) * 128)
        log_f = -_softplus(-(f_ref[0, :, sl] + bf_ref[...]))
        p1 = log_f.astype(BF16)
        r1 = log_f - p1.astype(F32)
        p2 = r1.astype(BF16)
        p3 = (r1 - p2.astype(F32)).astype(BF16)
        inc = _dot(p1, tri) + _dot(p2, tri) + _dot(p3, tri)
        o_ref[0, :, sl] = inc + carry
        carry = carry + jnp.sum(log_f, axis=-1, keepdims=True)


def _decay(f_t, b_f):
    b, _, s = f_t.shape
    tri = (lax.broadcasted_iota(jnp.int32, (128, 128), 0)
           <= lax.broadcasted_iota(jnp.int32, (128, 128), 1)).astype(BF16)
    return pl.pallas_call(
        _decay_kernel,
        grid=(b,),
        in_specs=[
            pl.BlockSpec((1, 8, s), lambda bi: (bi, 0, 0)),
            pl.BlockSpec((8, 1), lambda bi: (0, 0)),
            pl.BlockSpec((128, 128), lambda bi: (0, 0)),
        ],
        out_specs=pl.BlockSpec((1, 8, s), lambda bi: (bi, 0, 0)),
        out_shape=jax.ShapeDtypeStruct((b, 8, s), F32),
        compiler_params=_cparams("parallel"),
        name="fox_decay",
    )(f_t, b_f, tri)


def _block_positions(i, ks):
    row = lax.broadcasted_iota(jnp.int32, (N_HEADS * TQC, TK), 0) % TQC + i * TQC
    col = lax.broadcasted_iota(jnp.int32, (N_HEADS * TQC, TK), 1) + ks
    return row, col


SB_SATURATED = 110.0


def _sb_kernel(q_ref, k_ref, v_ref, uo_ref, o_ref, acc_ref, car_ref):
    i = pl.program_id(1)
    qs = _stack_heads(q_ref[0, 0])
    k_diag = (i * TQC) // TK

    half_rows = N_HEADS * TQC // 2
    halves = [slice(0, half_rows), slice(half_rows, 2 * half_rows)]

    def block(kb, diagonal):
        ks = pl.multiple_of(kb * TK, TK)
        k_blk = k_ref[0, 0, pl.ds(ks, TK), :]
        v_blk = v_ref[0, 0, pl.ds(ks, TK), :]
        if diagonal:
            row, col = _block_positions(i, ks)
            strict = (col < row)[:half_rows]
        zs = [_dot_nt(qs[r], k_blk) for r in halves]
        drops, sums = [], []
        for z in zs:
            drop = _softplus(z)
            cum_in = jnp.where(strict, drop, 0.0) if diagonal else drop
            sums.append(_dot(cum_in.astype(BF16), uo_ref[...]))
            drops.append(drop)
        for r, z, drop, sm in zip(halves, zs, drops, sums):
            if diagonal:
                att = jnp.where(strict, jnp.exp(z - drop - sm[:, :TK]), 0.0)
                car_ref[r] = sm[:, TK:]
                acc_ref[r] = _dot(att.astype(BF16), v_blk)
            else:
                car = car_ref[r]
                att = jnp.exp(z - drop - sm[:, :TK] - car)
                car_ref[r] = car + sm[:, TK:]
                acc_ref[r] += _dot(att.astype(BF16), v_blk)
        return jnp.min(car_ref[:, :LANES]) < SB_SATURATED

    def cond(state):
        kb, live = state
        return jnp.logical_and(kb >= 0, live)

    def body(state):
        kb, _ = state
        return kb - 1, block(kb, False)

    lax.while_loop(cond, body, (k_diag - 1, block(k_diag, True)))
    o_ref[0] = _unstack_heads(acc_ref[...], TQC).astype(o_ref.dtype)


def _sb_attention(qkv):
    _, b, s, _ = qkv.shape
    later = (lax.broadcasted_iota(jnp.int32, (TK, TK), 0)
             > lax.broadcasted_iota(jnp.int32, (TK, TK), 1))
    u = jnp.concatenate([later, jnp.ones((TK, TK), bool)], axis=1).astype(BF16)
    return pl.pallas_call(
        _sb_kernel,
        grid=(b, s // TQC),
        in_specs=[
            pl.BlockSpec((1, 1, TQC, MIX_W), lambda bi, i: (0, bi, i, 0)),
            pl.BlockSpec((1, 1, s, MIX_W), lambda bi, i: (1, bi, 0, 0)),
            pl.BlockSpec((1, 1, s, MIX_W), lambda bi, i: (2, bi, 0, 0)),
            pl.BlockSpec((TK, 2 * TK), lambda bi, i: (0, 0)),
        ],
        out_specs=pl.BlockSpec((1, TQC, MIX_W), lambda bi, i: (bi, i, 0)),
        out_shape=jax.ShapeDtypeStruct((b, s, MIX_W), BF16),
        scratch_shapes=[
            pltpu.VMEM((N_HEADS * TQC, MIX_W), F32),
            pltpu.VMEM((N_HEADS * TQC, TK), F32),
        ],
        compiler_params=_cparams("parallel", "arbitrary"),
        name="sb_attn",
    )(qkv, qkv, qkv, u)


def _fox_kernel(q_ref, k_ref, v_ref, cum_ref, o_ref, za_ref, zb_ref, ma_ref, mb_ref, l_ref, acc_ref):
    i = pl.program_id(1)
    n_q = pl.num_programs(1) - 1

    def front_block(qs, kb, diagonal, z_ref, mx_ref):
        ks = pl.multiple_of(kb * TK, TK)
        cum = cum_ref[0, :, pl.ds(ks, TK)]
        decay = jnp.concatenate(
            [jnp.broadcast_to(cum[h:h + 1], (TQC, TK)) for h in range(N_HEADS)], axis=0)
        z = _dot_nt(qs, k_ref[0, 0, pl.ds(ks, TK), :]) - decay
        if diagonal:
            row, col = _block_positions(i, ks)
            z = jnp.where(col <= row, z, NEG_BIG)
        z_ref[kb] = z
        zmax = jnp.maximum(z[:, :LANES], z[:, LANES:])
        mx_ref[...] = zmax if diagonal else jnp.maximum(mx_ref[...], zmax)

    def back_block(kb, z_ref, mx_ref):
        ks = pl.multiple_of(kb * TK, TK)
        m = mx_ref[...]
        p = jnp.exp(z_ref[kb] - jnp.concatenate([m, m], axis=1))
        l_ref[...] += p[:, :LANES] + p[:, LANES:]
        acc_ref[...] += _dot(p.astype(BF16), v_ref[0, 0, pl.ds(ks, TK), :])

    def step(z_cur, m_cur, z_prev, m_prev):
        @pl.when(i < n_q)
        def _():
            qs = _stack_heads(q_ref[0, 0])
            front_block(qs, i, True, z_cur, m_cur)

            def both(kb):
                front_block(qs, kb, False, z_cur, m_cur)
                back_block(kb, z_prev, m_prev)

            def pair(j, carry):
                both(2 * j)
                both(2 * j + 1)
                return carry

            lax.fori_loop(0, i // 2, pair, 0)
            pl.when(i % 2 == 1)(lambda: both(i - 1))
            m = jnp.max(m_cur[...], axis=-1, keepdims=True)
            m_cur[...] = jnp.broadcast_to(m, m_cur.shape)

        @pl.when(i == n_q)
        def _():
            def back_only(kb, carry):
                back_block(kb, z_prev, m_prev)
                return carry

            lax.fori_loop(0, i, back_only, 0)

    l_ref[...] = jnp.zeros_like(l_ref)
    acc_ref[...] = jnp.zeros_like(acc_ref)
    pl.when(i % 2 == 0)(functools.partial(step, za_ref, ma_ref, zb_ref, mb_ref))
    pl.when(i % 2 == 1)(functools.partial(step, zb_ref, mb_ref, za_ref, ma_ref))

    @pl.when(i > 0)
    def _():
        l = jnp.sum(l_ref[...], axis=-1, keepdims=True)
        o_ref[0] = _unstack_heads(acc_ref[...] / l, TQC).astype(o_ref.dtype)


def _fox_attention(qkv, cum):
    _, b, s, _ = qkv.shape
    assert TQC == TK
    n_q = s // TQC
    rows = N_HEADS * TQC
    return pl.pallas_call(
        _fox_kernel,
        grid=(b, n_q + 1),
        in_specs=[
            pl.BlockSpec((1, 1, TQC, MIX_W), lambda bi, i: (0, bi, jnp.minimum(i, n_q - 1), 0)),
            pl.BlockSpec((1, 1, s, MIX_W), lambda bi, i: (1, bi, 0, 0)),
            pl.BlockSpec((1, 1, s, MIX_W), lambda bi, i: (2, bi, 0, 0)),
            pl.BlockSpec((1, 8, s), lambda bi, i: (bi, 0, 0)),
        ],
        out_specs=pl.BlockSpec((1, TQC, MIX_W), lambda bi, i: (bi, jnp.maximum(i - 1, 0), 0)),
        out_shape=jax.ShapeDtypeStruct((b, s, MIX_W), BF16),
        scratch_shapes=[
            pltpu.VMEM((s // TK, rows, TK), F32),
            pltpu.VMEM((s // TK, rows, TK), F32),
            pltpu.VMEM((rows, LANES), F32),
            pltpu.VMEM((rows, LANES), F32),
            pltpu.VMEM((rows, LANES), F32),
            pltpu.VMEM((rows, MIX_W), F32),
        ],
        compiler_params=_cparams("parallel", "arbitrary"),
        name="fox_attn",
    )(qkv, qkv, qkv, cum)


def _t5_bucket(dist):
    n = jnp.maximum(dist, 0)
    max_exact = NUM_BUCKETS // 2
    nf = jnp.maximum(n, 1).astype(F32)
    large = max_exact + (jnp.log(nf / max_exact) / math.log(MAX_DISTANCE / max_exact)
                         * (NUM_BUCKETS - max_exact)).astype(jnp.int32)
    large = jnp.minimum(large, NUM_BUCKETS - 1)
    return jnp.where(n < max_exact, n, large)


def _bias_kernel(idx_ref, rb_ref, o_ref):
    for d in range(len(DILATIONS)):
        for variant in range(2):
            idx = idx_ref[d, variant]
            for h in range(N_HEADS):
                acc = jnp.full(idx.shape, NEG_BIG, F32)
                for bucket in range(NUM_BUCKETS):
                    acc = jnp.where(idx == bucket, rb_ref[bucket, h], acc)
                o_ref[d, variant, h] = acc


def _bias_tables(rel_bias):
    t = jnp.arange(TQ)[:, None]
    c = jnp.arange(2 * TQ)[None, :]
    delta = TQ + t - c
    valid = (delta >= 0) & (delta <= DIL_STEPS)
    idx = jnp.stack([jnp.where(valid, _t5_bucket(delta * dil), -1) for dil in DILATIONS])
    idx_first = jnp.concatenate([idx[:, :, TQ:], jnp.full_like(idx[:, :, TQ:], -1)], axis=-1)
    idx = jnp.stack([idx, idx_first], axis=1)
    out = pl.pallas_call(
        _bias_kernel,
        in_specs=[
            pl.BlockSpec(memory_space=pltpu.VMEM),
            pl.BlockSpec(memory_space=pltpu.SMEM),
        ],
        out_specs=pl.BlockSpec(memory_space=pltpu.VMEM),
        out_shape=jax.ShapeDtypeStruct((len(DILATIONS), 2, N_HEADS, TQ, 2 * TQ), F32),
        name="dil_bias",
    )(idx.astype(jnp.int32), rel_bias)
    return out.reshape(len(DILATIONS), 2, N_HEADS * TQ, 2 * TQ)


DIL_GROUP = 8


def _dil_kernel(q_ref, k_ref, v_ref, t_ref, o_ref, lse_ref, *, by_class, group, n_groups):
    g = pl.program_id(2)

    def attend(jj, scores, v_win):
        m = jnp.max(scores, axis=-1, keepdims=True)
        p = jnp.exp(scores - m)
        l = jnp.sum(p, axis=-1, keepdims=True)
        o = _unstack_heads(_dot(p.astype(BF16), v_win) / l, TQ).astype(o_ref.dtype)
        lse = _unstack_cols(m + jnp.log(l), TQ)
        if by_class:
            o_ref[0, jj] = o
            lse_ref[0, jj] = lse
        else:
            o_ref[0, 0, jj * TQ:(jj + 1) * TQ] = o
            lse_ref[0, 0, jj * TQ:(jj + 1) * TQ] = lse

    pending = []
    for jj in range(group):
        if by_class:
            scores = (_dot_nt(_stack_heads(q_ref[0, 0, jj]), k_ref[0, 0, jj])
                      + t_ref[0, 0, :, TQ:2 * TQ])
            pending.append((scores, v_ref[0, 0, jj]))
            continue
        qs = _stack_heads(q_ref[0, 0, 0, jj * TQ:(jj + 1) * TQ])
        if n_groups == 1 or jj > 0:
            n = g * group + jj if n_groups > 1 else jj
            start = (n - 1) * TQ if jj > 0 else 0
            bias = t_ref[0, 0 if jj > 0 else 1]
        else:
            start = jnp.maximum(g * group - 1, 0) * TQ
            bias = t_ref[0, (g == 0).astype(jnp.int32)]
        if not isinstance(start, int):
            start = pl.multiple_of(start, TQ)
        window = pl.ds(start, 2 * TQ)
        pending.append((_dot_nt(qs, k_ref[0, 0, 0, window, :]) + bias, v_ref[0, 0, 0, window, :]))
    for jj, (scores, v_win) in enumerate(pending):
        attend(jj, scores, v_win)


def _dil_branch(qkv, tables, branch):
    _, b, dil, cls_len, _ = qkv.shape
    n_blocks = cls_len // TQ
    by_class = n_blocks == 1
    group = min(DIL_GROUP, dil if by_class else n_blocks)
    if by_class:
        grid = (b, dil // group, 1)
        qkv_specs = [pl.BlockSpec((1, 1, group, TQ, MIX_W), functools.partial(
            lambda bi, c, g, j: (j, bi, c, 0, 0), j=j)) for j in range(3)]
        out_spec = pl.BlockSpec((1, group, TQ, MIX_W), lambda bi, c, g: (bi, c, 0, 0))
    else:
        grid = (b, dil, n_blocks // group)
        qkv_specs = [pl.BlockSpec((1, 1, 1, group * TQ, MIX_W), lambda bi, r, g: (0, bi, r, g, 0))]
        qkv_specs += [pl.BlockSpec((1, 1, 1, cls_len, MIX_W), functools.partial(
            lambda bi, r, g, j: (j, bi, r, 0, 0), j=j)) for j in (1, 2)]
        out_spec = pl.BlockSpec((1, 1, group * TQ, MIX_W), lambda bi, r, g: (bi, r, g, 0))
    o, lse = pl.pallas_call(
        functools.partial(_dil_kernel, by_class=by_class, group=group, n_groups=grid[2]),
        grid=grid,
        in_specs=qkv_specs + [
            pl.BlockSpec((1, 2, N_HEADS * TQ, 2 * TQ), lambda bi, r, g: (branch, 0, 0, 0))],
        out_specs=[out_spec, out_spec],
        out_shape=[jax.ShapeDtypeStruct((b, dil, cls_len, MIX_W), BF16),
                   jax.ShapeDtypeStruct((b, dil, cls_len, MIX_W), F32)],
        compiler_params=_cparams("parallel", "parallel", "arbitrary"),
        name=f"dil_attn_{dil}",
    )(qkv, qkv, qkv, tables)
    to_seq = lambda a: a.transpose(0, 2, 1, 3).reshape(b, dil * cls_len, MIX_W)
    return to_seq(o), to_seq(lse)


def _dil_mixture(outs, lses):
    m = functools.reduce(jnp.maximum, lses)
    es = [jnp.exp(a - m) for a in lses]
    tot = functools.reduce(jnp.add, es)
    return functools.reduce(jnp.add, [(e / tot) * o for e, o in zip(es, outs)])


def _shift_rows(x, k, fill):
    row = lax.broadcasted_iota(jnp.int32, x.shape, 0)
    return jnp.where(row >= k, pltpu.roll(x, k, 0), fill)


def _lru_kernel(xg_ref, cw_ref, cb_ref, wa_ref, ba_ref, wx_ref, bx_ref, lam_ref, o_ref, h_ref):
    x = xg_ref[0, 0]
    gate = xg_ref[1, 0]
    s = x.shape[0]
    kw = cw_ref.shape[0]
    xc = cb_ref[...] + cw_ref[kw - 1:kw, :] * x
    for j in range(kw - 1):
        xc = xc + cw_ref[j:j + 1, :] * _shift_rows(x, kw - 1 - j, 0.0)
    xcb = xc.astype(BF16)
    r = jax.nn.sigmoid(_dot(xcb, wa_ref[...]) + ba_ref[...])
    i_gate = jax.nn.sigmoid(_dot(xcb, wx_ref[...]) + bx_ref[...])
    a = jnp.exp(-LRU_C * r * _softplus(-lam_ref[...]))
    h = jnp.sqrt(1.0 - a * a) * (i_gate * xc)
    in_group = lax.broadcasted_iota(jnp.int32, x.shape, 0) % SUBLANES
    k = 1
    while k < SUBLANES:
        prev = in_group >= k
        h = a * jnp.where(prev, pltpu.roll(h, k, 0), 0.0) + h
        a = a * jnp.where(prev, pltpu.roll(a, k, 0), 1.0)
        k *= 2
    carry = jnp.zeros((1, x.shape[1]), F32)
    for c in range(s // SUBLANES):
        rows = slice(c * SUBLANES, (c + 1) * SUBLANES)
        hc = h[rows] + a[rows] * carry
        h_ref[rows] = hc
        carry = hc[SUBLANES - 1:SUBLANES]
    o_ref[0] = (h_ref[...] * _gelu(gate)).astype(o_ref.dtype)


def _lru(xg, conv_w, conv_b, wa_bd, b_a, wx_bd, b_x, lam):
    _, b, s, w = xg.shape
    full = lambda a: pl.BlockSpec(a.shape, lambda bi: (0,) * a.ndim)
    args = (conv_w, conv_b, wa_bd, b_a, wx_bd, b_x, lam)
    return pl.pallas_call(
        _lru_kernel,
        grid=(b,),
        in_specs=[pl.BlockSpec((2, 1, s, w), lambda bi: (0, bi, 0, 0))] + [full(a) for a in args],
        out_specs=pl.BlockSpec((1, s, w), lambda bi: (bi, 0, 0)),
        out_shape=jax.ShapeDtypeStruct((b, s, w), BF16),
        scratch_shapes=[pltpu.VMEM((s, w), F32)],
        compiler_params=_cparams("parallel"),
        name="rg_lru",
    )(xg, *args)


def _memkv_kernel(mem_ref, g_ref, wk_ref, wv_ref, k_ref, v_ref):
    mn = _rms(mem_ref[0], g_ref[...]).astype(BF16)
    k_ref[0] = _dot(mn, wk_ref[...]).astype(BF16)
    v_ref[0] = _dot(mn, wv_ref[...]).astype(BF16)


def _memkv(mem, g, w_ck, w_cv):
    b, m, d = mem.shape
    out = jax.ShapeDtypeStruct((b, m, MIX_W), BF16)
    return pl.pallas_call(
        _memkv_kernel,
        grid=(b,),
        in_specs=[
            pl.BlockSpec((1, m, d), lambda bi: (bi, 0, 0)),
            pl.BlockSpec((1, d), lambda bi: (0, 0)),
            pl.BlockSpec((d, MIX_W), lambda bi: (0, 0)),
            pl.BlockSpec((d, MIX_W), lambda bi: (0, 0)),
        ],
        out_specs=[pl.BlockSpec((1, m, MIX_W), lambda bi: (bi, 0, 0))] * 2,
        out_shape=[out, out],
        compiler_params=_cparams("parallel"),
        name="mem_kv",
    )(mem, g, w_ck, w_cv)


def _mix_cross_kernel(x_ref, sb_ref, fox_ref, lru_ref, o1_ref, l1_ref, o4_ref, l4_ref,
                      o16_ref, l16_ref, wo_ref, g_ref, wq_ref, km_ref, vm_ref, wco_ref, o_ref):
    t = MIX_SUB
    subs = [slice(u * t, (u + 1) * t) for u in range(x_ref.shape[1] // t)]
    xs = []
    for r in subs:
        dil = _dil_mixture([ref[0, r] for ref in (o1_ref, o4_ref, o16_ref)],
                           [ref[0, r] for ref in (l1_ref, l4_ref, l16_ref)]).astype(BF16)
        x = x_ref[0, r]
        for j, mixed in enumerate((sb_ref[0, r], fox_ref[0, r], dil, lru_ref[0, r])):
            x = x + _dot(mixed, wo_ref[j])
        xs.append(x)
    qs = [_stack_heads(_dot(_rms(x, g_ref[...]).astype(BF16), wq_ref[...])) for x in xs]
    scores = [_dot_nt(q, km_ref[0]) for q in qs]
    outs = []
    for sc in scores:
        p = jnp.exp(sc - jnp.max(sc, axis=-1, keepdims=True))
        p = p / jnp.sum(p, axis=-1, keepdims=True)
        outs.append(_unstack_heads(_dot(p.astype(BF16), vm_ref[0]), t))
    for r, x, o in zip(subs, xs, outs):
        o_ref[0, r] = x + _dot(o.astype(BF16), wco_ref[...])


MIX_SUB = 256


def _mix_cross(x, o_sb, o_fox, o_lru, dil_branches, w_out, g, w_cq, k_mem, v_mem, w_co, ts=512):
    b, s, d = x.shape
    m = k_mem.shape[1]
    mix_spec = pl.BlockSpec((1, ts, MIX_W), lambda bi, i: (bi, i, 0))
    const = lambda a: pl.BlockSpec(a.shape, lambda bi, i: (0,) * a.ndim)
    mem_spec = pl.BlockSpec((1, m, MIX_W), lambda bi, i: (bi, 0, 0))
    dil_flat = [a for pair in dil_branches for a in pair]
    return pl.pallas_call(
        _mix_cross_kernel,
        grid=(b, s // ts),
        in_specs=[pl.BlockSpec((1, ts, d), lambda bi, i: (bi, i, 0))] + [mix_spec] * (3 + len(dil_flat))
                 + [const(w_out), const(g), const(w_cq), mem_spec, mem_spec, const(w_co)],
        out_specs=pl.BlockSpec((1, ts, d), lambda bi, i: (bi, i, 0)),
        out_shape=jax.ShapeDtypeStruct((b, s, d), F32),
        compiler_params=_cparams("parallel", "parallel"),
        name="mix_cross",
    )(x, o_sb, o_fox, o_lru, *dil_flat, w_out, g, w_cq, k_mem, v_mem, w_co)


HALO = 8
FFN_DOWN_LAG = 2
FFN_ACT_SLOTS = FFN_DOWN_LAG + 1


def _ffn_kernel(x_ref, g_ref, wup_ref, cw_ref, cb_ref, wdn_ref, gf_ref, o_ref,
                ext_ref, tail_ref, act_ref, acc_ref, *, final_norm):
    i = pl.program_id(1)
    x = x_ref[0]
    t = x.shape[0]
    hn = _rms(x, g_ref[...]).astype(BF16)
    acc_ref[...] = jnp.zeros_like(acc_ref)

    @pl.when(i == 0)
    def _():
        tail_ref[...] = jnp.zeros_like(tail_ref)

    def project(c):
        for half in range(2):
            blk = half * N_FF_CHUNKS + c
            slot = 2 * (c % 2) + half
            pre = _dot(hn, wup_ref[:, blk * FF_CHUNK:(blk + 1) * FF_CHUNK])
            ext_ref[slot, 0:HALO] = tail_ref[blk]
            ext_ref[slot, HALO:] = pre
            tail_ref[blk] = pre[t - HALO:]

    def conv(c, slot):
        ext = ext_ref.at[slot]
        w = cw_ref[c]
        return (cb_ref[c] + w[2:3] * ext[pl.ds(HALO, t), :]
                + w[1:2] * ext[pl.ds(HALO - 1, t), :]
                + w[0:1] * ext[pl.ds(HALO - 2, t), :])

    def down(c):
        acc_ref[...] += _dot(act_ref[c % FFN_ACT_SLOTS], wdn_ref[c])

    project(0)
    for c in range(N_FF_CHUNKS):
        if c + 1 < N_FF_CHUNKS:
            project(c + 1)
        if c >= FFN_DOWN_LAG:
            down(c - FFN_DOWN_LAG)
        up = conv(c, 2 * (c % 2))
        gate = conv(N_FF_CHUNKS + c, 2 * (c % 2) + 1)
        act_ref[c % FFN_ACT_SLOTS] = (_gelu(gate) * up).astype(BF16)
    for c in range(N_FF_CHUNKS - FFN_DOWN_LAG, N_FF_CHUNKS):
        down(c)
    out = x + acc_ref[...]
    if final_norm:
        out = _rms(out, gf_ref[...])
    o_ref[0] = out


def _ffn(x, g, w_up, conv_w, conv_b, w_down, g_final, final_norm, ts=256):
    b, s, d = x.shape
    const = lambda a: pl.BlockSpec(a.shape, lambda bi, i: (0,) * a.ndim)
    return pl.pallas_call(
        functools.partial(_ffn_kernel, final_norm=final_norm),
        grid=(b, s // ts),
        in_specs=[
            pl.BlockSpec((1, ts, d), lambda bi, i: (bi, i, 0)),
            const(g), const(w_up), const(conv_w), const(conv_b), const(w_down), const(g_final),
        ],
        out_specs=pl.BlockSpec((1, ts, d), lambda bi, i: (bi, i, 0)),
        out_shape=jax.ShapeDtypeStruct((b, s, d), F32),
        scratch_shapes=[
            pltpu.VMEM((4, HALO + ts, FF_CHUNK), F32),
            pltpu.VMEM((2 * N_FF_CHUNKS, HALO, FF_CHUNK), F32),
            pltpu.VMEM((FFN_ACT_SLOTS, ts, FF_CHUNK), BF16),
            pltpu.VMEM((ts, d), F32),
        ],
        compiler_params=_cparams("parallel", "arbitrary"),
        name="conv_geglu",
    )(x, g, w_up, conv_w, conv_b, w_down, g_final)


def _block_diag(w):
    g, n, _ = w.shape
    eye = jnp.eye(g, dtype=w.dtype)
    return (eye[:, None, :, None] * w[:, :, None, :]).reshape(g * n, g * n)


def kernel(x, mem, norm_mix_g, w_in, b_forget, lru_conv_w, lru_conv_b, lru_w_a, lru_b_a, lru_w_x, lru_b_x, lru_lambda, w_out, norm_cross_g, norm_mem_g, w_cq, w_ck, w_cv, w_co, norm_ffn_g, w_up, ffn_conv_w, ffn_conv_b, w_down, rel_bias, final_norm_g):
    depth = w_in.shape[0]
    qkv_w = 3 * MIX_W
    q_scale = HEAD_DIM ** -0.5
    col_scale = jnp.ones((3, 3, MIX_W), F32).at[:, 0, :].set(q_scale).reshape(-1)
    col_scale = jnp.concatenate([col_scale, jnp.ones((2 * MIX_W,), F32)])
    tables = _bias_tables(rel_bias)
    row = lambda v: v.reshape(1, -1)

    for l in range(depth):
        w = w_in[l]
        n_f = N_HEADS
        w_main = jnp.concatenate([w[:, :2 * qkv_w], w[:, 2 * qkv_w + n_f:]], axis=1)
        w_main = (w_main * col_scale).astype(BF16)
        w_f = jnp.zeros((8, D_MODEL), F32).at[:n_f].set(w[:, 2 * qkv_w:2 * qkv_w + n_f].T).astype(BF16)
        b_f = jnp.zeros((8, 1), F32).at[:n_f, 0].set(b_forget[l])

        sb_qkv, fox_qkv, *dil_qkvs, lru_xg, f_t = _inproj(x, row(norm_mix_g[l]), w_main, w_f)
        o_sb = _sb_attention(sb_qkv)
        o_fox = _fox_attention(fox_qkv, _decay(f_t, b_f))
        dil_branches = [_dil_branch(qkv, tables, br) for br, qkv in enumerate(dil_qkvs)]
        o_lru = _lru(lru_xg, lru_conv_w[l], row(lru_conv_b[l]),
                     _block_diag(lru_w_a[l]).astype(BF16), row(lru_b_a[l]),
                     _block_diag(lru_w_x[l]).astype(BF16), row(lru_b_x[l]), row(lru_lambda[l]))

        k_mem, v_mem = _memkv(mem, row(norm_mem_g[l]), w_ck[l].astype(BF16), w_cv[l].astype(BF16))
        x = _mix_cross(x, o_sb, o_fox, o_lru, dil_branches,
                       w_out[l].reshape(4, MIX_W, D_MODEL).astype(BF16), row(norm_cross_g[l]),
                       (w_cq[l] * q_scale).astype(BF16), k_mem, v_mem, w_co[l].astype(BF16))

        wu = w_up[l].astype(BF16)
        cw = ffn_conv_w[l].reshape(-1, 2 * N_FF_CHUNKS, FF_CHUNK).transpose(1, 0, 2)
        cb = ffn_conv_b[l].reshape(2 * N_FF_CHUNKS, 1, FF_CHUNK)
        wd = w_down[l].reshape(N_FF_CHUNKS, FF_CHUNK, D_MODEL).astype(BF16)
        x = _ffn(x, row(norm_ffn_g[l]), wu, cw, cb, wd, row(final_norm_g), l == depth - 1)
    return x
```

```python
import functools
import math
from typing import Any, NamedTuple

import jax
import jax.numpy as jnp
from jax import lax
from jax.experimental import pallas as pl
from jax.experimental.pallas import tpu as pltpu

F32 = jnp.float32
BF16 = jnp.bfloat16

LANES = 128
SUBLANES = 8
D_MODEL = 1024
HEAD_DIM = 64
N_HEADS = 4
MIX_W = N_HEADS * HEAD_DIM
D_FF = 2816
FF_CHUNK = 256
N_FF_CHUNKS = D_FF // FF_CHUNK
N_MEM = 256
NUM_BUCKETS = 32
MAX_DISTANCE = 2048
DILATIONS = (1, 4, 16)
DIL_STEPS = 128
LRU_C = 8.0
EPS = 1e-6
NEG_BIG = -1e30

TQ = 128
TQC = 256
TK = 256
VMEM_LIMIT = 56 * 1024 * 1024

NT_DIMS = (((1,), (1,)), ((), ()))


def _cparams(*sem):
    return pltpu.CompilerParams(dimension_semantics=sem, vmem_limit_bytes=VMEM_LIMIT)


def _rms(x, g):
    return x * lax.rsqrt(jnp.mean(x * x, axis=-1, keepdims=True) + EPS) * g


def _softplus(x):
    return jnp.maximum(x, 0.0) + jnp.log(1.0 + jnp.exp(-jnp.abs(x)))


def _gelu(x):
    return 0.5 * x * (1.0 + lax.erf(x * math.sqrt(0.5)))


def _dot(a, b):
    return jnp.dot(a, b, preferred_element_type=F32)


def _dot_nt(a, b):
    return lax.dot_general(a, b, NT_DIMS, preferred_element_type=F32)


def _split2(x):
    hi = x.astype(BF16)
    lo = (x - hi.astype(F32)).astype(BF16)
    return hi, lo


def _head_of_lane(rows):
    return lax.broadcasted_iota(jnp.int32, (rows, MIX_W), 1) // HEAD_DIM


def _stack_heads(q):
    t = q.shape[0]
    head = _head_of_lane(t)
    qf = q.astype(F32)
    return jnp.concatenate([jnp.where(head == h, qf, 0.0) for h in range(N_HEADS)], axis=0).astype(BF16)


def _unstack_heads(o, t):
    head = _head_of_lane(t)
    out = jnp.where(head == 0, o[0:t], 0.0)
    for h in range(1, N_HEADS):
        out = jnp.where(head == h, o[h * t:(h + 1) * t], out)
    return out


def _unstack_cols(c, t):
    head = _head_of_lane(t)
    out = jnp.broadcast_to(c[0:t], (t, MIX_W))
    for h in range(1, N_HEADS):
        out = jnp.where(head == h, jnp.broadcast_to(c[h * t:(h + 1) * t], (t, MIX_W)), out)
    return out


def _inproj_kernel(x_ref, g_ref, w_ref, wf_ref, sb_ref, fox_ref, d1_ref, d4_ref, d16_ref,
                   lru_ref, f_ref, hn_ref, half_ref):
    hn_ref[...] = _rms(x_ref[0], g_ref[...]).astype(BF16)
    ts = hn_ref.shape[0]

    def project(rows, block):
        return _dot(hn_ref[rows], w_ref[:, block * MIX_W:(block + 1) * MIX_W])

    for j in range(3):
        res = project(slice(None), 6 + j)
        d1_ref[j, 0, 0] = res.astype(BF16)
        for h in range(2):
            half_ref[j, h] = res[:, h * LANES:(h + 1) * LANES]

    def regroup(j, ref, dil):
        for r in range(dil):
            for h in range(2):
                ref[j, 0, r, :, h * LANES:(h + 1) * LANES] = (
                    half_ref[j, h, pl.ds(r, ts // dil, stride=dil), :].astype(BF16))

    pending = [(j, ref, dil) for j in range(3)
               for ref, dil in ((d4_ref, DILATIONS[1]), (d16_ref, DILATIONS[2]))]
    for m, ref in enumerate((sb_ref, fox_ref)):
        for j in range(3):
            ref[j, 0] = project(slice(None), 3 * m + j).astype(BF16)
            regroup(*pending.pop(0))
    lru_ref[0, 0] = project(slice(None), 9)
    for rows in (slice(0, ts // 2), slice(ts // 2, ts)):
        lru_ref[1, 0, rows] = project(rows, 10)
    f_ref[0] = _dot_nt(wf_ref[...], hn_ref[...])


def _inproj(x, g, w_main, w_f, ts=512):
    b, s, d = x.shape
    qkv_shape = jax.ShapeDtypeStruct((3, b, s, MIX_W), BF16)
    qkv_spec = pl.BlockSpec((3, 1, ts, MIX_W), lambda bi, i: (0, bi, i, 0))
    cls_shapes = [jax.ShapeDtypeStruct((3, b, dil, s // dil, MIX_W), BF16) for dil in DILATIONS]
    cls_specs = [pl.BlockSpec((3, 1, dil, ts // dil, MIX_W), lambda bi, i: (0, bi, 0, i, 0))
                 for dil in DILATIONS]
    return pl.pallas_call(
        _inproj_kernel,
        grid=(b, s // ts),
        in_specs=[
            pl.BlockSpec((1, ts, d), lambda bi, i: (bi, i, 0)),
            pl.BlockSpec((1, d), lambda bi, i: (0, 0)),
            pl.BlockSpec(w_main.shape, lambda bi, i: (0, 0)),
            pl.BlockSpec(w_f.shape, lambda bi, i: (0, 0)),
        ],
        out_specs=[
            qkv_spec, qkv_spec, *cls_specs,
            pl.BlockSpec((2, 1, ts, MIX_W), lambda bi, i: (0, bi, i, 0)),
            pl.BlockSpec((1, 8, ts), lambda bi, i: (bi, 0, i)),
        ],
        out_shape=[
            qkv_shape, qkv_shape, *cls_shapes,
            jax.ShapeDtypeStruct((2, b, s, MIX_W), F32),
            jax.ShapeDtypeStruct((b, 8, s), F32),
        ],
        scratch_shapes=[pltpu.VMEM((ts, d), BF16), pltpu.VMEM((3, 2, ts, LANES), F32)],
        compiler_params=_cparams("parallel", "parallel"),
        name="inproj",
    )(x, g, w_main, w_f)


def _decay_kernel(f_ref, bf_ref, tri_ref, o_ref):
    s = f_ref.shape[2]
    tri = tri_ref[...]
    carry = jnp.zeros((8, 1), F32)
    for c in range(s // 128):
        sl = slice(c * 128, (c + 1) * 128)
        log_f = -_softplus(-(f_ref[0, :, sl] + bf_ref[...]))
        p1 = log_f.astype(BF16)
        r1 = log_f - p1.astype(F32)
        p2 = r1.astype(BF16)
        p3 = (r1 - p2.astype(F32)).astype(BF16)
        inc = _dot(p1, tri) + _dot(p2, tri) + _dot(p3, tri)
        o_ref[0, :, sl] = inc + carry
        carry = carry + jnp.sum(log_f, axis=-1, keepdims=True)


def _decay(f_t, b_f):
    b, _, s = f_t.shape
    tri = (lax.broadcasted_iota(jnp.int32, (128, 128), 0)
           <= lax.broadcasted_iota(jnp.int32, (128, 128), 1)).astype(BF16)
    return pl.pallas_call(
        _decay_kernel,
        grid=(b,),
        in_specs=[
            pl.BlockSpec((1, 8, s), lambda bi: (bi, 0, 0)),
            pl.BlockSpec((8, 1), lambda bi: (0, 0)),
            pl.BlockSpec((128, 128), lambda bi: (0, 0)),
        ],
        out_specs=pl.BlockSpec((1, 8, s), lambda bi: (bi, 0, 0)),
        out_shape=jax.ShapeDtypeStruct((b, 8, s), F32),
        compiler_params=_cparams("parallel"),
        name="fox_decay",
    )(f_t, b_f, tri)


def _block_positions(i, ks):
    row = lax.broadcasted_iota(jnp.int32, (N_HEADS * TQC, TK), 0) % TQC + i * TQC
    col = lax.broadcasted_iota(jnp.int32, (N_HEADS * TQC, TK), 1) + ks
    return row, col


SB_SATURATED = 110.0


class _SbItem(NamedTuple):
    q: jax.Array
    k: jax.Array
    v: jax.Array
    strict: Any
    acc: Any
    car: Any
    rows: slice


def _sb_process(items, uo_ref):
    zs = [_dot_nt(it.q, it.k) for it in items]
    staged = []
    for it, z in zip(items, zs):
        drop = _softplus(z)
        cum_in = drop if it.strict is None else jnp.where(it.strict, drop, 0.0)
        staged.append((drop, _dot(cum_in.astype(BF16), uo_ref[...])))
    for it, z, (drop, sm) in zip(items, zs, staged):
        if it.strict is not None:
            att = jnp.where(it.strict, jnp.exp(z - drop - sm[:, :TK]), 0.0)
            it.car[it.rows] = sm[:, TK:]
            it.acc[it.rows] = _dot(att.astype(BF16), it.v)
        else:
            car = it.car[it.rows]
            att = jnp.exp(z - drop - sm[:, :TK] - car)
            it.car[it.rows] = car + sm[:, TK:]
            it.acc[it.rows] += _dot(att.astype(BF16), it.v)


def _sb_kernel(q_ref, k_ref, v_ref, uo_ref, o_ref, qa_ref, qb_ref, acca_ref, accb_ref,
               cara_ref, carb_ref):
    i = pl.program_id(1)
    n_q = pl.num_programs(1) - 1
    half_rows = N_HEADS * TQC // 2
    halves = [slice(0, half_rows), slice(half_rows, 2 * half_rows)]

    def kv_block(kb):
        ks = pl.multiple_of(kb * TK, TK)
        return k_ref[0, 0, pl.ds(ks, TK), :], v_ref[0, 0, pl.ds(ks, TK), :]

    def step(q_cur, acc_cur, car_cur, q_prev, acc_prev, car_prev):
        def diagonal_items():
            q_cur[...] = _stack_heads(q_ref[0, 0])
            k_blk, v_blk = kv_block(i)
            row, col = _block_positions(i, i * TK)
            strict = (col < row)[:half_rows]
            return [_SbItem(q_cur[r], k_blk, v_blk, strict, acc_cur, car_cur, r) for r in halves]

        def earlier_items(kb):
            k_blk, v_blk = kv_block(kb)
            return [_SbItem(q_prev[r], k_blk, v_blk, None, acc_prev, car_prev, r) for r in halves]

        def prev_live():
            return jnp.min(car_prev[:, :LANES]) < SB_SATURATED

        @pl.when(i < 2)
        def _():
            _sb_process(diagonal_items(), uo_ref)

        @pl.when(i >= 2)
        def _():
            live = prev_live()
            has_cur = i < n_q
            pl.when(jnp.logical_and(live, has_cur))(
                lambda: _sb_process(diagonal_items() + earlier_items(i - 2), uo_ref))
            pl.when(jnp.logical_and(live, jnp.logical_not(has_cur)))(
                lambda: _sb_process(earlier_items(i - 2), uo_ref))
            pl.when(jnp.logical_and(jnp.logical_not(live), has_cur))(
                lambda: _sb_process(diagonal_items(), uo_ref))

            def cond(state):
                kb, go = state
                return jnp.logical_and(kb >= 0, go)

            def body(state):
                kb, _ = state
                _sb_process(earlier_items(kb), uo_ref)
                return kb - 1, prev_live()

            lax.while_loop(cond, body, (i - 3, prev_live()))

        @pl.when(i >= 1)
        def _():
            o_ref[0] = _unstack_heads(acc_prev[...], TQC).astype(o_ref.dtype)

    pl.when(i % 2 == 0)(functools.partial(step, qa_ref, acca_ref, cara_ref, qb_ref, accb_ref, carb_ref))
    pl.when(i % 2 == 1)(functools.partial(step, qb_ref, accb_ref, carb_ref, qa_ref, acca_ref, cara_ref))


def _sb_attention(qkv):
    _, b, s, _ = qkv.shape
    assert TQC == TK
    n_q = s // TQC
    rows = N_HEADS * TQC
    later = (lax.broadcasted_iota(jnp.int32, (TK, TK), 0)
             > lax.broadcasted_iota(jnp.int32, (TK, TK), 1))
    u = jnp.concatenate([later, jnp.ones((TK, TK), bool)], axis=1).astype(BF16)
    return pl.pallas_call(
        _sb_kernel,
        grid=(b, n_q + 1),
        in_specs=[
            pl.BlockSpec((1, 1, TQC, MIX_W), lambda bi, i: (0, bi, jnp.minimum(i, n_q - 1), 0)),
            pl.BlockSpec((1, 1, s, MIX_W), lambda bi, i: (1, bi, 0, 0)),
            pl.BlockSpec((1, 1, s, MIX_W), lambda bi, i: (2, bi, 0, 0)),
            pl.BlockSpec((TK, 2 * TK), lambda bi, i: (0, 0)),
        ],
        out_specs=pl.BlockSpec((1, TQC, MIX_W), lambda bi, i: (bi, jnp.maximum(i - 1, 0), 0)),
        out_shape=jax.ShapeDtypeStruct((b, s, MIX_W), BF16),
        scratch_shapes=[
            pltpu.VMEM((rows, MIX_W), BF16), pltpu.VMEM((rows, MIX_W), BF16),
            pltpu.VMEM((rows, MIX_W), F32), pltpu.VMEM((rows, MIX_W), F32),
            pltpu.VMEM((rows, TK), F32), pltpu.VMEM((rows, TK), F32),
        ],
        compiler_params=_cparams("parallel", "arbitrary"),
        name="sb_attn",
    )(qkv, qkv, qkv, u)


def _fox_kernel(q_ref, k_ref, v_ref, cum_ref, o_ref, za_ref, zb_ref, ma_ref, mb_ref, l_ref, acc_ref):
    i = pl.program_id(1)
    n_q = pl.num_programs(1) - 1

    def front_block(qs, kb, diagonal, z_ref, mx_ref):
        ks = pl.multiple_of(kb * TK, TK)
        cum = cum_ref[0, :, pl.ds(ks, TK)]
        decay = jnp.concatenate(
            [jnp.broadcast_to(cum[h:h + 1], (TQC, TK)) for h in range(N_HEADS)], axis=0)
        z = _dot_nt(qs, k_ref[0, 0, pl.ds(ks, TK), :]) - decay
        if diagonal:
            row, col = _block_positions(i, ks)
            z = jnp.where(col <= row, z, NEG_BIG)
        z_ref[kb] = z
        zmax = jnp.maximum(z[:, :LANES], z[:, LANES:])
        mx_ref[...] = zmax if diagonal else jnp.maximum(mx_ref[...], zmax)

    def back_block(kb, z_ref, mx_ref):
        ks = pl.multiple_of(kb * TK, TK)
        m = mx_ref[...]
        p = jnp.exp(z_ref[kb] - jnp.concatenate([m, m], axis=1))
        l_ref[...] += p[:, :LANES] + p[:, LANES:]
        acc_ref[...] += _dot(p.astype(BF16), v_ref[0, 0, pl.ds(ks, TK), :])

    def step(z_cur, m_cur, z_prev, m_prev):
        @pl.when(i < n_q)
        def _():
            qs = _stack_heads(q_ref[0, 0])
            front_block(qs, i, True, z_cur, m_cur)

            def both(kb):
                front_block(qs, kb, False, z_cur, m_cur)
                back_block(kb, z_prev, m_prev)

            def pair(j, carry):
                both(2 * j)
                both(2 * j + 1)
                return carry

            lax.fori_loop(0, i // 2, pair, 0)
            pl.when(i % 2 == 1)(lambda: both(i - 1))
            m = jnp.max(m_cur[...], axis=-1, keepdims=True)
            m_cur[...] = jnp.broadcast_to(m, m_cur.shape)

        @pl.when(i == n_q)
        def _():
            def back_only(kb, carry):
                back_block(kb, z_prev, m_prev)
                return carry

            lax.fori_loop(0, i, back_only, 0)

    l_ref[...] = jnp.zeros_like(l_ref)
    acc_ref[...] = jnp.zeros_like(acc_ref)
    pl.when(i % 2 == 0)(functools.partial(step, za_ref, ma_ref, zb_ref, mb_ref))
    pl.when(i % 2 == 1)(functools.partial(step, zb_ref, mb_ref, za_ref, ma_ref))

    @pl.when(i > 0)
    def _():
        l = jnp.sum(l_ref[...], axis=-1, keepdims=True)
        o_ref[0] = _unstack_heads(acc_ref[...] / l, TQC).astype(o_ref.dtype)


def _fox_attention(qkv, cum):
    _, b, s, _ = qkv.shape
    assert TQC == TK
    n_q = s // TQC
    rows = N_HEADS * TQC
    return pl.pallas_call(
        _fox_kernel,
        grid=(b, n_q + 1),
        in_specs=[
            pl.BlockSpec((1, 1, TQC, MIX_W), lambda bi, i: (0, bi, jnp.minimum(i, n_q - 1), 0)),
            pl.BlockSpec((1, 1, s, MIX_W), lambda bi, i: (1, bi, 0, 0)),
            pl.BlockSpec((1, 1, s, MIX_W), lambda bi, i: (2, bi, 0, 0)),
            pl.BlockSpec((1, 8, s), lambda bi, i: (bi, 0, 0)),
        ],
        out_specs=pl.BlockSpec((1, TQC, MIX_W), lambda bi, i: (bi, jnp.maximum(i - 1, 0), 0)),
        out_shape=jax.ShapeDtypeStruct((b, s, MIX_W), BF16),
        scratch_shapes=[
            pltpu.VMEM((s // TK, rows, TK), F32),
            pltpu.VMEM((s // TK, rows, TK), F32),
            pltpu.VMEM((rows, LANES), F32),
            pltpu.VMEM((rows, LANES), F32),
            pltpu.VMEM((rows, LANES), F32),
            pltpu.VMEM((rows, MIX_W), F32),
        ],
        compiler_params=_cparams("parallel", "arbitrary"),
        name="fox_attn",
    )(qkv, qkv, qkv, cum)


def _t5_bucket(dist):
    n = jnp.maximum(dist, 0)
    max_exact = NUM_BUCKETS // 2
    nf = jnp.maximum(n, 1).astype(F32)
    large = max_exact + (jnp.log(nf / max_exact) / math.log(MAX_DISTANCE / max_exact)
                         * (NUM_BUCKETS - max_exact)).astype(jnp.int32)
    large = jnp.minimum(large, NUM_BUCKETS - 1)
    return jnp.where(n < max_exact, n, large)


def _bias_kernel(idx_ref, rb_ref, o_ref):
    for d in range(len(DILATIONS)):
        for variant in range(2):
            idx = idx_ref[d, variant]
            for h in range(N_HEADS):
                acc = jnp.full(idx.shape, NEG_BIG, F32)
                for bucket in range(NUM_BUCKETS):
                    acc = jnp.where(idx == bucket, rb_ref[bucket, h], acc)
                o_ref[d, variant, h] = acc


def _bias_tables(rel_bias):
    t = jnp.arange(TQ)[:, None]
    c = jnp.arange(2 * TQ)[None, :]
    delta = TQ + t - c
    valid = (delta >= 0) & (delta <= DIL_STEPS)
    idx = jnp.stack([jnp.where(valid, _t5_bucket(delta * dil), -1) for dil in DILATIONS])
    idx_first = jnp.concatenate([idx[:, :, TQ:], jnp.full_like(idx[:, :, TQ:], -1)], axis=-1)
    idx = jnp.stack([idx, idx_first], axis=1)
    out = pl.pallas_call(
        _bias_kernel,
        in_specs=[
            pl.BlockSpec(memory_space=pltpu.VMEM),
            pl.BlockSpec(memory_space=pltpu.SMEM),
        ],
        out_specs=pl.BlockSpec(memory_space=pltpu.VMEM),
        out_shape=jax.ShapeDtypeStruct((len(DILATIONS), 2, N_HEADS, TQ, 2 * TQ), F32),
        name="dil_bias",
    )(idx.astype(jnp.int32), rel_bias)
    return out.reshape(len(DILATIONS), 2, N_HEADS * TQ, 2 * TQ)


DIL_GROUP = 8


def _dil_kernel(q_ref, k_ref, v_ref, t_ref, o_ref, lse_ref, *, by_class, group, n_groups):
    g = pl.program_id(2)

    def attend(jj, scores, v_win):
        m = jnp.max(scores, axis=-1, keepdims=True)
        p = jnp.exp(scores - m)
        l = jnp.sum(p, axis=-1, keepdims=True)
        o = _unstack_heads(_dot(p.astype(BF16), v_win) / l, TQ).astype(o_ref.dtype)
        lse = _unstack_cols(m + jnp.log(l), TQ)
        if by_class:
            o_ref[0, jj] = o
            lse_ref[0, jj] = lse
        else:
            o_ref[0, 0, jj * TQ:(jj + 1) * TQ] = o
            lse_ref[0, 0, jj * TQ:(jj + 1) * TQ] = lse

    pending = []
    for jj in range(group):
        if by_class:
            scores = (_dot_nt(_stack_heads(q_ref[0, 0, jj]), k_ref[0, 0, jj])
                      + t_ref[0, 0, :, TQ:2 * TQ])
            pending.append((scores, v_ref[0, 0, jj]))
            continue
        qs = _stack_heads(q_ref[0, 0, 0, jj * TQ:(jj + 1) * TQ])
        if n_groups == 1 or jj > 0:
            n = g * group + jj if n_groups > 1 else jj
            start = (n - 1) * TQ if jj > 0 else 0
            bias = t_ref[0, 0 if jj > 0 else 1]
        else:
            start = jnp.maximum(g * group - 1, 0) * TQ
            bias = t_ref[0, (g == 0).astype(jnp.int32)]
        if not isinstance(start, int):
            start = pl.multiple_of(start, TQ)
        window = pl.ds(start, 2 * TQ)
        pending.append((_dot_nt(qs, k_ref[0, 0, 0, window, :]) + bias, v_ref[0, 0, 0, window, :]))
    for jj, (scores, v_win) in enumerate(pending):
        attend(jj, scores, v_win)


def _dil_branch(qkv, tables, branch):
    _, b, dil, cls_len, _ = qkv.shape
    n_blocks = cls_len // TQ
    by_class = n_blocks == 1
    group = min(DIL_GROUP, dil if by_class else n_blocks)
    if by_class:
        grid = (b, dil // group, 1)
        qkv_specs = [pl.BlockSpec((1, 1, group, TQ, MIX_W), functools.partial(
            lambda bi, c, g, j: (j, bi, c, 0, 0), j=j)) for j in range(3)]
        out_spec = pl.BlockSpec((1, group, TQ, MIX_W), lambda bi, c, g: (bi, c, 0, 0))
    else:
        grid = (b, dil, n_blocks // group)
        qkv_specs = [pl.BlockSpec((1, 1, 1, group * TQ, MIX_W), lambda bi, r, g: (0, bi, r, g, 0))]
        qkv_specs += [pl.BlockSpec((1, 1, 1, cls_len, MIX_W), functools.partial(
            lambda bi, r, g, j: (j, bi, r, 0, 0), j=j)) for j in (1, 2)]
        out_spec = pl.BlockSpec((1, 1, group * TQ, MIX_W), lambda bi, r, g: (bi, r, g, 0))
    o, lse = pl.pallas_call(
        functools.partial(_dil_kernel, by_class=by_class, group=group, n_groups=grid[2]),
        grid=grid,
        in_specs=qkv_specs + [
            pl.BlockSpec((1, 2, N_HEADS * TQ, 2 * TQ), lambda bi, r, g: (branch, 0, 0, 0))],
        out_specs=[out_spec, out_spec],
        out_shape=[jax.ShapeDtypeStruct((b, dil, cls_len, MIX_W), BF16),
                   jax.ShapeDtypeStruct((b, dil, cls_len, MIX_W), F32)],
        compiler_params=_cparams("parallel", "parallel", "arbitrary"),
        name=f"dil_attn_{dil}",
    )(qkv, qkv, qkv, tables)
    to_seq = lambda a: a.transpose(0, 2, 1, 3).reshape(b, dil * cls_len, MIX_W)
    return to_seq(o), to_seq(lse)


def _dil_mixture(outs, lses):
    m = functools.reduce(jnp.maximum, lses)
    es = [jnp.exp(a - m) for a in lses]
    tot = functools.reduce(jnp.add, es)
    return functools.reduce(jnp.add, [(e / tot) * o for e, o in zip(es, outs)])


def _shift_rows(x, k, fill):
    row = lax.broadcasted_iota(jnp.int32, x.shape, 0)
    return jnp.where(row >= k, pltpu.roll(x, k, 0), fill)


def _lru_kernel(xg_ref, cw_ref, cb_ref, wa_ref, ba_ref, wx_ref, bx_ref, lam_ref, o_ref, h_ref):
    x = xg_ref[0, 0]
    gate = xg_ref[1, 0]
    s = x.shape[0]
    kw = cw_ref.shape[0]
    xc = cb_ref[...] + cw_ref[kw - 1:kw, :] * x
    for j in range(kw - 1):
        xc = xc + cw_ref[j:j + 1, :] * _shift_rows(x, kw - 1 - j, 0.0)
    xcb = xc.astype(BF16)
    r = jax.nn.sigmoid(_dot(xcb, wa_ref[...]) + ba_ref[...])
    i_gate = jax.nn.sigmoid(_dot(xcb, wx_ref[...]) + bx_ref[...])
    a = jnp.exp(-LRU_C * r * _softplus(-lam_ref[...]))
    h = jnp.sqrt(1.0 - a * a) * (i_gate * xc)
    in_group = lax.broadcasted_iota(jnp.int32, x.shape, 0) % SUBLANES
    k = 1
    while k < SUBLANES:
        prev = in_group >= k
        h = a * jnp.where(prev, pltpu.roll(h, k, 0), 0.0) + h
        a = a * jnp.where(prev, pltpu.roll(a, k, 0), 1.0)
        k *= 2
    carry = jnp.zeros((1, x.shape[1]), F32)
    for c in range(s // SUBLANES):
        rows = slice(c * SUBLANES, (c + 1) * SUBLANES)
        hc = h[rows] + a[rows] * carry
        h_ref[rows] = hc
        carry = hc[SUBLANES - 1:SUBLANES]
    o_ref[0] = (h_ref[...] * _gelu(gate)).astype(o_ref.dtype)


def _lru(xg, conv_w, conv_b, wa_bd, b_a, wx_bd, b_x, lam):
    _, b, s, w = xg.shape
    full = lambda a: pl.BlockSpec(a.shape, lambda bi: (0,) * a.ndim)
    args = (conv_w, conv_b, wa_bd, b_a, wx_bd, b_x, lam)
    return pl.pallas_call(
        _lru_kernel,
        grid=(b,),
        in_specs=[pl.BlockSpec((2, 1, s, w), lambda bi: (0, bi, 0, 0))] + [full(a) for a in args],
        out_specs=pl.BlockSpec((1, s, w), lambda bi: (bi, 0, 0)),
        out_shape=jax.ShapeDtypeStruct((b, s, w), BF16),
        scratch_shapes=[pltpu.VMEM((s, w), F32)],
        compiler_params=_cparams("parallel"),
        name="rg_lru",
    )(xg, *args)


def _memkv_kernel(mem_ref, g_ref, wk_ref, wv_ref, k_ref, v_ref):
    mn = _rms(mem_ref[0], g_ref[...]).astype(BF16)
    k_ref[0] = _dot(mn, wk_ref[...]).astype(BF16)
    v_ref[0] = _dot(mn, wv_ref[...]).astype(BF16)


def _memkv(mem, g, w_ck, w_cv):
    b, m, d = mem.shape
    out = jax.ShapeDtypeStruct((b, m, MIX_W), BF16)
    return pl.pallas_call(
        _memkv_kernel,
        grid=(b,),
        in_specs=[
            pl.BlockSpec((1, m, d), lambda bi: (bi, 0, 0)),
            pl.BlockSpec((1, d), lambda bi: (0, 0)),
            pl.BlockSpec((d, MIX_W), lambda bi: (0, 0)),
            pl.BlockSpec((d, MIX_W), lambda bi: (0, 0)),
        ],
        out_specs=[pl.BlockSpec((1, m, MIX_W), lambda bi: (bi, 0, 0))] * 2,
        out_shape=[out, out],
        compiler_params=_cparams("parallel"),
        name="mem_kv",
    )(mem, g, w_ck, w_cv)


def _mix_cross_kernel(x_ref, sb_ref, fox_ref, lru_ref, o1_ref, l1_ref, o4_ref, l4_ref,
                      o16_ref, l16_ref, wo_ref, g_ref, wq_ref, km_ref, vm_ref, wco_ref, o_ref):
    t = MIX_SUB
    subs = [slice(u * t, (u + 1) * t) for u in range(x_ref.shape[1] // t)]
    xs = []
    for r in subs:
        dil = _dil_mixture([ref[0, r] for ref in (o1_ref, o4_ref, o16_ref)],
                           [ref[0, r] for ref in (l1_ref, l4_ref, l16_ref)]).astype(BF16)
        x = x_ref[0, r]
        for j, mixed in enumerate((sb_ref[0, r], fox_ref[0, r], dil, lru_ref[0, r])):
            x = x + _dot(mixed, wo_ref[j])
        xs.append(x)
    qs = [_stack_heads(_dot(_rms(x, g_ref[...]).astype(BF16), wq_ref[...])) for x in xs]
    scores = [_dot_nt(q, km_ref[0]) for q in qs]
    outs = []
    for sc in scores:
        p = jnp.exp(sc - jnp.max(sc, axis=-1, keepdims=True))
        p = p / jnp.sum(p, axis=-1, keepdims=True)
        outs.append(_unstack_heads(_dot(p.astype(BF16), vm_ref[0]), t))
    for r, x, o in zip(subs, xs, outs):
        o_ref[0, r] = x + _dot(o.astype(BF16), wco_ref[...])


MIX_SUB = 128


def _mix_cross(x, o_sb, o_fox, o_lru, dil_branches, w_out, g, w_cq, k_mem, v_mem, w_co, ts=512):
    b, s, d = x.shape
    m = k_mem.shape[1]
    mix_spec = pl.BlockSpec((1, ts, MIX_W), lambda bi, i: (bi, i, 0))
    const = lambda a: pl.BlockSpec(a.shape, lambda bi, i: (0,) * a.ndim)
    mem_spec = pl.BlockSpec((1, m, MIX_W), lambda bi, i: (bi, 0, 0))
    dil_flat = [a for pair in dil_branches for a in pair]
    return pl.pallas_call(
        _mix_cross_kernel,
        grid=(b, s // ts),
        in_specs=[pl.BlockSpec((1, ts, d), lambda bi, i: (bi, i, 0))] + [mix_spec] * (3 + len(dil_flat))
                 + [const(w_out), const(g), const(w_cq), mem_spec, mem_spec, const(w_co)],
        out_specs=pl.BlockSpec((1, ts, d), lambda bi, i: (bi, i, 0)),
        out_shape=jax.ShapeDtypeStruct((b, s, d), F32),
        compiler_params=_cparams("parallel", "parallel"),
        name="mix_cross",
    )(x, o_sb, o_fox, o_lru, *dil_flat, w_out, g, w_cq, k_mem, v_mem, w_co)


HALO = 8
FFN_DOWN_LAG = 2
FFN_ACT_SLOTS = FFN_DOWN_LAG + 1


def _ffn_kernel(x_ref, g_ref, wup_ref, cw_ref, cb_ref, wdn_ref, gf_ref, o_ref,
                hn_ref, ext_ref, tail_ref, act_ref, acc_ref, *, final_norm):
    i = pl.program_id(1)
    x = x_ref[0]
    t = x.shape[0]
    hn_ref[...] = _rms(x, g_ref[...]).astype(BF16)
    acc_ref[...] = jnp.zeros_like(acc_ref)

    @pl.when(i == 0)
    def _():
        tail_ref[...] = jnp.zeros_like(tail_ref)

    def project(c):
        for half in range(2):
            blk = half * N_FF_CHUNKS + c
            slot = 2 * (c % 2) + half
            pre = _dot(hn_ref[...], wup_ref[:, blk * FF_CHUNK:(blk + 1) * FF_CHUNK])
            ext_ref[slot, 0:HALO] = tail_ref[blk]
            ext_ref[slot, HALO:] = pre
            tail_ref[blk] = pre[t - HALO:]

    def conv(c, slot):
        ext = ext_ref.at[slot]
        w = cw_ref[c]
        return (cb_ref[c] + w[2:3] * ext[pl.ds(HALO, t), :]
                + w[1:2] * ext[pl.ds(HALO - 1, t), :]
                + w[0:1] * ext[pl.ds(HALO - 2, t), :])

    def down(c):
        acc_ref[...] += _dot(act_ref[c % FFN_ACT_SLOTS], wdn_ref[c])

    project(0)
    for c in range(N_FF_CHUNKS):
        if c + 1 < N_FF_CHUNKS:
            project(c + 1)
        if c >= FFN_DOWN_LAG:
            down(c - FFN_DOWN_LAG)
        up = conv(c, 2 * (c % 2))
        gate = conv(N_FF_CHUNKS + c, 2 * (c % 2) + 1)
        act_ref[c % FFN_ACT_SLOTS] = (_gelu(gate) * up).astype(BF16)
    for c in range(N_FF_CHUNKS - FFN_DOWN_LAG, N_FF_CHUNKS):
        down(c)
    out = x + acc_ref[...]
    if final_norm:
        out = _rms(out, gf_ref[...])
    o_ref[0] = out


def _ffn(x, g, w_up, conv_w, conv_b, w_down, g_final, final_norm, ts=256):
    b, s, d = x.shape
    const = lambda a: pl.BlockSpec(a.shape, lambda bi, i: (0,) * a.ndim)
    return pl.pallas_call(
        functools.partial(_ffn_kernel, final_norm=final_norm),
        grid=(b, s // ts),
        in_specs=[
            pl.BlockSpec((1, ts, d), lambda bi, i: (bi, i, 0)),
            const(g), const(w_up), const(conv_w), const(conv_b), const(w_down), const(g_final),
        ],
        out_specs=pl.BlockSpec((1, ts, d), lambda bi, i: (bi, i, 0)),
        out_shape=jax.ShapeDtypeStruct((b, s, d), F32),
        scratch_shapes=[
            pltpu.VMEM((ts, d), BF16),
            pltpu.VMEM((4, HALO + ts, FF_CHUNK), F32),
            pltpu.VMEM((2 * N_FF_CHUNKS, HALO, FF_CHUNK), F32),
            pltpu.VMEM((FFN_ACT_SLOTS, ts, FF_CHUNK), BF16),
            pltpu.VMEM((ts, d), F32),
        ],
        compiler_params=_cparams("parallel", "arbitrary"),
        name="conv_geglu",
    )(x, g, w_up, conv_w, conv_b, w_down, g_final)


def _block_diag(w):
    g, n, _ = w.shape
    eye = jnp.eye(g, dtype=w.dtype)
    return (eye[:, None, :, None] * w[:, :, None, :]).reshape(g * n, g * n)


def kernel(x, mem, norm_mix_g, w_in, b_forget, lru_conv_w, lru_conv_b, lru_w_a, lru_b_a, lru_w_x, lru_b_x, lru_lambda, w_out, norm_cross_g, norm_mem_g, w_cq, w_ck, w_cv, w_co, norm_ffn_g, w_up, ffn_conv_w, ffn_conv_b, w_down, rel_bias, final_norm_g):
    depth = w_in.shape[0]
    qkv_w = 3 * MIX_W
    q_scale = HEAD_DIM ** -0.5
    col_scale = jnp.ones((3, 3, MIX_W), F32).at[:, 0, :].set(q_scale).reshape(-1)
    col_scale = jnp.concatenate([col_scale, jnp.ones((2 * MIX_W,), F32)])
    tables = _bias_tables(rel_bias)
    row = lambda v: v.reshape(1, -1)

    for l in range(depth):
        w = w_in[l]
        n_f = N_HEADS
        w_main = jnp.concatenate([w[:, :2 * qkv_w], w[:, 2 * qkv_w + n_f:]], axis=1)
        w_main = (w_main * col_scale).astype(BF16)
        w_f = jnp.zeros((8, D_MODEL), F32).at[:n_f].set(w[:, 2 * qkv_w:2 * qkv_w + n_f].T).astype(BF16)
        b_f = jnp.zeros((8, 1), F32).at[:n_f, 0].set(b_forget[l])

        sb_qkv, fox_qkv, *dil_qkvs, lru_xg, f_t = _inproj(x, row(norm_mix_g[l]), w_main, w_f)
        o_sb = _sb_attention(sb_qkv)
        o_fox = _fox_attention(fox_qkv, _decay(f_t, b_f))
        dil_branches = [_dil_branch(qkv, tables, br) for br, qkv in enumerate(dil_qkvs)]
        o_lru = _lru(lru_xg, lru_conv_w[l], row(lru_conv_b[l]),
                     _block_diag(lru_w_a[l]).astype(BF16), row(lru_b_a[l]),
                     _block_diag(lru_w_x[l]).astype(BF16), row(lru_b_x[l]), row(lru_lambda[l]))

        k_mem, v_mem = _memkv(mem, row(norm_mem_g[l]), w_ck[l].astype(BF16), w_cv[l].astype(BF16))
        x = _mix_cross(x, o_sb, o_fox, o_lru, dil_branches,
                       w_out[l].reshape(4, MIX_W, D_MODEL).astype(BF16), row(norm_cross_g[l]),
                       (w_cq[l] * q_scale).astype(BF16), k_mem, v_mem, w_co[l].astype(BF16))

        wu = w_up[l].astype(BF16)
        cw = ffn_conv_w[l].reshape(-1, 2 * N_FF_CHUNKS, FF_CHUNK).transpose(1, 0, 2)
        cb = ffn_conv_b[l].reshape(2 * N_FF_CHUNKS, 1, FF_CHUNK)
        wd = w_down[l].reshape(N_FF_CHUNKS, FF_CHUNK, D_MODEL).astype(BF16)
        x = _ffn(x, row(norm_ffn_g[l]), wu, cw, cb, wd, row(final_norm_g), l == depth - 1)
    return x
```

```python
import functools
import math

import jax
import jax.numpy as jnp
from jax import lax
from jax.experimental import pallas as pl
from jax.experimental.pallas import tpu as pltpu

F32 = jnp.float32
BF16 = jnp.bfloat16

LANES = 128
SUBLANES = 8
D_MODEL = 1024
HEAD_DIM = 64
N_HEADS = 4
MIX_W = N_HEADS * HEAD_DIM
D_FF = 2816
FF_CHUNK = 256
N_FF_CHUNKS = D_FF // FF_CHUNK
N_MEM = 256
NUM_BUCKETS = 32
MAX_DISTANCE = 2048
DILATIONS = (1, 4, 16)
DIL_STEPS = 128
LRU_C = 8.0
EPS = 1e-6
NEG_BIG = -1e30

TQ = 128
TQC = 256
TK = 256
VMEM_LIMIT = 56 * 1024 * 1024

NT_DIMS = (((1,), (1,)), ((), ()))


def _cparams(*sem):
    return pltpu.CompilerParams(dimension_semantics=sem, vmem_limit_bytes=VMEM_LIMIT)


def _rms(x, g):
    return x * lax.rsqrt(jnp.mean(x * x, axis=-1, keepdims=True) + EPS) * g


def _softplus(x):
    return jnp.maximum(x, 0.0) + jnp.log(1.0 + jnp.exp(-jnp.abs(x)))


def _gelu(x):
    return 0.5 * x * (1.0 + lax.erf(x * math.sqrt(0.5)))


def _dot(a, b):
    return jnp.dot(a, b, preferred_element_type=F32)


def _dot_nt(a, b):
    return lax.dot_general(a, b, NT_DIMS, preferred_element_type=F32)


def _split2(x):
    hi = x.astype(BF16)
    lo = (x - hi.astype(F32)).astype(BF16)
    return hi, lo


def _head_of_lane(rows):
    return lax.broadcasted_iota(jnp.int32, (rows, MIX_W), 1) // HEAD_DIM


def _stack_heads(q):
    t = q.shape[0]
    head = _head_of_lane(t)
    qf = q.astype(F32)
    return jnp.concatenate([jnp.where(head == h, qf, 0.0) for h in range(N_HEADS)], axis=0).astype(BF16)


def _unstack_heads(o, t):
    head = _head_of_lane(t)
    out = jnp.where(head == 0, o[0:t], 0.0)
    for h in range(1, N_HEADS):
        out = jnp.where(head == h, o[h * t:(h + 1) * t], out)
    return out


def _unstack_cols(c, t):
    head = _head_of_lane(t)
    out = jnp.broadcast_to(c[0:t], (t, MIX_W))
    for h in range(1, N_HEADS):
        out = jnp.where(head == h, jnp.broadcast_to(c[h * t:(h + 1) * t], (t, MIX_W)), out)
    return out


def _inproj_kernel(x_ref, g_ref, w_ref, wf_ref, sb_ref, fox_ref, d1_ref, d4_ref, d16_ref,
                   lru_ref, f_ref, hn_ref, half_ref):
    hn_ref[...] = _rms(x_ref[0], g_ref[...]).astype(BF16)
    ts = hn_ref.shape[0]

    def project(rows, block):
        return _dot(hn_ref[rows], w_ref[:, block * MIX_W:(block + 1) * MIX_W])

    for j in range(3):
        res = project(slice(None), 6 + j)
        d1_ref[j, 0, 0] = res.astype(BF16)
        for h in range(2):
            half_ref[j, h] = res[:, h * LANES:(h + 1) * LANES]

    def regroup(j, ref, dil):
        for r in range(dil):
            for h in range(2):
                ref[j, 0, r, :, h * LANES:(h + 1) * LANES] = (
                    half_ref[j, h, pl.ds(r, ts // dil, stride=dil), :].astype(BF16))

    pending = [(j, ref, dil) for j in range(3)
               for ref, dil in ((d4_ref, DILATIONS[1]), (d16_ref, DILATIONS[2]))]
    for m, ref in enumerate((sb_ref, fox_ref)):
        for j in range(3):
            ref[j, 0] = project(slice(None), 3 * m + j).astype(BF16)
            regroup(*pending.pop(0))
    lru_ref[0, 0] = project(slice(None), 9)
    for rows in (slice(0, ts // 2), slice(ts // 2, ts)):
        lru_ref[1, 0, rows] = project(rows, 10)
    f_ref[0] = _dot_nt(wf_ref[...], hn_ref[...])


def _inproj(x, g, w_main, w_f, ts=512):
    b, s, d = x.shape
    qkv_shape = jax.ShapeDtypeStruct((3, b, s, MIX_W), BF16)
    qkv_spec = pl.BlockSpec((3, 1, ts, MIX_W), lambda bi, i: (0, bi, i, 0))
    cls_shapes = [jax.ShapeDtypeStruct((3, b, dil, s // dil, MIX_W), BF16) for dil in DILATIONS]
    cls_specs = [pl.BlockSpec((3, 1, dil, ts // dil, MIX_W), lambda bi, i: (0, bi, 0, i, 0))
                 for dil in DILATIONS]
    return pl.pallas_call(
        _inproj_kernel,
        grid=(b, s // ts),
        in_specs=[
            pl.BlockSpec((1, ts, d), lambda bi, i: (bi, i, 0)),
            pl.BlockSpec((1, d), lambda bi, i: (0, 0)),
            pl.BlockSpec(w_main.shape, lambda bi, i: (0, 0)),
            pl.BlockSpec(w_f.shape, lambda bi, i: (0, 0)),
        ],
        out_specs=[
            qkv_spec, qkv_spec, *cls_specs,
            pl.BlockSpec((2, 1, ts, MIX_W), lambda bi, i: (0, bi, i, 0)),
            pl.BlockSpec((1, 8, ts), lambda bi, i: (bi, 0, i)),
        ],
        out_shape=[
            qkv_shape, qkv_shape, *cls_shapes,
            jax.ShapeDtypeStruct((2, b, s, MIX_W), F32),
            jax.ShapeDtypeStruct((b, 8, s), F32),
        ],
        scratch_shapes=[pltpu.VMEM((ts, d), BF16), pltpu.VMEM((3, 2, ts, LANES), F32)],
        compiler_params=_cparams("parallel", "parallel"),
        name="inproj",
    )(x, g, w_main, w_f)


def _decay_kernel(f_ref, bf_ref, tri_ref, o_ref):
    s = f_ref.shape[2]
    tri = tri_ref[...]
    carry = jnp.zeros((8, 1), F32)
    for c in range(s // 128):
        sl = slice(c * 128, (c + 1) * 128)
        log_f = -_softplus(-(f_ref[0, :, sl] + bf_ref[...]))
        p1 = log_f.astype(BF16)
        r1 = log_f - p1.astype(F32)
        p2 = r1.astype(BF16)
        p3 = (r1 - p2.astype(F32)).astype(BF16)
        inc = _dot(p1, tri) + _dot(p2, tri) + _dot(p3, tri)
        o_ref[0, :, sl] = inc + carry
        carry = carry + jnp.sum(log_f, axis=-1, keepdims=True)


def _decay(f_t, b_f):
    b, _, s = f_t.shape
    tri = (lax.broadcasted_iota(jnp.int32, (128, 128), 0)
           <= lax.broadcasted_iota(jnp.int32, (128, 128), 1)).astype(BF16)
    return pl.pallas_call(
        _decay_kernel,
        grid=(b,),
        in_specs=[
            pl.BlockSpec((1, 8, s), lambda bi: (bi, 0, 0)),
            pl.BlockSpec((8, 1), lambda bi: (0, 0)),
            pl.BlockSpec((128, 128), lambda bi: (0, 0)),
        ],
        out_specs=pl.BlockSpec((1, 8, s), lambda bi: (bi, 0, 0)),
        out_shape=jax.ShapeDtypeStruct((b, 8, s), F32),
        compiler_params=_cparams("parallel"),
        name="fox_decay",
    )(f_t, b_f, tri)


def _block_positions(i, ks):
    row = lax.broadcasted_iota(jnp.int32, (N_HEADS * TQC, TK), 0) % TQC + i * TQC
    col = lax.broadcasted_iota(jnp.int32, (N_HEADS * TQC, TK), 1) + ks
    return row, col


SB_SATURATED = 110.0


def _sb_kernel(q_ref, k_ref, v_ref, uo_ref, o_ref, acc_ref, car_ref):
    i = pl.program_id(1)
    qs = _stack_heads(q_ref[0, 0])

    half_rows = N_HEADS * TQC // 2
    halves = [slice(0, half_rows), slice(half_rows, 2 * half_rows)]

    def process(blocks):
        items = []
        for kb, diagonal in blocks:
            ks = pl.multiple_of(kb * TK, TK)
            k_blk = k_ref[0, 0, pl.ds(ks, TK), :]
            v_blk = v_ref[0, 0, pl.ds(ks, TK), :]
            strict = None
            if diagonal:
                row, col = _block_positions(i, ks)
                strict = (col < row)[:half_rows]
            items += [(r, k_blk, v_blk, strict) for r in halves]
        zs = [_dot_nt(qs[r], k_blk) for r, k_blk, _, _ in items]
        staged = []
        for (r, _, _, strict), z in zip(items, zs):
            drop = _softplus(z)
            cum_in = drop if strict is None else jnp.where(strict, drop, 0.0)
            staged.append((drop, _dot(cum_in.astype(BF16), uo_ref[...])))
        for (r, _, v_blk, strict), z, (drop, sm) in zip(items, zs, staged):
            if strict is not None:
                att = jnp.where(strict, jnp.exp(z - drop - sm[:, :TK]), 0.0)
                car_ref[r] = sm[:, TK:]
                acc_ref[r] = _dot(att.astype(BF16), v_blk)
            else:
                car = car_ref[r]
                att = jnp.exp(z - drop - sm[:, :TK] - car)
                car_ref[r] = car + sm[:, TK:]
                acc_ref[r] += _dot(att.astype(BF16), v_blk)

    def live():
        return jnp.min(car_ref[:, :LANES]) < SB_SATURATED

    pl.when(i == 0)(lambda: process([(i, True)]))
    pl.when(i > 0)(lambda: process([(i, True), (i - 1, False)]))

    def cond(state):
        kb, go = state
        return jnp.logical_and(kb >= 0, go)

    def body(state):
        kb, _ = state
        process([(kb, False)])
        return kb - 1, live()

    lax.while_loop(cond, body, (i - 2, live()))
    o_ref[0] = _unstack_heads(acc_ref[...], TQC).astype(o_ref.dtype)


def _sb_attention(qkv):
    _, b, s, _ = qkv.shape
    assert TQC == TK
    later =(lax.broadcasted_iota(jnp.int32, (TK, TK), 0)
             > lax.broadcasted_iota(jnp.int32, (TK, TK), 1))
    u = jnp.concatenate([later, jnp.ones((TK, TK), bool)], axis=1).astype(BF16)
    return pl.pallas_call(
        _sb_kernel,
        grid=(b, s // TQC),
        in_specs=[
            pl.BlockSpec((1, 1, TQC, MIX_W), lambda bi, i: (0, bi, i, 0)),
            pl.BlockSpec((1, 1, s, MIX_W), lambda bi, i: (1, bi, 0, 0)),
            pl.BlockSpec((1, 1, s, MIX_W), lambda bi, i: (2, bi, 0, 0)),
            pl.BlockSpec((TK, 2 * TK), lambda bi, i: (0, 0)),
        ],
        out_specs=pl.BlockSpec((1, TQC, MIX_W), lambda bi, i: (bi, i, 0)),
        out_shape=jax.ShapeDtypeStruct((b, s, MIX_W), BF16),
        scratch_shapes=[
            pltpu.VMEM((N_HEADS * TQC, MIX_W), F32),
            pltpu.VMEM((N_HEADS * TQC, TK), F32),
        ],
        compiler_params=_cparams("parallel", "arbitrary"),
        name="sb_attn",
    )(qkv, qkv, qkv, u)


def _fox_kernel(q_ref, k_ref, v_ref, cum_ref, o_ref, za_ref, zb_ref, ma_ref, mb_ref, l_ref, acc_ref):
    i = pl.program_id(1)
    n_q = pl.num_programs(1) - 1

    def front_block(qs, kb, diagonal, z_ref, mx_ref):
        ks = pl.multiple_of(kb * TK, TK)
        cum = cum_ref[0, :, pl.ds(ks, TK)]
        decay = jnp.concatenate(
            [jnp.broadcast_to(cum[h:h + 1], (TQC, TK)) for h in range(N_HEADS)], axis=0)
        z = _dot_nt(qs, k_ref[0, 0, pl.ds(ks, TK), :]) - decay
        if diagonal:
            row, col = _block_positions(i, ks)
            z = jnp.where(col <= row, z, NEG_BIG)
        z_ref[kb] = z
        zmax = jnp.maximum(z[:, :LANES], z[:, LANES:])
        mx_ref[...] = zmax if diagonal else jnp.maximum(mx_ref[...], zmax)

    def back_block(kb, z_ref, mx_ref):
        ks = pl.multiple_of(kb * TK, TK)
        m = mx_ref[...]
        p = jnp.exp(z_ref[kb] - jnp.concatenate([m, m], axis=1))
        l_ref[...] += p[:, :LANES] + p[:, LANES:]
        acc_ref[...] += _dot(p.astype(BF16), v_ref[0, 0, pl.ds(ks, TK), :])

    def step(z_cur, m_cur, z_prev, m_prev):
        @pl.when(i < n_q)
        def _():
            qs = _stack_heads(q_ref[0, 0])
            front_block(qs, i, True, z_cur, m_cur)

            def both(kb):
                front_block(qs, kb, False, z_cur, m_cur)
                back_block(kb, z_prev, m_prev)

            def pair(j, carry):
                both(2 * j)
                both(2 * j + 1)
                return carry

            lax.fori_loop(0, i // 2, pair, 0)
            pl.when(i % 2 == 1)(lambda: both(i - 1))
            m = jnp.max(m_cur[...], axis=-1, keepdims=True)
            m_cur[...] = jnp.broadcast_to(m, m_cur.shape)

        @pl.when(i == n_q)
        def _():
            def back_only(kb, carry):
                back_block(kb, z_prev, m_prev)
                return carry

            lax.fori_loop(0, i, back_only, 0)

    l_ref[...] = jnp.zeros_like(l_ref)
    acc_ref[...] = jnp.zeros_like(acc_ref)
    pl.when(i % 2 == 0)(functools.partial(step, za_ref, ma_ref, zb_ref, mb_ref))
    pl.when(i % 2 == 1)(functools.partial(step, zb_ref, mb_ref, za_ref, ma_ref))

    @pl.when(i > 0)
    def _():
        l = _unstack_cols(jnp.sum(l_ref[...], axis=-1, keepdims=True), TQC)
        o_ref[0] = (_unstack_heads(acc_ref[...], TQC) / l).astype(o_ref.dtype)


def _fox_attention(qkv, cum):
    _, b, s, _ = qkv.shape
    assert TQC == TK
    n_q = s // TQC
    rows = N_HEADS * TQC
    return pl.pallas_call(
        _fox_kernel,
        grid=(b, n_q + 1),
        in_specs=[
            pl.BlockSpec((1, 1, TQC, MIX_W), lambda bi, i: (0, bi, jnp.minimum(i, n_q - 1), 0)),
            pl.BlockSpec((1, 1, s, MIX_W), lambda bi, i: (1, bi, 0, 0)),
            pl.BlockSpec((1, 1, s, MIX_W), lambda bi, i: (2, bi, 0, 0)),
            pl.BlockSpec((1, 8, s), lambda bi, i: (bi, 0, 0)),
        ],
        out_specs=pl.BlockSpec((1, TQC, MIX_W), lambda bi, i: (bi, jnp.maximum(i - 1, 0), 0)),
        out_shape=jax.ShapeDtypeStruct((b, s, MIX_W), BF16),
        scratch_shapes=[
            pltpu.VMEM((s // TK, rows, TK), F32),
            pltpu.VMEM((s // TK, rows, TK), F32),
            pltpu.VMEM((rows, LANES), F32),
            pltpu.VMEM((rows, LANES), F32),
            pltpu.VMEM((rows, LANES), F32),
            pltpu.VMEM((rows, MIX_W), F32),
        ],
        compiler_params=_cparams("parallel", "arbitrary"),
        name="fox_attn",
    )(qkv, qkv, qkv, cum)


def _t5_bucket(dist):
    n = jnp.maximum(dist, 0)
    max_exact = NUM_BUCKETS // 2
    nf = jnp.maximum(n, 1).astype(F32)
    large = max_exact + (jnp.log(nf / max_exact) / math.log(MAX_DISTANCE / max_exact)
                         * (NUM_BUCKETS - max_exact)).astype(jnp.int32)
    large = jnp.minimum(large, NUM_BUCKETS - 1)
    return jnp.where(n < max_exact, n, large)


def _bias_kernel(idx_ref, rb_ref, o_ref):
    for d in range(len(DILATIONS)):
        for variant in range(2):
            idx = idx_ref[d, variant]
            for h in range(N_HEADS):
                acc = jnp.full(idx.shape, NEG_BIG, F32)
                for bucket in range(NUM_BUCKETS):
                    acc = jnp.where(idx == bucket, rb_ref[bucket, h], acc)
                o_ref[d, variant, h] = acc


def _bias_tables(rel_bias):
    t = jnp.arange(TQ)[:, None]
    c = jnp.arange(2 * TQ)[None, :]
    delta = TQ + t - c
    valid = (delta >= 0) & (delta <= DIL_STEPS)
    idx = jnp.stack([jnp.where(valid, _t5_bucket(delta * dil), -1) for dil in DILATIONS])
    idx_first = jnp.concatenate([idx[:, :, TQ:], jnp.full_like(idx[:, :, TQ:], -1)], axis=-1)
    idx = jnp.stack([idx, idx_first], axis=1)
    out = pl.pallas_call(
        _bias_kernel,
        in_specs=[
            pl.BlockSpec(memory_space=pltpu.VMEM),
            pl.BlockSpec(memory_space=pltpu.SMEM),
        ],
        out_specs=pl.BlockSpec(memory_space=pltpu.VMEM),
        out_shape=jax.ShapeDtypeStruct((len(DILATIONS), 2, N_HEADS, TQ, 2 * TQ), F32),
        name="dil_bias",
    )(idx.astype(jnp.int32), rel_bias)
    return out.reshape(len(DILATIONS), 2, N_HEADS * TQ, 2 * TQ)


DIL_GROUP = 8


def _dil_kernel(q_ref, k_ref, v_ref, t_ref, o_ref, lse_ref, *, by_class, group, n_groups):
    g = pl.program_id(2)

    def attend(jj, scores, v_win):
        m = jnp.max(scores, axis=-1, keepdims=True)
        p = jnp.exp(scores - m)
        l = _unstack_cols(jnp.sum(p, axis=-1, keepdims=True), TQ)
        o = (_unstack_heads(_dot(p.astype(BF16), v_win), TQ) / l).astype(o_ref.dtype)
        lse = _unstack_cols(m, TQ) + jnp.log(l)
        if by_class:
            o_ref[0, jj] = o
            lse_ref[0, jj] = lse
        else:
            o_ref[0, 0, jj * TQ:(jj + 1) * TQ] = o
            lse_ref[0, 0, jj * TQ:(jj + 1) * TQ] = lse

    pending = []
    for jj in range(group):
        if by_class:
            scores = (_dot_nt(_stack_heads(q_ref[0, 0, jj]), k_ref[0, 0, jj])
                      + t_ref[0, 0, :, TQ:2 * TQ])
            pending.append((scores, v_ref[0, 0, jj]))
            continue
        qs = _stack_heads(q_ref[0, 0, 0, jj * TQ:(jj + 1) * TQ])
        if n_groups == 1 or jj > 0:
            n = g * group + jj if n_groups > 1 else jj
            start = (n - 1) * TQ if jj > 0 else 0
            bias = t_ref[0, 0 if jj > 0 else 1]
        else:
            start = jnp.maximum(g * group - 1, 0) * TQ
            bias = t_ref[0, (g == 0).astype(jnp.int32)]
        if not isinstance(start, int):
            start = pl.multiple_of(start, TQ)
        window = pl.ds(start, 2 * TQ)
        pending.append((_dot_nt(qs, k_ref[0, 0, 0, window, :]) + bias, v_ref[0, 0, 0, window, :]))
    for jj, (scores, v_win) in enumerate(pending):
        attend(jj, scores, v_win)


def _dil_branch(qkv, tables, branch):
    _, b, dil, cls_len, _ = qkv.shape
    n_blocks = cls_len // TQ
    by_class = n_blocks == 1
    group = min(DIL_GROUP, dil if by_class else n_blocks)
    if by_class:
        grid = (b, dil // group, 1)
        qkv_specs = [pl.BlockSpec((1, 1, group, TQ, MIX_W), functools.partial(
            lambda bi, c, g, j: (j, bi, c, 0, 0), j=j)) for j in range(3)]
        out_spec = pl.BlockSpec((1, group, TQ, MIX_W), lambda bi, c, g: (bi, c, 0, 0))
    else:
        grid = (b, dil, n_blocks // group)
        qkv_specs = [pl.BlockSpec((1, 1, 1, group * TQ, MIX_W), lambda bi, r, g: (0, bi, r, g, 0))]
        qkv_specs += [pl.BlockSpec((1, 1, 1, cls_len, MIX_W), functools.partial(
            lambda bi, r, g, j: (j, bi, r, 0, 0), j=j)) for j in (1, 2)]
        out_spec = pl.BlockSpec((1, 1, group * TQ, MIX_W), lambda bi, r, g: (bi, r, g, 0))
    o, lse = pl.pallas_call(
        functools.partial(_dil_kernel, by_class=by_class, group=group, n_groups=grid[2]),
        grid=grid,
        in_specs=qkv_specs + [
            pl.BlockSpec((1, 2, N_HEADS * TQ, 2 * TQ), lambda bi, r, g: (branch, 0, 0, 0))],
        out_specs=[out_spec, out_spec],
        out_shape=[jax.ShapeDtypeStruct((b, dil, cls_len, MIX_W), BF16),
                   jax.ShapeDtypeStruct((b, dil, cls_len, MIX_W), F32)],
        compiler_params=_cparams("parallel", "parallel", "arbitrary"),
        name=f"dil_attn_{dil}",
    )(qkv, qkv, qkv, tables)
    to_seq = lambda a: a.transpose(0, 2, 1, 3).reshape(b, dil * cls_len, MIX_W)
    return to_seq(o), to_seq(lse)


def _dil_mixture(outs, lses):
    m = functools.reduce(jnp.maximum, lses)
    es = [jnp.exp(a - m) for a in lses]
    tot = functools.reduce(jnp.add, es)
    return functools.reduce(jnp.add, [(e / tot) * o for e, o in zip(es, outs)])


def _shift_rows(x, k, fill):
    row = lax.broadcasted_iota(jnp.int32, x.shape, 0)
    return jnp.where(row >= k, pltpu.roll(x, k, 0), fill)


def _lru_kernel(xg_ref, cw_ref, cb_ref, wa_ref, ba_ref, wx_ref, bx_ref, lam_ref, o_ref, h_ref):
    x = xg_ref[0, 0]
    gate = xg_ref[1, 0]
    s = x.shape[0]
    kw = cw_ref.shape[0]
    xc = cb_ref[...] + cw_ref[kw - 1:kw, :] * x
    for j in range(kw - 1):
        xc = xc + cw_ref[j:j + 1, :] * _shift_rows(x, kw - 1 - j, 0.0)
    xcb = xc.astype(BF16)
    r = jax.nn.sigmoid(_dot(xcb, wa_ref[...]) + ba_ref[...])
    i_gate = jax.nn.sigmoid(_dot(xcb, wx_ref[...]) + bx_ref[...])
    a = jnp.exp(-LRU_C * r * _softplus(-lam_ref[...]))
    h = jnp.sqrt(1.0 - a * a) * (i_gate * xc)
    grouped = (s // SUBLANES, SUBLANES, x.shape[1])
    h, a = h.reshape(grouped), a.reshape(grouped)
    in_group = lax.broadcasted_iota(jnp.int32, grouped, 1)
    k = 1
    while k < SUBLANES:
        prev = in_group >= k
        h = a * jnp.where(prev, pltpu.roll(h, k, 1), 0.0) + h
        a = a * jnp.where(prev, pltpu.roll(a, k, 1), 1.0)
        k *= 2
    h, a = h.reshape(x.shape), a.reshape(x.shape)
    carry = jnp.zeros((1, x.shape[1]), F32)
    for c in range(s // SUBLANES):
        rows = slice(c * SUBLANES, (c + 1) * SUBLANES)
        hc = h[rows] + a[rows] * carry
        h_ref[rows] = hc
        carry = hc[SUBLANES - 1:SUBLANES]
    o_ref[0] = (h_ref[...] * _gelu(gate)).astype(o_ref.dtype)


def _lru(xg, conv_w, conv_b, wa_bd, b_a, wx_bd, b_x, lam):
    _, b, s, w = xg.shape
    full = lambda a: pl.BlockSpec(a.shape, lambda bi: (0,) * a.ndim)
    args = (conv_w, conv_b, wa_bd, b_a, wx_bd, b_x, lam)
    return pl.pallas_call(
        _lru_kernel,
        grid=(b,),
        in_specs=[pl.BlockSpec((2, 1, s, w), lambda bi: (0, bi, 0, 0))] + [full(a) for a in args],
        out_specs=pl.BlockSpec((1, s, w), lambda bi: (bi, 0, 0)),
        out_shape=jax.ShapeDtypeStruct((b, s, w), BF16),
        scratch_shapes=[pltpu.VMEM((s, w), F32)],
        compiler_params=_cparams("parallel"),
        name="rg_lru",
    )(xg, *args)


def _memkv_kernel(mem_ref, g_ref, wk_ref, wv_ref, k_ref, v_ref):
    mn = _rms(mem_ref[0], g_ref[...]).astype(BF16)
    k_ref[0] = _dot(mn, wk_ref[...]).astype(BF16)
    v_ref[0] = _dot(mn, wv_ref[...]).astype(BF16)


def _memkv(mem, g, w_ck, w_cv):
    b, m, d = mem.shape
    out = jax.ShapeDtypeStruct((b, m, MIX_W), BF16)
    return pl.pallas_call(
        _memkv_kernel,
        grid=(b,),
        in_specs=[
            pl.BlockSpec((1, m, d), lambda bi: (bi, 0, 0)),
            pl.BlockSpec((1, d), lambda bi: (0, 0)),
            pl.BlockSpec((d, MIX_W), lambda bi: (0, 0)),
            pl.BlockSpec((d, MIX_W), lambda bi: (0, 0)),
        ],
        out_specs=[pl.BlockSpec((1, m, MIX_W), lambda bi: (bi, 0, 0))] * 2,
        out_shape=[out, out],
        compiler_params=_cparams("parallel"),
        name="mem_kv",
    )(mem, g, w_ck, w_cv)


def _mix_cross_kernel(x_ref, sb_ref, fox_ref, lru_ref, o1_ref, l1_ref, o4_ref, l4_ref,
                      o16_ref, l16_ref, wo_ref, g_ref, wq_ref, km_ref, vm_ref, wco_ref, o_ref):
    t = MIX_SUB
    subs = [slice(u * t, (u + 1) * t) for u in range(x_ref.shape[1] // t)]
    xs = []
    for r in subs:
        dil = _dil_mixture([ref[0, r] for ref in (o1_ref, o4_ref, o16_ref)],
                           [ref[0, r] for ref in (l1_ref, l4_ref, l16_ref)]).astype(BF16)
        x = x_ref[0, r]
        for j, mixed in enumerate((sb_ref[0, r], fox_ref[0, r], dil, lru_ref[0, r])):
            x = x + _dot(mixed, wo_ref[j])
        xs.append(x)
    qs = [_stack_heads(_dot(_rms(x, g_ref[...]).astype(BF16), wq_ref[...])) for x in xs]
    scores = [_dot_nt(q, km_ref[0]) for q in qs]
    outs = []
    for sc in scores:
        p = jnp.exp(sc - jnp.max(sc, axis=-1, keepdims=True))
        p = p / jnp.sum(p, axis=-1, keepdims=True)
        outs.append(_unstack_heads(_dot(p.astype(BF16), vm_ref[0]), t))
    for r, x, o in zip(subs, xs, outs):
        o_ref[0, r] = x + _dot(o.astype(BF16), wco_ref[...])


MIX_SUB = 128


def _mix_cross(x, o_sb, o_fox, o_lru, dil_branches, w_out, g, w_cq, k_mem, v_mem, w_co, ts=512):
    b, s, d = x.shape
    m = k_mem.shape[1]
    mix_spec = pl.BlockSpec((1, ts, MIX_W), lambda bi, i: (bi, i, 0))
    const = lambda a: pl.BlockSpec(a.shape, lambda bi, i: (0,) * a.ndim)
    mem_spec = pl.BlockSpec((1, m, MIX_W), lambda bi, i: (bi, 0, 0))
    dil_flat = [a for pair in dil_branches for a in pair]
    return pl.pallas_call(
        _mix_cross_kernel,
        grid=(b, s // ts),
        in_specs=[pl.BlockSpec((1, ts, d), lambda bi, i: (bi, i, 0))] + [mix_spec] * (3 + len(dil_flat))
                 + [const(w_out), const(g), const(w_cq), mem_spec, mem_spec, const(w_co)],
        out_specs=pl.BlockSpec((1, ts, d), lambda bi, i: (bi, i, 0)),
        out_shape=jax.ShapeDtypeStruct((b, s, d), F32),
        compiler_params=_cparams("parallel", "parallel"),
        name="mix_cross",
    )(x, o_sb, o_fox, o_lru, *dil_flat, w_out, g, w_cq, k_mem, v_mem, w_co)


HALO = 8
FFN_DOWN_LAG = 2
FFN_ACT_SLOTS = FFN_DOWN_LAG + 1


def _ffn_kernel(x_ref, g_ref, wup_ref, cw_ref, cb_ref, wdn_ref, gf_ref, o_ref,
                hn_ref, ext_ref, tail_ref, act_ref, acc_ref, *, final_norm):
    i = pl.program_id(1)
    x = x_ref[0]
    t = x.shape[0]
    hn_ref[...] = _rms(x, g_ref[...]).astype(BF16)
    acc_ref[...] = jnp.zeros_like(acc_ref)

    @pl.when(i == 0)
    def _():
        tail_ref[...] = jnp.zeros_like(tail_ref)

    def project(c):
        for half in range(2):
            blk = half * N_FF_CHUNKS + c
            slot = 2 * (c % 2) + half
            pre = _dot(hn_ref[...], wup_ref[:, blk * FF_CHUNK:(blk + 1) * FF_CHUNK])
            ext_ref[slot, 0:HALO] = tail_ref[blk]
            ext_ref[slot, HALO:] = pre
            tail_ref[blk] = pre[t - HALO:]

    def conv(c, slot):
        ext = ext_ref.at[slot]
        w = cw_ref[c]
        return (cb_ref[c] + w[2:3] * ext[pl.ds(HALO, t), :]
                + w[1:2] * ext[pl.ds(HALO - 1, t), :]
                + w[0:1] * ext[pl.ds(HALO - 2, t), :])

    def down(c):
        acc_ref[...] += _dot(act_ref[c % FFN_ACT_SLOTS], wdn_ref[c])

    project(0)
    for c in range(N_FF_CHUNKS):
        if c + 1 < N_FF_CHUNKS:
            project(c + 1)
        if c >= FFN_DOWN_LAG:
            down(c - FFN_DOWN_LAG)
        up = conv(c, 2 * (c % 2))
        gate = conv(N_FF_CHUNKS + c, 2 * (c % 2) + 1)
        act_ref[c % FFN_ACT_SLOTS] = (_gelu(gate) * up).astype(BF16)
    for c in range(N_FF_CHUNKS - FFN_DOWN_LAG, N_FF_CHUNKS):
        down(c)
    out = x + acc_ref[...]
    if final_norm:
        out = _rms(out, gf_ref[...])
    o_ref[0] = out


def _ffn(x, g, w_up, conv_w, conv_b, w_down, g_final, final_norm, ts=256):
    b, s, d = x.shape
    const = lambda a: pl.BlockSpec(a.shape, lambda bi, i: (0,) * a.ndim)
    return pl.pallas_call(
        functools.partial(_ffn_kernel, final_norm=final_norm),
        grid=(b, s // ts),
        in_specs=[
            pl.BlockSpec((1, ts, d), lambda bi, i: (bi, i, 0)),
            const(g), const(w_up), const(conv_w), const(conv_b), const(w_down), const(g_final),
        ],
        out_specs=pl.BlockSpec((1, ts, d), lambda bi, i: (bi, i, 0)),
        out_shape=jax.ShapeDtypeStruct((b, s, d), F32),
        scratch_shapes=[
            pltpu.VMEM((ts, d), BF16),
            pltpu.VMEM((4, HALO + ts, FF_CHUNK), F32),
            pltpu.VMEM((2 * N_FF_CHUNKS, HALO, FF_CHUNK), F32),
            pltpu.VMEM((FFN_ACT_SLOTS, ts, FF_CHUNK), BF16),
            pltpu.VMEM((ts, d), F32),
        ],
        compiler_params=_cparams("parallel", "arbitrary"),
        name="conv_geglu",
    )(x, g, w_up, conv_w, conv_b, w_down, g_final)


def _block_diag(w):
    g, n, _ = w.shape
    eye = jnp.eye(g, dtype=w.dtype)
    return (eye[:, None, :, None] * w[:, :, None, :]).reshape(g * n, g * n)


def kernel(x, mem, norm_mix_g, w_in, b_forget, lru_conv_w, lru_conv_b, lru_w_a, lru_b_a, lru_w_x, lru_b_x, lru_lambda, w_out, norm_cross_g, norm_mem_g, w_cq, w_ck, w_cv, w_co, norm_ffn_g, w_up, ffn_conv_w, ffn_conv_b, w_down, rel_bias, final_norm_g):
    depth = w_in.shape[0]
    qkv_w = 3 * MIX_W
    q_scale = HEAD_DIM ** -0.5
    col_scale = jnp.ones((3, 3, MIX_W), F32).at[:, 0, :].set(q_scale).reshape(-1)
    col_scale = jnp.concatenate([col_scale, jnp.ones((2 * MIX_W,), F32)])
    tables = _bias_tables(rel_bias)
    row = lambda v: v.reshape(1, -1)

    for l in range(depth):
        w = w_in[l]
        n_f = N_HEADS
        w_main = jnp.concatenate([w[:, :2 * qkv_w], w[:, 2 * qkv_w + n_f:]], axis=1)
        w_main = (w_main * col_scale).astype(BF16)
        w_f = jnp.zeros((8, D_MODEL), F32).at[:n_f].set(w[:, 2 * qkv_w:2 * qkv_w + n_f].T).astype(BF16)
        b_f = jnp.zeros((8, 1), F32).at[:n_f, 0].set(b_forget[l])

        sb_qkv, fox_qkv, *dil_qkvs, lru_xg, f_t = _inproj(x, row(norm_mix_g[l]), w_main, w_f)
        o_sb = _sb_attention(sb_qkv)
        o_fox = _fox_attention(fox_qkv, _decay(f_t, b_f))
        dil_branches = [_dil_branch(qkv, tables, br) for br, qkv in enumerate(dil_qkvs)]
        o_lru = _lru(lru_xg, lru_conv_w[l], row(lru_conv_b[l]),
                     _block_diag(lru_w_a[l]).astype(BF16), row(lru_b_a[l]),
                     _block_diag(lru_w_x[l]).astype(BF16), row(lru_b_x[l]), row(lru_lambda[l]))

        k_mem, v_mem = _memkv(mem, row(norm_mem_g[l]), w_ck[l].astype(BF16), w_cv[l].astype(BF16))
        x = _mix_cross(x, o_sb, o_fox, o_lru, dil_branches,
                       w_out[l].reshape(4, MIX_W, D_MODEL).astype(BF16), row(norm_cross_g[l]),
                       (w_cq[l] * q_scale).astype(BF16), k_mem, v_mem, w_co[l].astype(BF16))

        wu = w_up[l].astype(BF16)
        cw = ffn_conv_w[l].reshape(-1, 2 * N_FF_CHUNKS, FF_CHUNK).transpose(1, 0, 2)
        cb = ffn_conv_b[l].reshape(2 * N_FF_CHUNKS, 1, FF_CHUNK)
        wd = w_down[l].reshape(N_FF_CHUNKS, FF_CHUNK, D_MODEL).astype(BF16)
        x = _ffn(x, row(norm_ffn_g[l]), wu, cw, cb, wd, row(final_norm_g), l == depth - 1)
    return x
```

```python
import functools
import math

import jax
import jax.numpy as jnp
from jax import lax
from jax.experimental import pallas as pl
from jax.experimental.pallas import tpu as pltpu

F32 = jnp.float32
BF16 = jnp.bfloat16

LANES = 128
SUBLANES = 8
D_MODEL = 1024
HEAD_DIM = 64
N_HEADS = 4
MIX_W = N_HEADS * HEAD_DIM
D_FF = 2816
FF_CHUNK = 256
N_FF_CHUNKS = D_FF // FF_CHUNK
N_MEM = 256
NUM_BUCKETS = 32
MAX_DISTANCE = 2048
DILATIONS = (1, 4, 16)
DIL_STEPS = 128
LRU_C = 8.0
EPS = 1e-6
NEG_BIG = -1e30

TQ = 128
TQC = 256
TK = 256
VMEM_LIMIT = 56 * 1024 * 1024

NT_DIMS = (((1,), (1,)), ((), ()))


def _cparams(*sem):
    return pltpu.CompilerParams(dimension_semantics=sem, vmem_limit_bytes=VMEM_LIMIT)


def _rms(x, g):
    return x * lax.rsqrt(jnp.mean(x * x, axis=-1, keepdims=True) + EPS) * g


def _softplus(x):
    return jnp.maximum(x, 0.0) + jnp.log(1.0 + jnp.exp(-jnp.abs(x)))


def _gelu(x):
    return 0.5 * x * (1.0 + lax.erf(x * math.sqrt(0.5)))


def _dot(a, b):
    return jnp.dot(a, b, preferred_element_type=F32)


def _dot_nt(a, b):
    return lax.dot_general(a, b, NT_DIMS, preferred_element_type=F32)


def _head_of_lane(rows):
    return lax.broadcasted_iota(jnp.int32, (rows, MIX_W), 1) // HEAD_DIM


def _stack_heads(q):
    t = q.shape[0]
    head = _head_of_lane(t)
    qf = q.astype(F32)
    return jnp.concatenate([jnp.where(head == h, qf, 0.0) for h in range(N_HEADS)], axis=0).astype(BF16)


def _unstack_heads(o, t):
    head = _head_of_lane(t)
    out = jnp.where(head == 0, o[0:t], 0.0)
    for h in range(1, N_HEADS):
        out = jnp.where(head == h, o[h * t:(h + 1) * t], out)
    return out


def _unstack_cols(c, t):
    head = _head_of_lane(t)
    out = jnp.broadcast_to(c[0:t], (t, MIX_W))
    for h in range(1, N_HEADS):
        out = jnp.where(head == h, jnp.broadcast_to(c[h * t:(h + 1) * t], (t, MIX_W)), out)
    return out


def _inproj_kernel(x_ref, g_ref, w_ref, wf_ref, sb_ref, fox_ref, d1_ref, d4_ref, d16_ref,
                   lru_ref, f_ref, hn_ref, half_ref):
    hn_ref[...] = _rms(x_ref[0], g_ref[...]).astype(BF16)
    ts = hn_ref.shape[0]

    def project(rows, block):
        return _dot(hn_ref[rows], w_ref[:, block * MIX_W:(block + 1) * MIX_W])

    for j in range(3):
        res = project(slice(None), 6 + j)
        d1_ref[j, 0, 0] = res.astype(BF16)
        for h in range(2):
            half_ref[j, h] = res[:, h * LANES:(h + 1) * LANES]

    def regroup(j, ref, dil):
        for r in range(dil):
            for h in range(2):
                ref[j, 0, r, :, h * LANES:(h + 1) * LANES] = (
                    half_ref[j, h, pl.ds(r, ts // dil, stride=dil), :].astype(BF16))

    pending = [(j, ref, dil) for j in range(3)
               for ref, dil in ((d4_ref, DILATIONS[1]), (d16_ref, DILATIONS[2]))]
    for m, ref in enumerate((sb_ref, fox_ref)):
        for j in range(3):
            ref[j, 0] = project(slice(None), 3 * m + j).astype(BF16)
            regroup(*pending.pop(0))
    lru_ref[0, 0] = project(slice(None), 9)
    for rows in (slice(0, ts // 2), slice(ts // 2, ts)):
        lru_ref[1, 0, rows] = project(rows, 10)
    f_ref[0] = _dot_nt(wf_ref[...], hn_ref[...])


def _inproj(x, g, w_main, w_f, ts=512):
    b, s, d = x.shape
    qkv_shape = jax.ShapeDtypeStruct((3, b, s, MIX_W), BF16)
    qkv_spec = pl.BlockSpec((3, 1, ts, MIX_W), lambda bi, i: (0, bi, i, 0))
    cls_shapes = [jax.ShapeDtypeStruct((3, b, dil, s // dil, MIX_W), BF16) for dil in DILATIONS]
    cls_specs = [pl.BlockSpec((3, 1, dil, ts // dil, MIX_W), lambda bi, i: (0, bi, 0, i, 0))
                 for dil in DILATIONS]
    return pl.pallas_call(
        _inproj_kernel,
        grid=(b, s // ts),
        in_specs=[
            pl.BlockSpec((1, ts, d), lambda bi, i: (bi, i, 0)),
            pl.BlockSpec((1, d), lambda bi, i: (0, 0)),
            pl.BlockSpec(w_main.shape, lambda bi, i: (0, 0)),
            pl.BlockSpec(w_f.shape, lambda bi, i: (0, 0)),
        ],
        out_specs=[
            qkv_spec, qkv_spec, *cls_specs,
            pl.BlockSpec((2, 1, ts, MIX_W), lambda bi, i: (0, bi, i, 0)),
            pl.BlockSpec((1, 8, ts), lambda bi, i: (bi, 0, i)),
        ],
        out_shape=[
            qkv_shape, qkv_shape, *cls_shapes,
            jax.ShapeDtypeStruct((2, b, s, MIX_W), F32),
            jax.ShapeDtypeStruct((b, 8, s), F32),
        ],
        scratch_shapes=[pltpu.VMEM((ts, d), BF16), pltpu.VMEM((3, 2, ts, LANES), F32)],
        compiler_params=_cparams("parallel", "parallel"),
        name="inproj",
    )(x, g, w_main, w_f)


def _decay_kernel(f_ref, bf_ref, tri_ref, o_ref):
    s = f_ref.shape[2]
    tri = tri_ref[...]
    carry = jnp.zeros((8, 1), F32)
    for c in range(s // 128):
        sl = slice(c * 128, (c + 1) * 128)
        log_f = -_softplus(-(f_ref[0, :, sl] + bf_ref[...]))
        p1 = log_f.astype(BF16)
        r1 = log_f - p1.astype(F32)
        p2 = r1.astype(BF16)
        p3 = (r1 - p2.astype(F32)).astype(BF16)
        inc = _dot(p1, tri) + _dot(p2, tri) + _dot(p3, tri)
        o_ref[0, :, sl] = inc + carry
        carry = carry + jnp.sum(log_f, axis=-1, keepdims=True)


def _decay(f_t, b_f):
    b, _, s = f_t.shape
    tri = (lax.broadcasted_iota(jnp.int32, (128, 128), 0)
           <= lax.broadcasted_iota(jnp.int32, (128, 128), 1)).astype(BF16)
    return pl.pallas_call(
        _decay_kernel,
        grid=(b,),
        in_specs=[
            pl.BlockSpec((1, 8, s), lambda bi: (bi, 0, 0)),
            pl.BlockSpec((8, 1), lambda bi: (0, 0)),
            pl.BlockSpec((128, 128), lambda bi: (0, 0)),
        ],
        out_specs=pl.BlockSpec((1, 8, s), lambda bi: (bi, 0, 0)),
        out_shape=jax.ShapeDtypeStruct((b, 8, s), F32),
        compiler_params=_cparams("parallel"),
        name="fox_decay",
    )(f_t, b_f, tri)


def _block_positions(i, ks):
    row = lax.broadcasted_iota(jnp.int32, (N_HEADS * TQC, TK), 0) % TQC + i * TQC
    col = lax.broadcasted_iota(jnp.int32, (N_HEADS * TQC, TK), 1) + ks
    return row, col


SB_SATURATED = 110.0


def _sb_kernel(q_ref, k_ref, v_ref, uo_ref, o_ref, acc_ref, car_ref):
    i = pl.program_id(1)
    qs = _stack_heads(q_ref[0, 0])

    half_rows = N_HEADS * TQC // 2
    halves = [slice(0, half_rows), slice(half_rows, 2 * half_rows)]

    def process(blocks):
        items = []
        for kb, diagonal in blocks:
            ks = pl.multiple_of(kb * TK, TK)
            k_blk = k_ref[0, 0, pl.ds(ks, TK), :]
            v_blk = v_ref[0, 0, pl.ds(ks, TK), :]
            strict = None
            if diagonal:
                row, col = _block_positions(i, ks)
                strict = (col < row)[:half_rows]
            items += [(r, k_blk, v_blk, strict) for r in halves]
        zs = [_dot_nt(qs[r], k_blk) for r, k_blk, _, _ in items]
        staged = []
        for (r, _, _, strict), z in zip(items, zs):
            drop = _softplus(z)
            cum_in = drop if strict is None else jnp.where(strict, drop, 0.0)
            staged.append((drop, _dot(cum_in.astype(BF16), uo_ref[...])))
        for (r, _, v_blk, strict), z, (drop, sm) in zip(items, zs, staged):
            if strict is not None:
                att = jnp.where(strict, jnp.exp(z - drop - sm[:, :TK]), 0.0)
                car_ref[r] = sm[:, TK:]
                acc_ref[r] = _dot(att.astype(BF16), v_blk)
            else:
                car = car_ref[r]
                att = jnp.exp(z - drop - sm[:, :TK] - car)
                car_ref[r] = car + sm[:, TK:]
                acc_ref[r] += _dot(att.astype(BF16), v_blk)

    def live():
        return jnp.min(car_ref[:, :LANES]) < SB_SATURATED

    pl.when(i == 0)(lambda: process([(i, True)]))
    pl.when(i > 0)(lambda: process([(i, True), (i - 1, False)]))

    def cond(state):
        kb, go = state
        return jnp.logical_and(kb >= 0, go)

    def body(state):
        kb, _ = state
        process([(kb, False)])
        return kb - 1, live()

    lax.while_loop(cond, body, (i - 2, live()))
    o_ref[0] = _unstack_heads(acc_ref[...], TQC).astype(o_ref.dtype)


def _sb_attention(qkv):
    _, b, s, _ = qkv.shape
    assert TQC == TK
    later =(lax.broadcasted_iota(jnp.int32, (TK, TK), 0)
             > lax.broadcasted_iota(jnp.int32, (TK, TK), 1))
    u = jnp.concatenate([later, jnp.ones((TK, TK), bool)], axis=1).astype(BF16)
    return pl.pallas_call(
        _sb_kernel,
        grid=(b, s // TQC),
        in_specs=[
            pl.BlockSpec((1, 1, TQC, MIX_W), lambda bi, i: (0, bi, i, 0)),
            pl.BlockSpec((1, 1, s, MIX_W), lambda bi, i: (1, bi, 0, 0)),
            pl.BlockSpec((1, 1, s, MIX_W), lambda bi, i: (2, bi, 0, 0)),
            pl.BlockSpec((TK, 2 * TK), lambda bi, i: (0, 0)),
        ],
        out_specs=pl.BlockSpec((1, TQC, MIX_W), lambda bi, i: (bi, i, 0)),
        out_shape=jax.ShapeDtypeStruct((b, s, MIX_W), BF16),
        scratch_shapes=[
            pltpu.VMEM((N_HEADS * TQC, MIX_W), F32),
            pltpu.VMEM((N_HEADS * TQC, TK), F32),
        ],
        compiler_params=_cparams("parallel", "arbitrary"),
        name="sb_attn",
    )(qkv, qkv, qkv, u)


def _fox_kernel(q_ref, k_ref, v_ref, cum_ref, o_ref, za_ref, zb_ref, ma_ref, mb_ref, l_ref, acc_ref):
    i = pl.program_id(1)
    n_q = pl.num_programs(1) - 1

    def front_block(qs, kb, diagonal, z_ref, mx_ref):
        ks = pl.multiple_of(kb * TK, TK)
        cum = cum_ref[0, :, pl.ds(ks, TK)]
        decay = jnp.concatenate(
            [jnp.broadcast_to(cum[h:h + 1], (TQC, TK)) for h in range(N_HEADS)], axis=0)
        z = _dot_nt(qs, k_ref[0, 0, pl.ds(ks, TK), :]) - decay
        if diagonal:
            row, col = _block_positions(i, ks)
            z = jnp.where(col <= row, z, NEG_BIG)
        z_ref[kb] = z
        zmax = jnp.maximum(z[:, :LANES], z[:, LANES:])
        mx_ref[...] = zmax if diagonal else jnp.maximum(mx_ref[...], zmax)

    def back_block(kb, z_ref, mx_ref):
        ks = pl.multiple_of(kb * TK, TK)
        m = mx_ref[...]
        p = jnp.exp(z_ref[kb] - jnp.concatenate([m, m], axis=1))
        l_ref[...] += p[:, :LANES] + p[:, LANES:]
        acc_ref[...] += _dot(p.astype(BF16), v_ref[0, 0, pl.ds(ks, TK), :])

    def step(z_cur, m_cur, z_prev, m_prev):
        @pl.when(i < n_q)
        def _():
            qs = _stack_heads(q_ref[0, 0])
            front_block(qs, i, True, z_cur, m_cur)

            def both(kb):
                front_block(qs, kb, False, z_cur, m_cur)
                back_block(kb, z_prev, m_prev)

            def pair(j, carry):
                both(2 * j)
                both(2 * j + 1)
                return carry

            lax.fori_loop(0, i // 2, pair, 0)
            pl.when(i % 2 == 1)(lambda: both(i - 1))
            m = jnp.max(m_cur[...], axis=-1, keepdims=True)
            m_cur[...] = jnp.broadcast_to(m, m_cur.shape)

        @pl.when(i == n_q)
        def _():
            def back_only(kb, carry):
                back_block(kb, z_prev, m_prev)
                return carry

            lax.fori_loop(0, i, back_only, 0)

    l_ref[...] = jnp.zeros_like(l_ref)
    acc_ref[...] = jnp.zeros_like(acc_ref)
    pl.when(i % 2 == 0)(functools.partial(step, za_ref, ma_ref, zb_ref, mb_ref))
    pl.when(i % 2 == 1)(functools.partial(step, zb_ref, mb_ref, za_ref, ma_ref))

    @pl.when(i > 0)
    def _():
        l = _unstack_cols(jnp.sum(l_ref[...], axis=-1, keepdims=True), TQC)
        o_ref[0] = (_unstack_heads(acc_ref[...], TQC) / l).astype(o_ref.dtype)


def _fox_attention(qkv, cum):
    _, b, s, _ = qkv.shape
    assert TQC == TK
    n_q = s // TQC
    rows = N_HEADS * TQC
    return pl.pallas_call(
        _fox_kernel,
        grid=(b, n_q + 1),
        in_specs=[
            pl.BlockSpec((1, 1, TQC, MIX_W), lambda bi, i: (0, bi, jnp.minimum(i, n_q - 1), 0)),
            pl.BlockSpec((1, 1, s, MIX_W), lambda bi, i: (1, bi, 0, 0)),
            pl.BlockSpec((1, 1, s, MIX_W), lambda bi, i: (2, bi, 0, 0)),
            pl.BlockSpec((1, 8, s), lambda bi, i: (bi, 0, 0)),
        ],
        out_specs=pl.BlockSpec((1, TQC, MIX_W), lambda bi, i: (bi, jnp.maximum(i - 1, 0), 0)),
        out_shape=jax.ShapeDtypeStruct((b, s, MIX_W), BF16),
        scratch_shapes=[
            pltpu.VMEM((s // TK, rows, TK), F32),
            pltpu.VMEM((s // TK, rows, TK), F32),
            pltpu.VMEM((rows, LANES), F32),
            pltpu.VMEM((rows, LANES), F32),
            pltpu.VMEM((rows, LANES), F32),
            pltpu.VMEM((rows, MIX_W), F32),
        ],
        compiler_params=_cparams("parallel", "arbitrary"),
        name="fox_attn",
    )(qkv, qkv, qkv, cum)


def _t5_bucket(dist):
    n = jnp.maximum(dist, 0)
    max_exact = NUM_BUCKETS // 2
    nf = jnp.maximum(n, 1).astype(F32)
    large = max_exact + jnp.floor(jnp.log(nf / max_exact) / math.log(MAX_DISTANCE / max_exact)
                                  * (NUM_BUCKETS - max_exact)).astype(jnp.int32)
    large = jnp.minimum(large, NUM_BUCKETS - 1)
    return jnp.where(n < max_exact, n, large)


def _bias_kernel(idx_ref, rb_ref, o_ref):
    for d in range(len(DILATIONS)):
        for variant in range(2):
            idx = idx_ref[d, variant]
            for h in range(N_HEADS):
                acc = jnp.full(idx.shape, NEG_BIG, F32)
                for bucket in range(NUM_BUCKETS):
                    acc = jnp.where(idx == bucket, rb_ref[bucket, h], acc)
                o_ref[d, variant, h] = acc


def _bias_tables(rel_bias):
    t = jnp.arange(TQ)[:, None]
    c = jnp.arange(2 * TQ)[None, :]
    delta = TQ + t - c
    valid = (delta >= 0) & (delta <= DIL_STEPS)
    idx = jnp.stack([jnp.where(valid, _t5_bucket(delta * dil), -1) for dil in DILATIONS])
    idx_first = jnp.concatenate([idx[:, :, TQ:], jnp.full_like(idx[:, :, TQ:], -1)], axis=-1)
    idx = jnp.stack([idx, idx_first], axis=1)
    out = pl.pallas_call(
        _bias_kernel,
        in_specs=[
            pl.BlockSpec(memory_space=pltpu.VMEM),
            pl.BlockSpec(memory_space=pltpu.SMEM),
        ],
        out_specs=pl.BlockSpec(memory_space=pltpu.VMEM),
        out_shape=jax.ShapeDtypeStruct((len(DILATIONS), 2, N_HEADS, TQ, 2 * TQ), F32),
        name="dil_bias",
    )(idx.astype(jnp.int32), rel_bias)
    return out.reshape(len(DILATIONS), 2, N_HEADS * TQ, 2 * TQ)


DIL_GROUP = 8


def _dil_kernel(q_ref, k_ref, v_ref, t_ref, o_ref, lse_ref, *, by_class, group, n_groups):
    g = pl.program_id(2)

    def attend(jj, scores, v_win):
        m = jnp.max(scores, axis=-1, keepdims=True)
        p = jnp.exp(scores - m)
        l = _unstack_cols(jnp.sum(p, axis=-1, keepdims=True), TQ)
        o = (_unstack_heads(_dot(p.astype(BF16), v_win), TQ) / l).astype(o_ref.dtype)
        lse = _unstack_cols(m, TQ) + jnp.log(l)
        if by_class:
            o_ref[0, jj] = o
            lse_ref[0, jj] = lse
        else:
            o_ref[0, 0, jj * TQ:(jj + 1) * TQ] = o
            lse_ref[0, 0, jj * TQ:(jj + 1) * TQ] = lse

    pending = []
    for jj in range(group):
        if by_class:
            scores = (_dot_nt(_stack_heads(q_ref[0, 0, jj]), k_ref[0, 0, jj])
                      + t_ref[0, 0, :, TQ:2 * TQ])
            pending.append((scores, v_ref[0, 0, jj]))
            continue
        qs = _stack_heads(q_ref[0, 0, 0, jj * TQ:(jj + 1) * TQ])
        if n_groups == 1 or jj > 0:
            n = g * group + jj if n_groups > 1 else jj
            start = (n - 1) * TQ if jj > 0 else 0
            bias = t_ref[0, 0 if jj > 0 else 1]
        else:
            start = jnp.maximum(g * group - 1, 0) * TQ
            bias = t_ref[0, (g == 0).astype(jnp.int32)]
        if not isinstance(start, int):
            start = pl.multiple_of(start, TQ)
        window = pl.ds(start, 2 * TQ)
        pending.append((_dot_nt(qs, k_ref[0, 0, 0, window, :]) + bias, v_ref[0, 0, 0, window, :]))
    for jj, (scores, v_win) in enumerate(pending):
        attend(jj, scores, v_win)


def _dil_branch(qkv, tables, branch):
    _, b, dil, cls_len, _ = qkv.shape
    n_blocks = cls_len // TQ
    by_class = n_blocks == 1
    group = min(DIL_GROUP, dil if by_class else n_blocks)
    if by_class:
        grid = (b, dil // group, 1)
        qkv_specs = [pl.BlockSpec((1, 1, group, TQ, MIX_W), functools.partial(
            lambda bi, c, g, j: (j, bi, c, 0, 0), j=j)) for j in range(3)]
        out_spec = pl.BlockSpec((1, group, TQ, MIX_W), lambda bi, c, g: (bi, c, 0, 0))
    else:
        grid = (b, dil, n_blocks // group)
        qkv_specs = [pl.BlockSpec((1, 1, 1, group * TQ, MIX_W), lambda bi, r, g: (0, bi, r, g, 0))]
        qkv_specs += [pl.BlockSpec((1, 1, 1, cls_len, MIX_W), functools.partial(
            lambda bi, r, g, j: (j, bi, r, 0, 0), j=j)) for j in (1, 2)]
        out_spec = pl.BlockSpec((1, 1, group * TQ, MIX_W), lambda bi, r, g: (bi, r, g, 0))
    o, lse = pl.pallas_call(
        functools.partial(_dil_kernel, by_class=by_class, group=group, n_groups=grid[2]),
        grid=grid,
        in_specs=qkv_specs + [
            pl.BlockSpec((1, 2, N_HEADS * TQ, 2 * TQ), lambda bi, r, g: (branch, 0, 0, 0))],
        out_specs=[out_spec, out_spec],
        out_shape=[jax.ShapeDtypeStruct((b, dil, cls_len, MIX_W), BF16),
                   jax.ShapeDtypeStruct((b, dil, cls_len, MIX_W), F32)],
        compiler_params=_cparams("parallel", "parallel", "arbitrary"),
        name=f"dil_attn_{dil}",
    )(qkv, qkv, qkv, tables)
    to_seq = lambda a: a.transpose(0, 2, 1, 3).reshape(b, dil * cls_len, MIX_W)
    return to_seq(o), to_seq(lse)


def _dil_mixture(outs, lses):
    m = functools.reduce(jnp.maximum, lses)
    es = [jnp.exp(a - m) for a in lses]
    tot = functools.reduce(jnp.add, es)
    return functools.reduce(jnp.add, [(e / tot) * o for e, o in zip(es, outs)])


def _shift_rows(x, k, fill):
    row = lax.broadcasted_iota(jnp.int32, x.shape, 0)
    return jnp.where(row >= k, pltpu.roll(x, k, 0), fill)


def _lru_kernel(xg_ref, cw_ref, cb_ref, wa_ref, ba_ref, wx_ref, bx_ref, lam_ref, o_ref, h_ref):
    x = xg_ref[0, 0]
    gate = xg_ref[1, 0]
    s = x.shape[0]
    kw = cw_ref.shape[0]
    xc = cb_ref[...] + cw_ref[kw - 1:kw, :] * x
    for j in range(kw - 1):
        xc = xc + cw_ref[j:j + 1, :] * _shift_rows(x, kw - 1 - j, 0.0)
    xcb = xc.astype(BF16)
    r = jax.nn.sigmoid(_dot(xcb, wa_ref[...]) + ba_ref[...])
    i_gate = jax.nn.sigmoid(_dot(xcb, wx_ref[...]) + bx_ref[...])
    a = jnp.exp(-LRU_C * r * _softplus(-lam_ref[...]))
    h = jnp.sqrt(1.0 - a * a) * (i_gate * xc)
    grouped = (s // SUBLANES, SUBLANES, x.shape[1])
    h, a = h.reshape(grouped), a.reshape(grouped)
    in_group = lax.broadcasted_iota(jnp.int32, grouped, 1)
    k = 1
    while k < SUBLANES:
        prev = in_group >= k
        h = a * jnp.where(prev, pltpu.roll(h, k, 1), 0.0) + h
        a = a * jnp.where(prev, pltpu.roll(a, k, 1), 1.0)
        k *= 2
    h, a = h.reshape(x.shape), a.reshape(x.shape)
    carry = jnp.zeros((1, x.shape[1]), F32)
    for c in range(s // SUBLANES):
        rows = slice(c * SUBLANES, (c + 1) * SUBLANES)
        hc = h[rows] + a[rows] * carry
        h_ref[rows] = hc
        carry = hc[SUBLANES - 1:SUBLANES]
    o_ref[0] = (h_ref[...] * _gelu(gate)).astype(o_ref.dtype)


def _lru(xg, conv_w, conv_b, wa_bd, b_a, wx_bd, b_x, lam):
    _, b, s, w = xg.shape
    full = lambda a: pl.BlockSpec(a.shape, lambda bi: (0,) * a.ndim)
    args = (conv_w, conv_b, wa_bd, b_a, wx_bd, b_x, lam)
    return pl.pallas_call(
        _lru_kernel,
        grid=(b,),
        in_specs=[pl.BlockSpec((2, 1, s, w), lambda bi: (0, bi, 0, 0))] + [full(a) for a in args],
        out_specs=pl.BlockSpec((1, s, w), lambda bi: (bi, 0, 0)),
        out_shape=jax.ShapeDtypeStruct((b, s, w), BF16),
        scratch_shapes=[pltpu.VMEM((s, w), F32)],
        compiler_params=_cparams("parallel"),
        name="rg_lru",
    )(xg, *args)


def _memkv_kernel(mem_ref, g_ref, wk_ref, wv_ref, k_ref, v_ref):
    mn = _rms(mem_ref[0], g_ref[...]).astype(BF16)
    k_ref[0] = _dot(mn, wk_ref[...]).astype(BF16)
    v_ref[0] = _dot(mn, wv_ref[...]).astype(BF16)


def _memkv(mem, g, w_ck, w_cv):
    b, m, d = mem.shape
    out = jax.ShapeDtypeStruct((b, m, MIX_W), BF16)
    return pl.pallas_call(
        _memkv_kernel,
        grid=(b,),
        in_specs=[
            pl.BlockSpec((1, m, d), lambda bi: (bi, 0, 0)),
            pl.BlockSpec((1, d), lambda bi: (0, 0)),
            pl.BlockSpec((d, MIX_W), lambda bi: (0, 0)),
            pl.BlockSpec((d, MIX_W), lambda bi: (0, 0)),
        ],
        out_specs=[pl.BlockSpec((1, m, MIX_W), lambda bi: (bi, 0, 0))] * 2,
        out_shape=[out, out],
        compiler_params=_cparams("parallel"),
        name="mem_kv",
    )(mem, g, w_ck, w_cv)


def _mix_cross_kernel(x_ref, sb_ref, fox_ref, lru_ref, o1_ref, l1_ref, o4_ref, l4_ref,
                      o16_ref, l16_ref, wo_ref, g_ref, wq_ref, km_ref, vm_ref, wco_ref, o_ref):
    t = MIX_SUB
    subs = [slice(u * t, (u + 1) * t) for u in range(x_ref.shape[1] // t)]
    xs = []
    for r in subs:
        dil = _dil_mixture([ref[0, r] for ref in (o1_ref, o4_ref, o16_ref)],
                           [ref[0, r] for ref in (l1_ref, l4_ref, l16_ref)]).astype(BF16)
        mixed = jnp.concatenate([sb_ref[0, r], fox_ref[0, r], dil, lru_ref[0, r]], axis=1)
        xs.append(x_ref[0, r] + _dot(mixed, wo_ref[...]))
    qs = [_stack_heads(_dot(_rms(x, g_ref[...]).astype(BF16), wq_ref[...])) for x in xs]
    scores = [_dot_nt(q, km_ref[0]) for q in qs]
    outs = []
    for sc in scores:
        p = jnp.exp(sc - jnp.max(sc, axis=-1, keepdims=True))
        p = p / jnp.sum(p, axis=-1, keepdims=True)
        outs.append(_unstack_heads(_dot(p.astype(BF16), vm_ref[0]), t))
    for r, x, o in zip(subs, xs, outs):
        o_ref[0, r] = x + _dot(o.astype(BF16), wco_ref[...])


MIX_SUB = 128


def _mix_cross(x, o_sb, o_fox, o_lru, dil_branches, w_out, g, w_cq, k_mem, v_mem, w_co, ts=512):
    b, s, d = x.shape
    m = k_mem.shape[1]
    mix_spec = pl.BlockSpec((1, ts, MIX_W), lambda bi, i: (bi, i, 0))
    const = lambda a: pl.BlockSpec(a.shape, lambda bi, i: (0,) * a.ndim)
    mem_spec = pl.BlockSpec((1, m, MIX_W), lambda bi, i: (bi, 0, 0))
    dil_flat = [a for pair in dil_branches for a in pair]
    return pl.pallas_call(
        _mix_cross_kernel,
        grid=(b, s // ts),
        in_specs=[pl.BlockSpec((1, ts, d), lambda bi, i: (bi, i, 0))] + [mix_spec] * (3 + len(dil_flat))
                 + [const(w_out), const(g), const(w_cq), mem_spec, mem_spec, const(w_co)],
        out_specs=pl.BlockSpec((1, ts, d), lambda bi, i: (bi, i, 0)),
        out_shape=jax.ShapeDtypeStruct((b, s, d), F32),
        compiler_params=_cparams("parallel", "parallel"),
        name="mix_cross",
    )(x, o_sb, o_fox, o_lru, *dil_flat, w_out, g, w_cq, k_mem, v_mem, w_co)


HALO = 8
FFN_PROJ_LEAD = 2
FFN_EXT_SLOTS = FFN_PROJ_LEAD + 1
FFN_DOWN_LAG = 2
FFN_ACT_SLOTS = FFN_DOWN_LAG + 1


def _ffn_kernel(x_ref, g_ref, wup_ref, cw_ref, cb_ref, wdn_ref, gf_ref, o_ref,
                hn_ref, ext_ref, tail_ref, act_ref, acc_ref, *, final_norm):
    i = pl.program_id(1)
    x = x_ref[0]
    t = x.shape[0]
    hn_ref[...] = _rms(x, g_ref[...]).astype(BF16)
    acc_ref[...] = jnp.zeros_like(acc_ref)

    @pl.when(i == 0)
    def _():
        tail_ref[...] = jnp.zeros_like(tail_ref)

    def project(c):
        for half in range(2):
            blk = half * N_FF_CHUNKS + c
            slot = 2 * (c % FFN_EXT_SLOTS) + half
            pre = _dot(hn_ref[...], wup_ref[:, blk * FF_CHUNK:(blk + 1) * FF_CHUNK])
            ext_ref[slot, 0:HALO] = tail_ref[blk]
            ext_ref[slot, HALO:] = pre
            tail_ref[blk] = pre[t - HALO:]

    def conv(c, slot):
        ext = ext_ref.at[slot]
        w = cw_ref[c]
        return (cb_ref[c] + w[2:3] * ext[pl.ds(HALO, t), :]
                + w[1:2] * ext[pl.ds(HALO - 1, t), :]
                + w[0:1] * ext[pl.ds(HALO - 2, t), :])

    def down(c):
        acc_ref[...] += _dot(act_ref[c % FFN_ACT_SLOTS], wdn_ref[c])

    for c in range(FFN_PROJ_LEAD):
        project(c)
    for c in range(N_FF_CHUNKS):
        if c + FFN_PROJ_LEAD < N_FF_CHUNKS:
            project(c + FFN_PROJ_LEAD)
        if c >= FFN_DOWN_LAG:
            down(c - FFN_DOWN_LAG)
        up = conv(c, 2 * (c % FFN_EXT_SLOTS))
        gate = conv(N_FF_CHUNKS + c, 2 * (c % FFN_EXT_SLOTS) + 1)
        act_ref[c % FFN_ACT_SLOTS] = (_gelu(gate) * up).astype(BF16)
    for c in range(N_FF_CHUNKS - FFN_DOWN_LAG, N_FF_CHUNKS):
        down(c)
    out = x + acc_ref[...]
    if final_norm:
        out = _rms(out, gf_ref[...])
    o_ref[0] = out


def _ffn(x, g, w_up, conv_w, conv_b, w_down, g_final, final_norm, ts=256):
    b, s, d = x.shape
    const = lambda a: pl.BlockSpec(a.shape, lambda bi, i: (0,) * a.ndim)
    return pl.pallas_call(
        functools.partial(_ffn_kernel, final_norm=final_norm),
        grid=(b, s // ts),
        in_specs=[
            pl.BlockSpec((1, ts, d), lambda bi, i: (bi, i, 0)),
            const(g), const(w_up), const(conv_w), const(conv_b), const(w_down), const(g_final),
        ],
        out_specs=pl.BlockSpec((1, ts, d), lambda bi, i: (bi, i, 0)),
        out_shape=jax.ShapeDtypeStruct((b, s, d), F32),
        scratch_shapes=[
            pltpu.VMEM((ts, d), BF16),
            pltpu.VMEM((2 * FFN_EXT_SLOTS, HALO + ts, FF_CHUNK), F32),
            pltpu.VMEM((2 * N_FF_CHUNKS, HALO, FF_CHUNK), F32),
            pltpu.VMEM((FFN_ACT_SLOTS, ts, FF_CHUNK), BF16),
            pltpu.VMEM((ts, d), F32),
        ],
        compiler_params=_cparams("parallel", "arbitrary"),
        name="conv_geglu",
    )(x, g, w_up, conv_w, conv_b, w_down, g_final)


def _block_diag(w):
    g, n, _ = w.shape
    eye = jnp.eye(g, dtype=w.dtype)
    return (eye[:, None, :, None] * w[:, :, None, :]).reshape(g * n, g * n)


def kernel(x, mem, norm_mix_g, w_in, b_forget, lru_conv_w, lru_conv_b, lru_w_a, lru_b_a, lru_w_x, lru_b_x, lru_lambda, w_out, norm_cross_g, norm_mem_g, w_cq, w_ck, w_cv, w_co, norm_ffn_g, w_up, ffn_conv_w, ffn_conv_b, w_down, rel_bias, final_norm_g):
    depth = w_in.shape[0]
    qkv_w = 3 * MIX_W
    q_scale = HEAD_DIM ** -0.5
    col_scale = jnp.ones((3, 3, MIX_W), F32).at[:, 0, :].set(q_scale).reshape(-1)
    col_scale = jnp.concatenate([col_scale, jnp.ones((2 * MIX_W,), F32)])
    tables = _bias_tables(rel_bias)
    row = lambda v: v.reshape(1, -1)

    for l in range(depth):
        w = w_in[l]
        n_f = N_HEADS
        w_main = jnp.concatenate([w[:, :2 * qkv_w], w[:, 2 * qkv_w + n_f:]], axis=1)
        w_main = (w_main * col_scale).astype(BF16)
        w_f = jnp.zeros((8, D_MODEL), F32).at[:n_f].set(w[:, 2 * qkv_w:2 * qkv_w + n_f].T).astype(BF16)
        b_f = jnp.zeros((8, 1), F32).at[:n_f, 0].set(b_forget[l])

        sb_qkv, fox_qkv, *dil_qkvs, lru_xg, f_t = _inproj(x, row(norm_mix_g[l]), w_main, w_f)
        o_sb = _sb_attention(sb_qkv)
        o_fox = _fox_attention(fox_qkv, _decay(f_t, b_f))
        dil_branches = [_dil_branch(qkv, tables, br) for br, qkv in enumerate(dil_qkvs)]
        o_lru = _lru(lru_xg, lru_conv_w[l], row(lru_conv_b[l]),
                     _block_diag(lru_w_a[l]).astype(BF16), row(lru_b_a[l]),
                     _block_diag(lru_w_x[l]).astype(BF16), row(lru_b_x[l]), row(lru_lambda[l]))

        k_mem, v_mem = _memkv(mem, row(norm_mem_g[l]), w_ck[l].astype(BF16), w_cv[l].astype(BF16))
        x = _mix_cross(x, o_sb, o_fox, o_lru, dil_branches,
                       w_out[l].astype(BF16), row(norm_cross_g[l]),
                       (w_cq[l] * q_scale).astype(BF16), k_mem, v_mem, w_co[l].astype(BF16))

        wu = w_up[l].astype(BF16)
        cw = ffn_conv_w[l].reshape(-1, 2 * N_FF_CHUNKS, FF_CHUNK).transpose(1, 0, 2)
        cb = ffn_conv_b[l].reshape(2 * N_FF_CHUNKS, 1, FF_CHUNK)
        wd = w_down[l].reshape(N_FF_CHUNKS, FF_CHUNK, D_MODEL).astype(BF16)
        x = _ffn(x, row(norm_ffn_g[l]), wu, cw, cb, wd, row(final_norm_g), l == depth - 1)
    return x
```

```python
import functools
import math

import jax
import jax.numpy as jnp
from jax import lax
from jax.experimental import pallas as pl
from jax.experimental.pallas import tpu as pltpu

F32 = jnp.float32
BF16 = jnp.bfloat16

LANES = 128
SUBLANES = 8
D_MODEL = 1024
HEAD_DIM = 64
N_HEADS = 4
MIX_W = N_HEADS * HEAD_DIM
D_FF = 2816
FF_CHUNK = 256
N_FF_CHUNKS = D_FF // FF_CHUNK
N_MEM = 256
NUM_BUCKETS = 32
MAX_DISTANCE = 2048
DILATIONS = (1, 4, 16)
DIL_STEPS = 128
LRU_C = 8.0
EPS = 1e-6
NEG_BIG = -1e30

TQ = 128
TQC = 256
TK = 256
VMEM_LIMIT = 56 * 1024 * 1024

NT_DIMS = (((1,), (1,)), ((), ()))


def _cparams(*sem):
    return pltpu.CompilerParams(dimension_semantics=sem, vmem_limit_bytes=VMEM_LIMIT)


def _rms(x, g):
    return x * lax.rsqrt(jnp.mean(x * x, axis=-1, keepdims=True) + EPS) * g


def _softplus(x):
    return jnp.maximum(x, 0.0) + jnp.log(1.0 + jnp.exp(-jnp.abs(x)))


def _gelu(x):
    return 0.5 * x * (1.0 + lax.erf(x * math.sqrt(0.5)))


def _dot(a, b):
    return jnp.dot(a, b, preferred_element_type=F32)


def _dot_nt(a, b):
    return lax.dot_general(a, b, NT_DIMS, preferred_element_type=F32)


def _head_of_lane(rows):
    return lax.broadcasted_iota(jnp.int32, (rows, MIX_W), 1) // HEAD_DIM


def _stack_heads(q):
    t = q.shape[0]
    head = _head_of_lane(t)
    qf = q.astype(F32)
    return jnp.concatenate([jnp.where(head == h, qf, 0.0) for h in range(N_HEADS)], axis=0).astype(BF16)


def _unstack_heads(o, t):
    head = _head_of_lane(t)
    out = jnp.where(head == 0, o[0:t], 0.0)
    for h in range(1, N_HEADS):
        out = jnp.where(head == h, o[h * t:(h + 1) * t], out)
    return out


def _unstack_cols(c, t):
    head = _head_of_lane(t)
    out = jnp.broadcast_to(c[0:t], (t, MIX_W))
    for h in range(1, N_HEADS):
        out = jnp.where(head == h, jnp.broadcast_to(c[h * t:(h + 1) * t], (t, MIX_W)), out)
    return out


def _inproj_kernel(x_ref, g_ref, w_ref, wf_ref, sb_ref, fox_ref, d1_ref, d4_ref, d16_ref,
                   lru_ref, f_ref, hn_ref, half_ref):
    hn_ref[...] = _rms(x_ref[0], g_ref[...]).astype(BF16)
    ts = hn_ref.shape[0]

    def project(rows, block):
        return _dot(hn_ref[rows], w_ref[:, block * MIX_W:(block + 1) * MIX_W])

    for j in range(3):
        res = project(slice(None), 6 + j)
        d1_ref[j, 0, 0] = res.astype(BF16)
        for h in range(2):
            half_ref[j, h] = res[:, h * LANES:(h + 1) * LANES]

    def regroup(j, ref, dil):
        for r in range(dil):
            for h in range(2):
                ref[j, 0, r, :, h * LANES:(h + 1) * LANES] = (
                    half_ref[j, h, pl.ds(r, ts // dil, stride=dil), :].astype(BF16))

    pending = [(j, ref, dil) for j in range(3)
               for ref, dil in ((d4_ref, DILATIONS[1]), (d16_ref, DILATIONS[2]))]
    for m, ref in enumerate((sb_ref, fox_ref)):
        for j in range(3):
            ref[j, 0] = project(slice(None), 3 * m + j).astype(BF16)
            regroup(*pending.pop(0))
    lru_ref[0, 0] = project(slice(None), 9)
    for rows in (slice(0, ts // 2), slice(ts // 2, ts)):
        lru_ref[1, 0, rows] = project(rows, 10)
    f_ref[0] = _dot_nt(wf_ref[...], hn_ref[...])


def _inproj(x, g, w_main, w_f, ts=1024):
    b, s, d = x.shape
    qkv_shape = jax.ShapeDtypeStruct((3, b, s, MIX_W), BF16)
    qkv_spec = pl.BlockSpec((3, 1, ts, MIX_W), lambda bi, i: (0, bi, i, 0))
    cls_shapes = [jax.ShapeDtypeStruct((3, b, dil, s // dil, MIX_W), BF16) for dil in DILATIONS]
    cls_specs = [pl.BlockSpec((3, 1, dil, ts // dil, MIX_W), lambda bi, i: (0, bi, 0, i, 0))
                 for dil in DILATIONS]
    return pl.pallas_call(
        _inproj_kernel,
        grid=(b, s // ts),
        in_specs=[
            pl.BlockSpec((1, ts, d), lambda bi, i: (bi, i, 0)),
            pl.BlockSpec((1, d), lambda bi, i: (0, 0)),
            pl.BlockSpec(w_main.shape, lambda bi, i: (0, 0)),
            pl.BlockSpec(w_f.shape, lambda bi, i: (0, 0)),
        ],
        out_specs=[
            qkv_spec, qkv_spec, *cls_specs,
            pl.BlockSpec((2, 1, ts, MIX_W), lambda bi, i: (0, bi, i, 0)),
            pl.BlockSpec((1, 8, ts), lambda bi, i: (bi, 0, i)),
        ],
        out_shape=[
            qkv_shape, qkv_shape, *cls_shapes,
            jax.ShapeDtypeStruct((2, b, s, MIX_W), F32),
            jax.ShapeDtypeStruct((b, 8, s), F32),
        ],
        scratch_shapes=[pltpu.VMEM((ts, d), BF16), pltpu.VMEM((3, 2, ts, LANES), F32)],
        compiler_params=_cparams("parallel", "parallel"),
        name="inproj",
    )(x, g, w_main, w_f)


def _decay_kernel(f_ref, bf_ref, tri_ref, o_ref):
    s = f_ref.shape[2]
    tri = tri_ref[...]
    carry = jnp.zeros((8, 1), F32)
    for c in range(s // 128):
        sl = slice(c * 128, (c + 1) * 128)
        log_f = -_softplus(-(f_ref[0, :, sl] + bf_ref[...]))
        p1 = log_f.astype(BF16)
        r1 = log_f - p1.astype(F32)
        p2 = r1.astype(BF16)
        p3 = (r1 - p2.astype(F32)).astype(BF16)
        inc = _dot(p1, tri) + _dot(p2, tri) + _dot(p3, tri)
        o_ref[0, :, sl] = inc + carry
        carry = carry + jnp.sum(log_f, axis=-1, keepdims=True)


def _decay(f_t, b_f):
    b, _, s = f_t.shape
    tri = (lax.broadcasted_iota(jnp.int32, (128, 128), 0)
           <= lax.broadcasted_iota(jnp.int32, (128, 128), 1)).astype(BF16)
    return pl.pallas_call(
        _decay_kernel,
        grid=(b,),
        in_specs=[
            pl.BlockSpec((1, 8, s), lambda bi: (bi, 0, 0)),
            pl.BlockSpec((8, 1), lambda bi: (0, 0)),
            pl.BlockSpec((128, 128), lambda bi: (0, 0)),
        ],
        out_specs=pl.BlockSpec((1, 8, s), lambda bi: (bi, 0, 0)),
        out_shape=jax.ShapeDtypeStruct((b, 8, s), F32),
        compiler_params=_cparams("parallel"),
        name="fox_decay",
    )(f_t, b_f, tri)


def _block_positions(i, ks):
    row = lax.broadcasted_iota(jnp.int32, (N_HEADS * TQC, TK), 0) % TQC + i * TQC
    col = lax.broadcasted_iota(jnp.int32, (N_HEADS * TQC, TK), 1) + ks
    return row, col


SB_SATURATED = 110.0


def _sb_kernel(q_ref, k_ref, v_ref, uo_ref, o_ref, acc_ref, car_ref):
    i = pl.program_id(1)
    qs = _stack_heads(q_ref[0, 0])

    half_rows = N_HEADS * TQC // 2
    halves = [slice(0, half_rows), slice(half_rows, 2 * half_rows)]

    def process(blocks):
        items = []
        for kb, diagonal in blocks:
            ks = pl.multiple_of(kb * TK, TK)
            k_blk = k_ref[0, 0, pl.ds(ks, TK), :]
            v_blk = v_ref[0, 0, pl.ds(ks, TK), :]
            strict = None
            if diagonal:
                row, col = _block_positions(i, ks)
                strict = (col < row)[:half_rows]
            items += [(r, k_blk, v_blk, strict) for r in halves]
        zs = [_dot_nt(qs[r], k_blk) for r, k_blk, _, _ in items]
        staged = []
        for (r, _, _, strict), z in zip(items, zs):
            drop = _softplus(z)
            cum_in = drop if strict is None else jnp.where(strict, drop, 0.0)
            staged.append((drop, _dot(cum_in.astype(BF16), uo_ref[...])))
        for (r, _, v_blk, strict), z, (drop, sm) in zip(items, zs, staged):
            if strict is not None:
                att = jnp.where(strict, jnp.exp(z - drop - sm[:, :TK]), 0.0)
                car_ref[r] = sm[:, TK:]
                acc_ref[r] = _dot(att.astype(BF16), v_blk)
            else:
                car = car_ref[r]
                att = jnp.exp(z - drop - sm[:, :TK] - jnp.concatenate([car] * (TK // LANES), axis=1))
                car_ref[r] = car + sm[:, TK:]
                acc_ref[r] += _dot(att.astype(BF16), v_blk)

    def live():
        return jnp.min(car_ref[...]) < SB_SATURATED

    pl.when(i == 0)(lambda: process([(i, True)]))
    pl.when(i > 0)(lambda: process([(i, True), (i - 1, False)]))

    def cond(state):
        kb, go = state
        return jnp.logical_and(kb >= 0, go)

    def body(state):
        kb, _ = state
        process([(kb, False)])
        return kb - 1, live()

    lax.while_loop(cond, body, (i - 2, live()))
    o_ref[0] = _unstack_heads(acc_ref[...], TQC).astype(o_ref.dtype)


def _sb_attention(qkv):
    _, b, s, _ = qkv.shape
    assert TQC == TK
    later =(lax.broadcasted_iota(jnp.int32, (TK, TK), 0)
             > lax.broadcasted_iota(jnp.int32, (TK, TK), 1))
    u = jnp.concatenate([later, jnp.ones((TK, LANES), bool)], axis=1).astype(BF16)
    return pl.pallas_call(
        _sb_kernel,
        grid=(b, s // TQC),
        in_specs=[
            pl.BlockSpec((1, 1, TQC, MIX_W), lambda bi, i: (0, bi, i, 0)),
            pl.BlockSpec((1, 1, s, MIX_W), lambda bi, i: (1, bi, 0, 0)),
            pl.BlockSpec((1, 1, s, MIX_W), lambda bi, i: (2, bi, 0, 0)),
            pl.BlockSpec((TK, TK + LANES), lambda bi, i: (0, 0)),
        ],
        out_specs=pl.BlockSpec((1, TQC, MIX_W), lambda bi, i: (bi, i, 0)),
        out_shape=jax.ShapeDtypeStruct((b, s, MIX_W), BF16),
        scratch_shapes=[
            pltpu.VMEM((N_HEADS * TQC, MIX_W), F32),
            pltpu.VMEM((N_HEADS * TQC, LANES), F32),
        ],
        compiler_params=_cparams("parallel", "arbitrary"),
        name="sb_attn",
    )(qkv, qkv, qkv, u)


def _fox_kernel(q_ref, k_ref, v_ref, cum_ref, o_ref, za_ref, zb_ref, ma_ref, mb_ref, l_ref, acc_ref):
    i = pl.program_id(1)
    n_q = pl.num_programs(1) - 1

    def front_block(qs, kb, diagonal, z_ref, mx_ref):
        ks = pl.multiple_of(kb * TK, TK)
        cum = cum_ref[0, :, pl.ds(ks, TK)]
        decay = jnp.concatenate(
            [jnp.broadcast_to(cum[h:h + 1], (TQC, TK)) for h in range(N_HEADS)], axis=0)
        z = _dot_nt(qs, k_ref[0, 0, pl.ds(ks, TK), :]) - decay
        if diagonal:
            row, col = _block_positions(i, ks)
            z = jnp.where(col <= row, z, NEG_BIG)
        z_ref[kb] = z
        zmax = jnp.maximum(z[:, :LANES], z[:, LANES:])
        mx_ref[...] = zmax if diagonal else jnp.maximum(mx_ref[...], zmax)

    def back_block(kb, z_ref, mx_ref):
        ks = pl.multiple_of(kb * TK, TK)
        m = mx_ref[...]
        p = jnp.exp(z_ref[kb] - jnp.concatenate([m, m], axis=1))
        l_ref[...] += p[:, :LANES] + p[:, LANES:]
        acc_ref[...] += _dot(p.astype(BF16), v_ref[0, 0, pl.ds(ks, TK), :])

    def step(z_cur, m_cur, z_prev, m_prev):
        @pl.when(i < n_q)
        def _():
            qs = _stack_heads(q_ref[0, 0])
            front_block(qs, i, True, z_cur, m_cur)

            def both(kb):
                front_block(qs, kb, False, z_cur, m_cur)
                back_block(kb, z_prev, m_prev)

            def pair(j, carry):
                both(2 * j)
                both(2 * j + 1)
                return carry

            lax.fori_loop(0, i // 2, pair, 0)
            pl.when(i % 2 == 1)(lambda: both(i - 1))
            m = jnp.max(m_cur[...], axis=-1, keepdims=True)
            m_cur[...] = jnp.broadcast_to(m, m_cur.shape)

        @pl.when(i == n_q)
        def _():
            def back_only(kb, carry):
                back_block(kb, z_prev, m_prev)
                return carry

            lax.fori_loop(0, i, back_only, 0)

    l_ref[...] = jnp.zeros_like(l_ref)
    acc_ref[...] = jnp.zeros_like(acc_ref)
    pl.when(i % 2 == 0)(functools.partial(step, za_ref, ma_ref, zb_ref, mb_ref))
    pl.when(i % 2 == 1)(functools.partial(step, zb_ref, mb_ref, za_ref, ma_ref))

    @pl.when(i > 0)
    def _():
        l = _unstack_cols(jnp.sum(l_ref[...], axis=-1, keepdims=True), TQC)
        o_ref[0] = (_unstack_heads(acc_ref[...], TQC) / l).astype(o_ref.dtype)


def _fox_attention(qkv, cum):
    _, b, s, _ = qkv.shape
    assert TQC == TK
    n_q = s // TQC
    rows = N_HEADS * TQC
    return pl.pallas_call(
        _fox_kernel,
        grid=(b, n_q + 1),
        in_specs=[
            pl.BlockSpec((1, 1, TQC, MIX_W), lambda bi, i: (0, bi, jnp.minimum(i, n_q - 1), 0)),
            pl.BlockSpec((1, 1, s, MIX_W), lambda bi, i: (1, bi, 0, 0)),
            pl.BlockSpec((1, 1, s, MIX_W), lambda bi, i: (2, bi, 0, 0)),
            pl.BlockSpec((1, 8, s), lambda bi, i: (bi, 0, 0)),
        ],
        out_specs=pl.BlockSpec((1, TQC, MIX_W), lambda bi, i: (bi, jnp.maximum(i - 1, 0), 0)),
        out_shape=jax.ShapeDtypeStruct((b, s, MIX_W), BF16),
        scratch_shapes=[
            pltpu.VMEM((s // TK, rows, TK), F32),
            pltpu.VMEM((s // TK, rows, TK), F32),
            pltpu.VMEM((rows, LANES), F32),
            pltpu.VMEM((rows, LANES), F32),
            pltpu.VMEM((rows, LANES), F32),
            pltpu.VMEM((rows, MIX_W), F32),
        ],
        compiler_params=_cparams("parallel", "arbitrary"),
        name="fox_attn",
    )(qkv, qkv, qkv, cum)


def _t5_bucket(dist):
    n = jnp.maximum(dist, 0)
    max_exact = NUM_BUCKETS // 2
    nf = jnp.maximum(n, 1).astype(F32)
    large = max_exact + jnp.floor(jnp.log(nf / max_exact) / math.log(MAX_DISTANCE / max_exact)
                                  * (NUM_BUCKETS - max_exact)).astype(jnp.int32)
    large = jnp.minimum(large, NUM_BUCKETS - 1)
    return jnp.where(n < max_exact, n, large)


def _bias_kernel(idx_ref, rb_ref, o_ref):
    for d in range(len(DILATIONS)):
        for variant in range(2):
            idx = idx_ref[d, variant]
            for h in range(N_HEADS):
                acc = jnp.full(idx.shape, NEG_BIG, F32)
                for bucket in range(NUM_BUCKETS):
                    acc = jnp.where(idx == bucket, rb_ref[bucket, h], acc)
                o_ref[d, variant, h] = acc


def _bias_tables(rel_bias):
    t = jnp.arange(TQ)[:, None]
    c = jnp.arange(2 * TQ)[None, :]
    delta = TQ + t - c
    valid = (delta >= 0) & (delta <= DIL_STEPS)
    idx = jnp.stack([jnp.where(valid, _t5_bucket(delta * dil), -1) for dil in DILATIONS])
    idx_first = jnp.concatenate([idx[:, :, TQ:], jnp.full_like(idx[:, :, TQ:], -1)], axis=-1)
    idx = jnp.stack([idx, idx_first], axis=1)
    out = pl.pallas_call(
        _bias_kernel,
        in_specs=[
            pl.BlockSpec(memory_space=pltpu.VMEM),
            pl.BlockSpec(memory_space=pltpu.SMEM),
        ],
        out_specs=pl.BlockSpec(memory_space=pltpu.VMEM),
        out_shape=jax.ShapeDtypeStruct((len(DILATIONS), 2, N_HEADS, TQ, 2 * TQ), F32),
        name="dil_bias",
    )(idx.astype(jnp.int32), rel_bias)
    return out.reshape(len(DILATIONS), 2, N_HEADS * TQ, 2 * TQ)


DIL_GROUP = 8


def _dil_kernel(q_ref, k_ref, v_ref, t_ref, o_ref, lse_ref, *, by_class, group, n_groups):
    g = pl.program_id(2)

    def attend(jj, scores, v_win):
        m = jnp.max(scores, axis=-1, keepdims=True)
        p = jnp.exp(scores - m)
        l = _unstack_cols(jnp.sum(p, axis=-1, keepdims=True), TQ)
        o = (_unstack_heads(_dot(p.astype(BF16), v_win), TQ) / l).astype(o_ref.dtype)
        lse = _unstack_cols(m, TQ) + jnp.log(l)
        if by_class:
            o_ref[0, jj] = o
            lse_ref[0, jj] = lse
        else:
            o_ref[0, 0, jj * TQ:(jj + 1) * TQ] = o
            lse_ref[0, 0, jj * TQ:(jj + 1) * TQ] = lse

    pending = []
    for jj in range(group):
        if by_class:
            scores = (_dot_nt(_stack_heads(q_ref[0, 0, jj]), k_ref[0, 0, jj])
                      + t_ref[0, 0, :, TQ:2 * TQ])
            pending.append((scores, v_ref[0, 0, jj]))
            continue
        qs = _stack_heads(q_ref[0, 0, 0, jj * TQ:(jj + 1) * TQ])
        if n_groups == 1 or jj > 0:
            n = g * group + jj if n_groups > 1 else jj
            start = (n - 1) * TQ if jj > 0 else 0
            bias = t_ref[0, 0 if jj > 0 else 1]
        else:
            start = jnp.maximum(g * group - 1, 0) * TQ
            bias = t_ref[0, (g == 0).astype(jnp.int32)]
        if not isinstance(start, int):
            start = pl.multiple_of(start, TQ)
        window = pl.ds(start, 2 * TQ)
        pending.append((_dot_nt(qs, k_ref[0, 0, 0, window, :]) + bias, v_ref[0, 0, 0, window, :]))
    for jj, (scores, v_win) in enumerate(pending):
        attend(jj, scores, v_win)


def _dil_branch(qkv, tables, branch):
    _, b, dil, cls_len, _ = qkv.shape
    n_blocks = cls_len // TQ
    by_class = n_blocks == 1
    group = min(DIL_GROUP, dil if by_class else n_blocks)
    if by_class:
        grid = (b, dil // group, 1)
        qkv_specs = [pl.BlockSpec((1, 1, group, TQ, MIX_W), functools.partial(
            lambda bi, c, g, j: (j, bi, c, 0, 0), j=j)) for j in range(3)]
        out_spec = pl.BlockSpec((1, group, TQ, MIX_W), lambda bi, c, g: (bi, c, 0, 0))
    else:
        grid = (b, dil, n_blocks // group)
        qkv_specs = [pl.BlockSpec((1, 1, 1, group * TQ, MIX_W), lambda bi, r, g: (0, bi, r, g, 0))]
        qkv_specs += [pl.BlockSpec((1, 1, 1, cls_len, MIX_W), functools.partial(
            lambda bi, r, g, j: (j, bi, r, 0, 0), j=j)) for j in (1, 2)]
        out_spec = pl.BlockSpec((1, 1, group * TQ, MIX_W), lambda bi, r, g: (bi, r, g, 0))
    o, lse = pl.pallas_call(
        functools.partial(_dil_kernel, by_class=by_class, group=group, n_groups=grid[2]),
        grid=grid,
        in_specs=qkv_specs + [
            pl.BlockSpec((1, 2, N_HEADS * TQ, 2 * TQ), lambda bi, r, g: (branch, 0, 0, 0))],
        out_specs=[out_spec, out_spec],
        out_shape=[jax.ShapeDtypeStruct((b, dil, cls_len, MIX_W), BF16),
                   jax.ShapeDtypeStruct((b, dil, cls_len, MIX_W), F32)],
        compiler_params=_cparams("parallel", "parallel", "arbitrary"),
        name=f"dil_attn_{dil}",
    )(qkv, qkv, qkv, tables)
    to_seq = lambda a: a.transpose(0, 2, 1, 3).reshape(b, dil * cls_len, MIX_W)
    return to_seq(o), to_seq(lse)


def _dil_mixture(outs, lses):
    m = functools.reduce(jnp.maximum, lses)
    es = [jnp.exp(a - m) for a in lses]
    tot = functools.reduce(jnp.add, es)
    return functools.reduce(jnp.add, [(e / tot) * o for e, o in zip(es, outs)])


def _shift_rows(x, k, fill):
    row = lax.broadcasted_iota(jnp.int32, x.shape, 0)
    return jnp.where(row >= k, pltpu.roll(x, k, 0), fill)


def _lru_kernel(xg_ref, cw_ref, cb_ref, wa_ref, ba_ref, wx_ref, bx_ref, lam_ref, o_ref, h_ref):
    x = xg_ref[0, 0]
    gate = xg_ref[1, 0]
    s = x.shape[0]
    kw = cw_ref.shape[0]
    xc = cb_ref[...] + cw_ref[kw - 1:kw, :] * x
    for j in range(kw - 1):
        xc = xc + cw_ref[j:j + 1, :] * _shift_rows(x, kw - 1 - j, 0.0)
    xcb = xc.astype(BF16)
    r = jax.nn.sigmoid(_dot(xcb, wa_ref[...]) + ba_ref[...])
    i_gate = jax.nn.sigmoid(_dot(xcb, wx_ref[...]) + bx_ref[...])
    a = jnp.exp(-LRU_C * r * _softplus(-lam_ref[...]))
    h = jnp.sqrt(1.0 - a * a) * (i_gate * xc)
    grouped = (s // SUBLANES, SUBLANES, x.shape[1])
    h, a = h.reshape(grouped), a.reshape(grouped)
    in_group = lax.broadcasted_iota(jnp.int32, grouped, 1)
    k = 1
    while k < SUBLANES:
        prev = in_group >= k
        h = a * jnp.where(prev, pltpu.roll(h, k, 1), 0.0) + h
        a = a * jnp.where(prev, pltpu.roll(a, k, 1), 1.0)
        k *= 2
    h, a = h.reshape(x.shape), a.reshape(x.shape)
    carry = jnp.zeros((1, x.shape[1]), F32)
    for c in range(s // SUBLANES):
        rows = slice(c * SUBLANES, (c + 1) * SUBLANES)
        hc = h[rows] + a[rows] * carry
        h_ref[rows] = hc
        carry = hc[SUBLANES - 1:SUBLANES]
    o_ref[0] = (h_ref[...] * _gelu(gate)).astype(o_ref.dtype)


def _lru(xg, conv_w, conv_b, wa_bd, b_a, wx_bd, b_x, lam):
    _, b, s, w = xg.shape
    full = lambda a: pl.BlockSpec(a.shape, lambda bi: (0,) * a.ndim)
    args = (conv_w, conv_b, wa_bd, b_a, wx_bd, b_x, lam)
    return pl.pallas_call(
        _lru_kernel,
        grid=(b,),
        in_specs=[pl.BlockSpec((2, 1, s, w), lambda bi: (0, bi, 0, 0))] + [full(a) for a in args],
        out_specs=pl.BlockSpec((1, s, w), lambda bi: (bi, 0, 0)),
        out_shape=jax.ShapeDtypeStruct((b, s, w), BF16),
        scratch_shapes=[pltpu.VMEM((s, w), F32)],
        compiler_params=_cparams("parallel"),
        name="rg_lru",
    )(xg, *args)


def _memkv_kernel(mem_ref, g_ref, wk_ref, wv_ref, k_ref, v_ref):
    mn = _rms(mem_ref[0], g_ref[...]).astype(BF16)
    k_ref[0] = _dot(mn, wk_ref[...]).astype(BF16)
    v_ref[0] = _dot(mn, wv_ref[...]).astype(BF16)


def _memkv(mem, g, w_ck, w_cv):
    b, m, d = mem.shape
    out = jax.ShapeDtypeStruct((b, m, MIX_W), BF16)
    return pl.pallas_call(
        _memkv_kernel,
        grid=(b,),
        in_specs=[
            pl.BlockSpec((1, m, d), lambda bi: (bi, 0, 0)),
            pl.BlockSpec((1, d), lambda bi: (0, 0)),
            pl.BlockSpec((d, MIX_W), lambda bi: (0, 0)),
            pl.BlockSpec((d, MIX_W), lambda bi: (0, 0)),
        ],
        out_specs=[pl.BlockSpec((1, m, MIX_W), lambda bi: (bi, 0, 0))] * 2,
        out_shape=[out, out],
        compiler_params=_cparams("parallel"),
        name="mem_kv",
    )(mem, g, w_ck, w_cv)


def _mix_cross_kernel(x_ref, sb_ref, fox_ref, lru_ref, o1_ref, l1_ref, o4_ref, l4_ref,
                      o16_ref, l16_ref, wo_ref, g_ref, wq_ref, km_ref, vm_ref, wco_ref, o_ref):
    t = MIX_SUB
    subs = [slice(u * t, (u + 1) * t) for u in range(x_ref.shape[1] // t)]
    xs = []
    for r in subs:
        dil = _dil_mixture([ref[0, r] for ref in (o1_ref, o4_ref, o16_ref)],
                           [ref[0, r] for ref in (l1_ref, l4_ref, l16_ref)]).astype(BF16)
        mixed = jnp.concatenate([sb_ref[0, r], fox_ref[0, r], dil, lru_ref[0, r]], axis=1)
        xs.append(x_ref[0, r] + _dot(mixed, wo_ref[...]))
    qs = [_stack_heads(_dot(_rms(x, g_ref[...]).astype(BF16), wq_ref[...])) for x in xs]
    scores = [_dot_nt(q, km_ref[0]) for q in qs]
    outs = []
    for sc in scores:
        p = jnp.exp(sc - jnp.max(sc, axis=-1, keepdims=True))
        p = p / jnp.sum(p, axis=-1, keepdims=True)
        outs.append(_unstack_heads(_dot(p.astype(BF16), vm_ref[0]), t))
    for r, x, o in zip(subs, xs, outs):
        o_ref[0, r] = x + _dot(o.astype(BF16), wco_ref[...])


MIX_SUB = 128


def _mix_cross(x, o_sb, o_fox, o_lru, dil_branches, w_out, g, w_cq, k_mem, v_mem, w_co, ts=1024):
    b, s, d = x.shape
    m = k_mem.shape[1]
    mix_spec = pl.BlockSpec((1, ts, MIX_W), lambda bi, i: (bi, i, 0))
    const = lambda a: pl.BlockSpec(a.shape, lambda bi, i: (0,) * a.ndim)
    mem_spec = pl.BlockSpec((1, m, MIX_W), lambda bi, i: (bi, 0, 0))
    dil_flat = [a for pair in dil_branches for a in pair]
    return pl.pallas_call(
        _mix_cross_kernel,
        grid=(b, s // ts),
        in_specs=[pl.BlockSpec((1, ts, d), lambda bi, i: (bi, i, 0))] + [mix_spec] * (3 + len(dil_flat))
                 + [const(w_out), const(g), const(w_cq), mem_spec, mem_spec, const(w_co)],
        out_specs=pl.BlockSpec((1, ts, d), lambda bi, i: (bi, i, 0)),
        out_shape=jax.ShapeDtypeStruct((b, s, d), F32),
        compiler_params=_cparams("parallel", "parallel"),
        name="mix_cross",
    )(x, o_sb, o_fox, o_lru, *dil_flat, w_out, g, w_cq, k_mem, v_mem, w_co)


HALO = 8
FFN_PROJ_LEAD = 2
FFN_EXT_SLOTS = FFN_PROJ_LEAD + 1
FFN_DOWN_LAG = 2
FFN_ACT_SLOTS = FFN_DOWN_LAG + 1


def _ffn_kernel(x_ref, g_ref, wup_ref, cw_ref, cb_ref, wdn_ref, gf_ref, o_ref,
                hn_ref, ext_ref, tail_ref, act_ref, acc_ref, *, final_norm):
    i = pl.program_id(1)
    x = x_ref[0]
    t = x.shape[0]
    hn_ref[...] = _rms(x, g_ref[...]).astype(BF16)
    acc_ref[...] = jnp.zeros_like(acc_ref)

    @pl.when(i == 0)
    def _():
        tail_ref[...] = jnp.zeros_like(tail_ref)

    def project(c):
        for half in range(2):
            blk = half * N_FF_CHUNKS + c
            slot = 2 * (c % FFN_EXT_SLOTS) + half
            pre = _dot(hn_ref[...], wup_ref[:, blk * FF_CHUNK:(blk + 1) * FF_CHUNK])
            ext_ref[slot, 0:HALO] = tail_ref[blk]
            ext_ref[slot, HALO:] = pre
            tail_ref[blk] = pre[t - HALO:]

    def conv(c, slot):
        ext = ext_ref.at[slot]
        w = cw_ref[c]
        return (cb_ref[c] + w[2:3] * ext[pl.ds(HALO, t), :]
                + w[1:2] * ext[pl.ds(HALO - 1, t), :]
                + w[0:1] * ext[pl.ds(HALO - 2, t), :])

    def down(c):
        acc_ref[...] += _dot(act_ref[c % FFN_ACT_SLOTS], wdn_ref[c])

    for c in range(FFN_PROJ_LEAD):
        project(c)
    for c in range(N_FF_CHUNKS):
        if c + FFN_PROJ_LEAD < N_FF_CHUNKS:
            project(c + FFN_PROJ_LEAD)
        if c >= FFN_DOWN_LAG:
            down(c - FFN_DOWN_LAG)
        up = conv(c, 2 * (c % FFN_EXT_SLOTS))
        gate = conv(N_FF_CHUNKS + c, 2 * (c % FFN_EXT_SLOTS) + 1)
        act_ref[c % FFN_ACT_SLOTS] = (_gelu(gate) * up).astype(BF16)
    for c in range(N_FF_CHUNKS - FFN_DOWN_LAG, N_FF_CHUNKS):
        down(c)
    out = x + acc_ref[...]
    if final_norm:
        out = _rms(out, gf_ref[...])
    o_ref[0] = out


def _ffn(x, g, w_up, conv_w, conv_b, w_down, g_final, final_norm, ts=256):
    b, s, d = x.shape
    const = lambda a: pl.BlockSpec(a.shape, lambda bi, i: (0,) * a.ndim)
    return pl.pallas_call(
        functools.partial(_ffn_kernel, final_norm=final_norm),
        grid=(b, s // ts),
        in_specs=[
            pl.BlockSpec((1, ts, d), lambda bi, i: (bi, i, 0)),
            const(g), const(w_up), const(conv_w), const(conv_b), const(w_down), const(g_final),
        ],
        out_specs=pl.BlockSpec((1, ts, d), lambda bi, i: (bi, i, 0)),
        out_shape=jax.ShapeDtypeStruct((b, s, d), F32),
        scratch_shapes=[
            pltpu.VMEM((ts, d), BF16),
            pltpu.VMEM((2 * FFN_EXT_SLOTS, HALO + ts, FF_CHUNK), F32),
            pltpu.VMEM((2 * N_FF_CHUNKS, HALO, FF_CHUNK), F32),
            pltpu.VMEM((FFN_ACT_SLOTS, ts, FF_CHUNK), BF16),
            pltpu.VMEM((ts, d), F32),
        ],
        compiler_params=_cparams("parallel", "arbitrary"),
        name="conv_geglu",
    )(x, g, w_up, conv_w, conv_b, w_down, g_final)


def _block_diag(w):
    g, n, _ = w.shape
    eye = jnp.eye(g, dtype=w.dtype)
    return (eye[:, None, :, None] * w[:, :, None, :]).reshape(g * n, g * n)


def kernel(x, mem, norm_mix_g, w_in, b_forget, lru_conv_w, lru_conv_b, lru_w_a, lru_b_a, lru_w_x, lru_b_x, lru_lambda, w_out, norm_cross_g, norm_mem_g, w_cq, w_ck, w_cv, w_co, norm_ffn_g, w_up, ffn_conv_w, ffn_conv_b, w_down, rel_bias, final_norm_g):
    depth = w_in.shape[0]
    qkv_w = 3 * MIX_W
    q_scale = HEAD_DIM ** -0.5
    col_scale = jnp.ones((3, 3, MIX_W), F32).at[:, 0, :].set(q_scale).reshape(-1)
    col_scale = jnp.concatenate([col_scale, jnp.ones((2 * MIX_W,), F32)])
    tables = _bias_tables(rel_bias)
    row = lambda v: v.reshape(1, -1)

    for l in range(depth):
        w = w_in[l]
        n_f = N_HEADS
        w_main = jnp.concatenate([w[:, :2 * qkv_w], w[:, 2 * qkv_w + n_f:]], axis=1)
        w_main = (w_main * col_scale).astype(BF16)
        w_f = jnp.zeros((8, D_MODEL), F32).at[:n_f].set(w[:, 2 * qkv_w:2 * qkv_w + n_f].T).astype(BF16)
        b_f = jnp.zeros((8, 1), F32).at[:n_f, 0].set(b_forget[l])

        sb_qkv, fox_qkv, *dil_qkvs, lru_xg, f_t = _inproj(x, row(norm_mix_g[l]), w_main, w_f)
        o_sb = _sb_attention(sb_qkv)
        o_fox = _fox_attention(fox_qkv, _decay(f_t, b_f))
        dil_branches = [_dil_branch(qkv, tables, br) for br, qkv in enumerate(dil_qkvs)]
        o_lru = _lru(lru_xg, lru_conv_w[l], row(lru_conv_b[l]),
                     _block_diag(lru_w_a[l]).astype(BF16), row(lru_b_a[l]),
                     _block_diag(lru_w_x[l]).astype(BF16), row(lru_b_x[l]), row(lru_lambda[l]))

        k_mem, v_mem = _memkv(mem, row(norm_mem_g[l]), w_ck[l].astype(BF16), w_cv[l].astype(BF16))
        x = _mix_cross(x, o_sb, o_fox, o_lru, dil_branches,
                       w_out[l].astype(BF16), row(norm_cross_g[l]),
                       (w_cq[l] * q_scale).astype(BF16), k_mem, v_mem, w_co[l].astype(BF16))

        wu = w_up[l].astype(BF16)
        cw = ffn_conv_w[l].reshape(-1, 2 * N_FF_CHUNKS, FF_CHUNK).transpose(1, 0, 2)
        cb = ffn_conv_b[l].reshape(2 * N_FF_CHUNKS, 1, FF_CHUNK)
        wd = w_down[l].reshape(N_FF_CHUNKS, FF_CHUNK, D_MODEL).astype(BF16)
        x = _ffn(x, row(norm_ffn_g[l]), wu, cw, cb, wd, row(final_norm_g), l == depth - 1)
    return x
```

```python
import functools
import math

import jax
import jax.numpy as jnp
from jax import lax
from jax.experimental import pallas as pl
from jax.experimental.pallas import tpu as pltpu

F32 = jnp.float32
BF16 = jnp.bfloat16

LANES = 128
SUBLANES = 8
D_MODEL = 1024
HEAD_DIM = 64
N_HEADS = 4
MIX_W = N_HEADS * HEAD_DIM
D_FF = 2816
FF_CHUNK = 256
N_FF_CHUNKS = D_FF // FF_CHUNK
N_MEM = 256
NUM_BUCKETS = 32
MAX_DISTANCE = 2048
DILATIONS = (1, 4, 16)
DIL_STEPS = 128
LRU_C = 8.0
EPS = 1e-6
NEG_BIG = -1e30

TQ = 128
TQC = 256
TK = 256
VMEM_LIMIT = 56 * 1024 * 1024

NT_DIMS = (((1,), (1,)), ((), ()))


def _cparams(*sem):
    return pltpu.CompilerParams(dimension_semantics=sem, vmem_limit_bytes=VMEM_LIMIT)


def _rms(x, g):
    return x * lax.rsqrt(jnp.mean(x * x, axis=-1, keepdims=True) + EPS) * g


def _softplus(x):
    return jnp.maximum(x, 0.0) + jnp.log(1.0 + jnp.exp(-jnp.abs(x)))


def _gelu(x):
    return 0.5 * x * (1.0 + lax.erf(x * math.sqrt(0.5)))


def _dot(a, b):
    return jnp.dot(a, b, preferred_element_type=F32)


def _dot_nt(a, b):
    return lax.dot_general(a, b, NT_DIMS, preferred_element_type=F32)


def _head_of_lane(rows):
    return lax.broadcasted_iota(jnp.int32, (rows, MIX_W), 1) // HEAD_DIM


def _stack_heads(q):
    t = q.shape[0]
    head = _head_of_lane(t)
    qf = q.astype(F32)
    return jnp.concatenate([jnp.where(head == h, qf, 0.0) for h in range(N_HEADS)], axis=0).astype(BF16)


def _unstack_heads(o, t):
    head = _head_of_lane(t)
    out = jnp.where(head == 0, o[0:t], 0.0)
    for h in range(1, N_HEADS):
        out = jnp.where(head == h, o[h * t:(h + 1) * t], out)
    return out


def _unstack_cols(c, t):
    head = _head_of_lane(t)
    out = jnp.broadcast_to(c[0:t], (t, MIX_W))
    for h in range(1, N_HEADS):
        out = jnp.where(head == h, jnp.broadcast_to(c[h * t:(h + 1) * t], (t, MIX_W)), out)
    return out


def _inproj_kernel(x_ref, g_ref, w_ref, wf_ref, sb_ref, fox_ref, d1_ref, d4_ref, d16_ref,
                   lru_ref, f_ref, hn_ref, half_ref):
    hn_ref[...] = _rms(x_ref[0], g_ref[...]).astype(BF16)
    ts = hn_ref.shape[0]

    def project(rows, block):
        return _dot(hn_ref[rows], w_ref[:, block * MIX_W:(block + 1) * MIX_W])

    for j in range(3):
        res = project(slice(None), 6 + j)
        d1_ref[j, 0, 0] = res.astype(BF16)
        for h in range(2):
            half_ref[j, h] = res[:, h * LANES:(h + 1) * LANES]

    def regroup(j, ref, dil):
        for r in range(dil):
            for h in range(2):
                ref[j, 0, r, :, h * LANES:(h + 1) * LANES] = (
                    half_ref[j, h, pl.ds(r, ts // dil, stride=dil), :].astype(BF16))

    pending = [(j, ref, dil) for j in range(3)
               for ref, dil in ((d4_ref, DILATIONS[1]), (d16_ref, DILATIONS[2]))]
    for m, ref in enumerate((sb_ref, fox_ref)):
        for j in range(3):
            ref[j, 0] = project(slice(None), 3 * m + j).astype(BF16)
            regroup(*pending.pop(0))
    lru_ref[0, 0] = project(slice(None), 9)
    for rows in (slice(0, ts // 2), slice(ts // 2, ts)):
        lru_ref[1, 0, rows] = project(rows, 10)
    f_ref[0] = _dot_nt(wf_ref[...], hn_ref[...])


def _inproj(x, g, w_main, w_f, ts=1024):
    b, s, d = x.shape
    qkv_shape = jax.ShapeDtypeStruct((3, b, s, MIX_W), BF16)
    qkv_spec = pl.BlockSpec((3, 1, ts, MIX_W), lambda bi, i: (0, bi, i, 0))
    cls_shapes = [jax.ShapeDtypeStruct((3, b, dil, s // dil, MIX_W), BF16) for dil in DILATIONS]
    cls_specs = [pl.BlockSpec((3, 1, dil, ts // dil, MIX_W), lambda bi, i: (0, bi, 0, i, 0))
                 for dil in DILATIONS]
    return pl.pallas_call(
        _inproj_kernel,
        grid=(b, s // ts),
        in_specs=[
            pl.BlockSpec((1, ts, d), lambda bi, i: (bi, i, 0)),
            pl.BlockSpec((1, d), lambda bi, i: (0, 0)),
            pl.BlockSpec(w_main.shape, lambda bi, i: (0, 0)),
            pl.BlockSpec(w_f.shape, lambda bi, i: (0, 0)),
        ],
        out_specs=[
            qkv_spec, qkv_spec, *cls_specs,
            pl.BlockSpec((2, 1, ts, MIX_W), lambda bi, i: (0, bi, i, 0)),
            pl.BlockSpec((1, 8, ts), lambda bi, i: (bi, 0, i)),
        ],
        out_shape=[
            qkv_shape, qkv_shape, *cls_shapes,
            jax.ShapeDtypeStruct((2, b, s, MIX_W), F32),
            jax.ShapeDtypeStruct((b, 8, s), F32),
        ],
        scratch_shapes=[pltpu.VMEM((ts, d), BF16), pltpu.VMEM((3, 2, ts, LANES), F32)],
        compiler_params=_cparams("parallel", "parallel"),
        name="inproj",
    )(x, g, w_main, w_f)


def _decay_kernel(f_ref, bf_ref, tri_ref, o_ref):
    s = f_ref.shape[2]
    tri = tri_ref[...]
    carry = jnp.zeros((8, 1), F32)
    for c in range(s // 128):
        sl = slice(c * 128, (c + 1) * 128)
        log_f = -_softplus(-(f_ref[0, :, sl] + bf_ref[...]))
        p1 = log_f.astype(BF16)
        r1 = log_f - p1.astype(F32)
        p2 = r1.astype(BF16)
        p3 = (r1 - p2.astype(F32)).astype(BF16)
        inc = _dot(p1, tri) + _dot(p2, tri) + _dot(p3, tri)
        o_ref[0, :, sl] = inc + carry
        carry = carry + jnp.sum(log_f, axis=-1, keepdims=True)


def _decay(f_t, b_f):
    b, _, s = f_t.shape
    tri = (lax.broadcasted_iota(jnp.int32, (128, 128), 0)
           <= lax.broadcasted_iota(jnp.int32, (128, 128), 1)).astype(BF16)
    return pl.pallas_call(
        _decay_kernel,
        grid=(b,),
        in_specs=[
            pl.BlockSpec((1, 8, s), lambda bi: (bi, 0, 0)),
            pl.BlockSpec((8, 1), lambda bi: (0, 0)),
            pl.BlockSpec((128, 128), lambda bi: (0, 0)),
        ],
        out_specs=pl.BlockSpec((1, 8, s), lambda bi: (bi, 0, 0)),
        out_shape=jax.ShapeDtypeStruct((b, 8, s), F32),
        compiler_params=_cparams("parallel"),
        name="fox_decay",
    )(f_t, b_f, tri)


def _block_positions(i, ks):
    row = lax.broadcasted_iota(jnp.int32, (N_HEADS * TQC, TK), 0) % TQC + i * TQC
    col = lax.broadcasted_iota(jnp.int32, (N_HEADS * TQC, TK), 1) + ks
    return row, col


SB_SATURATED = 110.0


def _sb_kernel(q_ref, k_ref, v_ref, uo_ref, o_ref, acc_ref, car_ref):
    i = pl.program_id(1)
    qs = _stack_heads(q_ref[0, 0])

    half_rows = N_HEADS * TQC // 2
    halves = [slice(0, half_rows), slice(half_rows, 2 * half_rows)]

    def process(blocks):
        items = []
        for kb, diagonal in blocks:
            ks = pl.multiple_of(kb * TK, TK)
            k_blk = k_ref[0, 0, pl.ds(ks, TK), :]
            v_blk = v_ref[0, 0, pl.ds(ks, TK), :]
            strict = None
            if diagonal:
                row, col = _block_positions(i, ks)
                strict = (col < row)[:half_rows]
            items += [(r, k_blk, v_blk, strict) for r in halves]
        zs = [_dot_nt(qs[r], k_blk) for r, k_blk, _, _ in items]
        staged = []
        for (r, _, _, strict), z in zip(items, zs):
            drop = _softplus(z)
            cum_in = drop if strict is None else jnp.where(strict, drop, 0.0)
            staged.append((drop, _dot(cum_in.astype(BF16), uo_ref[...])))
        for (r, _, v_blk, strict), z, (drop, sm) in zip(items, zs, staged):
            if strict is not None:
                att = jnp.where(strict, jnp.exp(z - drop - sm[:, :TK]), 0.0)
                car_ref[r] = sm[:, TK:]
                acc_ref[r] = _dot(att.astype(BF16), v_blk)
            else:
                car = car_ref[r]
                att = jnp.exp(z - drop - sm[:, :TK] - jnp.concatenate([car] * (TK // LANES), axis=1))
                car_ref[r] = car + sm[:, TK:]
                acc_ref[r] += _dot(att.astype(BF16), v_blk)

    def live():
        return jnp.min(car_ref[...]) < SB_SATURATED

    pl.when(i == 0)(lambda: process([(i, True)]))
    pl.when(i > 0)(lambda: process([(i, True), (i - 1, False)]))

    def cond(state):
        kb, go = state
        return jnp.logical_and(kb >= 0, go)

    def body(state):
        kb, _ = state
        process([(kb, False)])
        return kb - 1, live()

    lax.while_loop(cond, body, (i - 2, live()))
    o_ref[0] = _unstack_heads(acc_ref[...], TQC).astype(o_ref.dtype)


def _sb_attention(qkv):
    _, b, s, _ = qkv.shape
    assert TQC == TK
    later =(lax.broadcasted_iota(jnp.int32, (TK, TK), 0)
             > lax.broadcasted_iota(jnp.int32, (TK, TK), 1))
    u = jnp.concatenate([later, jnp.ones((TK, LANES), bool)], axis=1).astype(BF16)
    return pl.pallas_call(
        _sb_kernel,
        grid=(b, s // TQC),
        in_specs=[
            pl.BlockSpec((1, 1, TQC, MIX_W), lambda bi, i: (0, bi, i, 0)),
            pl.BlockSpec((1, 1, s, MIX_W), lambda bi, i: (1, bi, 0, 0)),
            pl.BlockSpec((1, 1, s, MIX_W), lambda bi, i: (2, bi, 0, 0)),
            pl.BlockSpec((TK, TK + LANES), lambda bi, i: (0, 0)),
        ],
        out_specs=pl.BlockSpec((1, TQC, MIX_W), lambda bi, i: (bi, i, 0)),
        out_shape=jax.ShapeDtypeStruct((b, s, MIX_W), BF16),
        scratch_shapes=[
            pltpu.VMEM((N_HEADS * TQC, MIX_W), F32),
            pltpu.VMEM((N_HEADS * TQC, LANES), F32),
        ],
        compiler_params=_cparams("parallel", "arbitrary"),
        name="sb_attn",
    )(qkv, qkv, qkv, u)


def _fox_kernel(q_ref, k_ref, v_ref, cum_ref, o_ref, za_ref, zb_ref, ma_ref, mb_ref, l_ref, acc_ref):
    i = pl.program_id(1)
    n_q = pl.num_programs(1) - 1

    def front_block(qs, kb, diagonal, z_ref, mx_ref):
        ks = pl.multiple_of(kb * TK, TK)
        cum = cum_ref[0, :, pl.ds(ks, TK)]
        decay = jnp.concatenate(
            [jnp.broadcast_to(cum[h:h + 1], (TQC, TK)) for h in range(N_HEADS)], axis=0)
        z = _dot_nt(qs, k_ref[0, 0, pl.ds(ks, TK), :]) - decay
        if diagonal:
            row, col = _block_positions(i, ks)
            z = jnp.where(col <= row, z, NEG_BIG)
        z_ref[kb] = z
        zmax = jnp.maximum(z[:, :LANES], z[:, LANES:])
        mx_ref[...] = zmax if diagonal else jnp.maximum(mx_ref[...], zmax)

    def back_block(kb, z_ref, mx_ref):
        ks = pl.multiple_of(kb * TK, TK)
        m = mx_ref[...]
        p = jnp.exp(z_ref[kb] - jnp.concatenate([m, m], axis=1))
        l_ref[...] += p[:, :LANES] + p[:, LANES:]
        acc_ref[...] += _dot(p.astype(BF16), v_ref[0, 0, pl.ds(ks, TK), :])

    def step(z_cur, m_cur, z_prev, m_prev):
        @pl.when(i < n_q)
        def _():
            qs = _stack_heads(q_ref[0, 0])
            front_block(qs, i, True, z_cur, m_cur)

            def both(kb):
                front_block(qs, kb, False, z_cur, m_cur)
                back_block(kb, z_prev, m_prev)

            def pair(j, carry):
                both(2 * j)
                both(2 * j + 1)
                return carry

            lax.fori_loop(0, i // 2, pair, 0)
            pl.when(i % 2 == 1)(lambda: both(i - 1))
            m = jnp.max(m_cur[...], axis=-1, keepdims=True)
            m_cur[...] = jnp.broadcast_to(m, m_cur.shape)

        @pl.when(i == n_q)
        def _():
            def back_only(kb, carry):
                back_block(kb, z_prev, m_prev)
                return carry

            lax.fori_loop(0, i, back_only, 0)

    l_ref[...] = jnp.zeros_like(l_ref)
    acc_ref[...] = jnp.zeros_like(acc_ref)
    pl.when(i % 2 == 0)(functools.partial(step, za_ref, ma_ref, zb_ref, mb_ref))
    pl.when(i % 2 == 1)(functools.partial(step, zb_ref, mb_ref, za_ref, ma_ref))

    @pl.when(i > 0)
    def _():
        l = _unstack_cols(jnp.sum(l_ref[...], axis=-1, keepdims=True), TQC)
        o_ref[0] = (_unstack_heads(acc_ref[...], TQC) / l).astype(o_ref.dtype)


def _fox_attention(qkv, cum):
    _, b, s, _ = qkv.shape
    assert TQC == TK
    n_q = s // TQC
    rows = N_HEADS * TQC
    return pl.pallas_call(
        _fox_kernel,
        grid=(b, n_q + 1),
        in_specs=[
            pl.BlockSpec((1, 1, TQC, MIX_W), lambda bi, i: (0, bi, jnp.minimum(i, n_q - 1), 0)),
            pl.BlockSpec((1, 1, s, MIX_W), lambda bi, i: (1, bi, 0, 0)),
            pl.BlockSpec((1, 1, s, MIX_W), lambda bi, i: (2, bi, 0, 0)),
            pl.BlockSpec((1, 8, s), lambda bi, i: (bi, 0, 0)),
        ],
        out_specs=pl.BlockSpec((1, TQC, MIX_W), lambda bi, i: (bi, jnp.maximum(i - 1, 0), 0)),
        out_shape=jax.ShapeDtypeStruct((b, s, MIX_W), BF16),
        scratch_shapes=[
            pltpu.VMEM((s // TK, rows, TK), F32),
            pltpu.VMEM((s // TK, rows, TK), F32),
            pltpu.VMEM((rows, LANES), F32),
            pltpu.VMEM((rows, LANES), F32),
            pltpu.VMEM((rows, LANES), F32),
            pltpu.VMEM((rows, MIX_W), F32),
        ],
        compiler_params=_cparams("parallel", "arbitrary"),
        name="fox_attn",
    )(qkv, qkv, qkv, cum)


def _t5_bucket(dist):
    n = jnp.maximum(dist, 0)
    max_exact = NUM_BUCKETS // 2
    nf = jnp.maximum(n, 1).astype(F32)
    large = max_exact + jnp.floor(jnp.log(nf / max_exact) / math.log(MAX_DISTANCE / max_exact)
                                  * (NUM_BUCKETS - max_exact)).astype(jnp.int32)
    large = jnp.minimum(large, NUM_BUCKETS - 1)
    return jnp.where(n < max_exact, n, large)


def _bias_kernel(idx_ref, rb_ref, o_ref):
    for d in range(len(DILATIONS)):
        for variant in range(2):
            idx = idx_ref[d, variant]
            for h in range(N_HEADS):
                acc = jnp.full(idx.shape, NEG_BIG, F32)
                for bucket in range(NUM_BUCKETS):
                    acc = jnp.where(idx == bucket, rb_ref[bucket, h], acc)
                o_ref[d, variant, h] = acc


def _bias_tables(rel_bias):
    t = jnp.arange(TQ)[:, None]
    c = jnp.arange(2 * TQ)[None, :]
    delta = TQ + t - c
    valid = (delta >= 0) & (delta <= DIL_STEPS)
    idx = jnp.stack([jnp.where(valid, _t5_bucket(delta * dil), -1) for dil in DILATIONS])
    idx_first = jnp.concatenate([idx[:, :, TQ:], jnp.full_like(idx[:, :, TQ:], -1)], axis=-1)
    idx = jnp.stack([idx, idx_first], axis=1)
    out = pl.pallas_call(
        _bias_kernel,
        in_specs=[
            pl.BlockSpec(memory_space=pltpu.VMEM),
            pl.BlockSpec(memory_space=pltpu.SMEM),
        ],
        out_specs=pl.BlockSpec(memory_space=pltpu.VMEM),
        out_shape=jax.ShapeDtypeStruct((len(DILATIONS), 2, N_HEADS, TQ, 2 * TQ), F32),
        name="dil_bias",
    )(idx.astype(jnp.int32), rel_bias)
    return out.reshape(len(DILATIONS), 2, N_HEADS * TQ, 2 * TQ)


DIL_GROUP = 16


def _dil_kernel(q_ref, k_ref, v_ref, t_ref, o_ref, lse_ref, *, by_class, group, n_groups):
    g = pl.program_id(2)

    def attend(jj, scores, v_win):
        m = jnp.max(scores, axis=-1, keepdims=True)
        p = jnp.exp(scores - m)
        l = _unstack_cols(jnp.sum(p, axis=-1, keepdims=True), TQ)
        o = (_unstack_heads(_dot(p.astype(BF16), v_win), TQ) / l).astype(o_ref.dtype)
        lse = _unstack_cols(m, TQ) + jnp.log(l)
        if by_class:
            o_ref[0, jj] = o
            lse_ref[0, jj] = lse
        else:
            o_ref[0, 0, jj * TQ:(jj + 1) * TQ] = o
            lse_ref[0, 0, jj * TQ:(jj + 1) * TQ] = lse

    pending = []
    for jj in range(group):
        if by_class:
            scores = (_dot_nt(_stack_heads(q_ref[0, 0, jj]), k_ref[0, 0, jj])
                      + t_ref[0, 0, :, TQ:2 * TQ])
            pending.append((scores, v_ref[0, 0, jj]))
            continue
        qs = _stack_heads(q_ref[0, 0, 0, jj * TQ:(jj + 1) * TQ])
        if n_groups == 1 or jj > 0:
            n = g * group + jj if n_groups > 1 else jj
            start = (n - 1) * TQ if jj > 0 else 0
            bias = t_ref[0, 0 if jj > 0 else 1]
        else:
            start = jnp.maximum(g * group - 1, 0) * TQ
            bias = t_ref[0, (g == 0).astype(jnp.int32)]
        if not isinstance(start, int):
            start = pl.multiple_of(start, TQ)
        window = pl.ds(start, 2 * TQ)
        pending.append((_dot_nt(qs, k_ref[0, 0, 0, window, :]) + bias, v_ref[0, 0, 0, window, :]))
    for jj, (scores, v_win) in enumerate(pending):
        attend(jj, scores, v_win)


def _dil_branch(qkv, tables, branch):
    _, b, dil, cls_len, _ = qkv.shape
    n_blocks = cls_len // TQ
    by_class = n_blocks == 1
    group = min(DIL_GROUP, dil if by_class else n_blocks)
    if by_class:
        grid = (b, dil // group, 1)
        qkv_specs = [pl.BlockSpec((1, 1, group, TQ, MIX_W), functools.partial(
            lambda bi, c, g, j: (j, bi, c, 0, 0), j=j)) for j in range(3)]
        out_spec = pl.BlockSpec((1, group, TQ, MIX_W), lambda bi, c, g: (bi, c, 0, 0))
    else:
        grid = (b, dil, n_blocks // group)
        qkv_specs = [pl.BlockSpec((1, 1, 1, group * TQ, MIX_W), lambda bi, r, g: (0, bi, r, g, 0))]
        qkv_specs += [pl.BlockSpec((1, 1, 1, cls_len, MIX_W), functools.partial(
            lambda bi, r, g, j: (j, bi, r, 0, 0), j=j)) for j in (1, 2)]
        out_spec = pl.BlockSpec((1, 1, group * TQ, MIX_W), lambda bi, r, g: (bi, r, g, 0))
    o, lse = pl.pallas_call(
        functools.partial(_dil_kernel, by_class=by_class, group=group, n_groups=grid[2]),
        grid=grid,
        in_specs=qkv_specs + [
            pl.BlockSpec((1, 2, N_HEADS * TQ, 2 * TQ), lambda bi, r, g: (branch, 0, 0, 0))],
        out_specs=[out_spec, out_spec],
        out_shape=[jax.ShapeDtypeStruct((b, dil, cls_len, MIX_W), BF16),
                   jax.ShapeDtypeStruct((b, dil, cls_len, MIX_W), F32)],
        compiler_params=_cparams("parallel", "parallel", "arbitrary"),
        name=f"dil_attn_{dil}",
    )(qkv, qkv, qkv, tables)
    to_seq = lambda a: a.transpose(0, 2, 1, 3).reshape(b, dil * cls_len, MIX_W)
    return to_seq(o), to_seq(lse)


def _dil_mixture(outs, lses):
    m = functools.reduce(jnp.maximum, lses)
    es = [jnp.exp(a - m) for a in lses]
    tot = functools.reduce(jnp.add, es)
    return functools.reduce(jnp.add, [(e / tot) * o for e, o in zip(es, outs)])


def _shift_rows(x, k, fill):
    row = lax.broadcasted_iota(jnp.int32, x.shape, 0)
    return jnp.where(row >= k, pltpu.roll(x, k, 0), fill)


def _lru_kernel(xg_ref, cw_ref, cb_ref, wa_ref, ba_ref, wx_ref, bx_ref, lam_ref, o_ref, h_ref):
    x = xg_ref[0, 0]
    gate = xg_ref[1, 0]
    s = x.shape[0]
    kw = cw_ref.shape[0]
    xc = cb_ref[...] + cw_ref[kw - 1:kw, :] * x
    for j in range(kw - 1):
        xc = xc + cw_ref[j:j + 1, :] * _shift_rows(x, kw - 1 - j, 0.0)
    xcb = xc.astype(BF16)
    r = jax.nn.sigmoid(_dot(xcb, wa_ref[...]) + ba_ref[...])
    i_gate = jax.nn.sigmoid(_dot(xcb, wx_ref[...]) + bx_ref[...])
    a = jnp.exp(-LRU_C * r * _softplus(-lam_ref[...]))
    h = jnp.sqrt(1.0 - a * a) * (i_gate * xc)
    grouped = (s // SUBLANES, SUBLANES, x.shape[1])
    h, a = h.reshape(grouped), a.reshape(grouped)
    in_group = lax.broadcasted_iota(jnp.int32, grouped, 1)
    k = 1
    while k < SUBLANES:
        prev = in_group >= k
        h = a * jnp.where(prev, pltpu.roll(h, k, 1), 0.0) + h
        a = a * jnp.where(prev, pltpu.roll(a, k, 1), 1.0)
        k *= 2
    h, a = h.reshape(x.shape), a.reshape(x.shape)
    carry = jnp.zeros((1, x.shape[1]), F32)
    for c in range(s // SUBLANES):
        rows = slice(c * SUBLANES, (c + 1) * SUBLANES)
        hc = h[rows] + a[rows] * carry
        h_ref[rows] = hc
        carry = hc[SUBLANES - 1:SUBLANES]
    o_ref[0] = (h_ref[...] * _gelu(gate)).astype(o_ref.dtype)


def _lru(xg, conv_w, conv_b, wa_bd, b_a, wx_bd, b_x, lam):
    _, b, s, w = xg.shape
    full = lambda a: pl.BlockSpec(a.shape, lambda bi: (0,) * a.ndim)
    args = (conv_w, conv_b, wa_bd, b_a, wx_bd, b_x, lam)
    return pl.pallas_call(
        _lru_kernel,
        grid=(b,),
        in_specs=[pl.BlockSpec((2, 1, s, w), lambda bi: (0, bi, 0, 0))] + [full(a) for a in args],
        out_specs=pl.BlockSpec((1, s, w), lambda bi: (bi, 0, 0)),
        out_shape=jax.ShapeDtypeStruct((b, s, w), BF16),
        scratch_shapes=[pltpu.VMEM((s, w), F32)],
        compiler_params=_cparams("parallel"),
        name="rg_lru",
    )(xg, *args)


def _memkv_kernel(mem_ref, g_ref, wk_ref, wv_ref, k_ref, v_ref):
    mn = _rms(mem_ref[0], g_ref[...]).astype(BF16)
    k_ref[0] = _dot(mn, wk_ref[...]).astype(BF16)
    v_ref[0] = _dot(mn, wv_ref[...]).astype(BF16)


def _memkv(mem, g, w_ck, w_cv):
    b, m, d = mem.shape
    out = jax.ShapeDtypeStruct((b, m, MIX_W), BF16)
    return pl.pallas_call(
        _memkv_kernel,
        grid=(b,),
        in_specs=[
            pl.BlockSpec((1, m, d), lambda bi: (bi, 0, 0)),
            pl.BlockSpec((1, d), lambda bi: (0, 0)),
            pl.BlockSpec((d, MIX_W), lambda bi: (0, 0)),
            pl.BlockSpec((d, MIX_W), lambda bi: (0, 0)),
        ],
        out_specs=[pl.BlockSpec((1, m, MIX_W), lambda bi: (bi, 0, 0))] * 2,
        out_shape=[out, out],
        compiler_params=_cparams("parallel"),
        name="mem_kv",
    )(mem, g, w_ck, w_cv)


def _mix_cross_kernel(x_ref, sb_ref, fox_ref, lru_ref, o1_ref, l1_ref, o4_ref, l4_ref,
                      o16_ref, l16_ref, wo_ref, g_ref, wq_ref, km_ref, vm_ref, wco_ref, o_ref):
    t = MIX_SUB
    subs = [slice(u * t, (u + 1) * t) for u in range(x_ref.shape[1] // t)]
    xs = []
    for r in subs:
        dil = _dil_mixture([ref[0, r] for ref in (o1_ref, o4_ref, o16_ref)],
                           [ref[0, r] for ref in (l1_ref, l4_ref, l16_ref)]).astype(BF16)
        mixed = jnp.concatenate([sb_ref[0, r], fox_ref[0, r], dil, lru_ref[0, r]], axis=1)
        xs.append(x_ref[0, r] + _dot(mixed, wo_ref[...]))
    qs = [_stack_heads(_dot(_rms(x, g_ref[...]).astype(BF16), wq_ref[...])) for x in xs]
    scores = [_dot_nt(q, km_ref[0]) for q in qs]
    outs = []
    for sc in scores:
        p = jnp.exp(sc - jnp.max(sc, axis=-1, keepdims=True))
        p = p / jnp.sum(p, axis=-1, keepdims=True)
        outs.append(_unstack_heads(_dot(p.astype(BF16), vm_ref[0]), t))
    for r, x, o in zip(subs, xs, outs):
        o_ref[0, r] = x + _dot(o.astype(BF16), wco_ref[...])


MIX_SUB = 128


def _mix_cross(x, o_sb, o_fox, o_lru, dil_branches, w_out, g, w_cq, k_mem, v_mem, w_co, ts=1024):
    b, s, d = x.shape
    m = k_mem.shape[1]
    mix_spec = pl.BlockSpec((1, ts, MIX_W), lambda bi, i: (bi, i, 0))
    const = lambda a: pl.BlockSpec(a.shape, lambda bi, i: (0,) * a.ndim)
    mem_spec = pl.BlockSpec((1, m, MIX_W), lambda bi, i: (bi, 0, 0))
    dil_flat = [a for pair in dil_branches for a in pair]
    return pl.pallas_call(
        _mix_cross_kernel,
        grid=(b, s // ts),
        in_specs=[pl.BlockSpec((1, ts, d), lambda bi, i: (bi, i, 0))] + [mix_spec] * (3 + len(dil_flat))
                 + [const(w_out), const(g), const(w_cq), mem_spec, mem_spec, const(w_co)],
        out_specs=pl.BlockSpec((1, ts, d), lambda bi, i: (bi, i, 0)),
        out_shape=jax.ShapeDtypeStruct((b, s, d), F32),
        compiler_params=_cparams("parallel", "parallel"),
        name="mix_cross",
    )(x, o_sb, o_fox, o_lru, *dil_flat, w_out, g, w_cq, k_mem, v_mem, w_co)


HALO = 8
FFN_PROJ_LEAD = 2
FFN_EXT_SLOTS = FFN_PROJ_LEAD + 1
FFN_DOWN_LAG = 2
FFN_ACT_SLOTS = FFN_DOWN_LAG + 1


def _ffn_kernel(x_ref, g_ref, wup_ref, cw_ref, cb_ref, wdn_ref, gf_ref, o_ref,
                hn_ref, ext_ref, tail_ref, act_ref, acc_ref, *, final_norm):
    i = pl.program_id(1)
    x = x_ref[0]
    t = x.shape[0]
    hn_ref[...] = _rms(x, g_ref[...]).astype(BF16)

    @pl.when(i == 0)
    def _():
        tail_ref[...] = jnp.zeros_like(tail_ref)

    def project(c):
        for half in range(2):
            blk = half * N_FF_CHUNKS + c
            slot = 2 * (c % FFN_EXT_SLOTS) + half
            pre = _dot(hn_ref[...], wup_ref[:, blk * FF_CHUNK:(blk + 1) * FF_CHUNK])
            ext_ref[slot, 0:HALO] = tail_ref[blk]
            ext_ref[slot, HALO:] = pre
            tail_ref[blk] = pre[t - HALO:]

    def conv(c, slot):
        ext = ext_ref.at[slot]
        w = cw_ref[c]
        return (cb_ref[c] + w[2:3] * ext[pl.ds(HALO, t), :]
                + w[1:2] * ext[pl.ds(HALO - 1, t), :]
                + w[0:1] * ext[pl.ds(HALO - 2, t), :])

    def down(c):
        part = _dot(act_ref[c % FFN_ACT_SLOTS], wdn_ref[c])
        acc_ref[...] = part if c == 0 else acc_ref[...] + part

    for c in range(FFN_PROJ_LEAD):
        project(c)
    for c in range(N_FF_CHUNKS):
        if c + FFN_PROJ_LEAD < N_FF_CHUNKS:
            project(c + FFN_PROJ_LEAD)
        if c >= FFN_DOWN_LAG:
            down(c - FFN_DOWN_LAG)
        up = conv(c, 2 * (c % FFN_EXT_SLOTS))
        gate = conv(N_FF_CHUNKS + c, 2 * (c % FFN_EXT_SLOTS) + 1)
        act_ref[c % FFN_ACT_SLOTS] = (_gelu(gate) * up).astype(BF16)
    for c in range(N_FF_CHUNKS - FFN_DOWN_LAG, N_FF_CHUNKS):
        down(c)
    out = x + acc_ref[...]
    if final_norm:
        out = _rms(out, gf_ref[...])
    o_ref[0] = out


def _ffn(x, g, w_up, conv_w, conv_b, w_down, g_final, final_norm, ts=256):
    b, s, d = x.shape
    const = lambda a: pl.BlockSpec(a.shape, lambda bi, i: (0,) * a.ndim)
    return pl.pallas_call(
        functools.partial(_ffn_kernel, final_norm=final_norm),
        grid=(b, s // ts),
        in_specs=[
            pl.BlockSpec((1, ts, d), lambda bi, i: (bi, i, 0)),
            const(g), const(w_up), const(conv_w), const(conv_b), const(w_down), const(g_final),
        ],
        out_specs=pl.BlockSpec((1, ts, d), lambda bi, i: (bi, i, 0)),
        out_shape=jax.ShapeDtypeStruct((b, s, d), F32),
        scratch_shapes=[
            pltpu.VMEM((ts, d), BF16),
            pltpu.VMEM((2 * FFN_EXT_SLOTS, HALO + ts, FF_CHUNK), F32),
            pltpu.VMEM((2 * N_FF_CHUNKS, HALO, FF_CHUNK), F32),
            pltpu.VMEM((FFN_ACT_SLOTS, ts, FF_CHUNK), BF16),
            pltpu.VMEM((ts, d), F32),
        ],
        compiler_params=_cparams("parallel", "arbitrary"),
        name="conv_geglu",
    )(x, g, w_up, conv_w, conv_b, w_down, g_final)


def _block_diag(w):
    g, n, _ = w.shape
    eye = jnp.eye(g, dtype=w.dtype)
    return (eye[:, None, :, None] * w[:, :, None, :]).reshape(g * n, g * n)


def kernel(x, mem, norm_mix_g, w_in, b_forget, lru_conv_w, lru_conv_b, lru_w_a, lru_b_a, lru_w_x, lru_b_x, lru_lambda, w_out, norm_cross_g, norm_mem_g, w_cq, w_ck, w_cv, w_co, norm_ffn_g, w_up, ffn_conv_w, ffn_conv_b, w_down, rel_bias, final_norm_g):
    depth = w_in.shape[0]
    qkv_w = 3 * MIX_W
    q_scale = HEAD_DIM ** -0.5
    col_scale = jnp.ones((3, 3, MIX_W), F32).at[:, 0, :].set(q_scale).reshape(-1)
    col_scale = jnp.concatenate([col_scale, jnp.ones((2 * MIX_W,), F32)])
    tables = _bias_tables(rel_bias)
    row = lambda v: v.reshape(1, -1)

    for l in range(depth):
        w = w_in[l]
        n_f = N_HEADS
        w_main = jnp.concatenate([w[:, :2 * qkv_w], w[:, 2 * qkv_w + n_f:]], axis=1)
        w_main = (w_main * col_scale).astype(BF16)
        w_f = jnp.zeros((8, D_MODEL), F32).at[:n_f].set(w[:, 2 * qkv_w:2 * qkv_w + n_f].T).astype(BF16)
        b_f = jnp.zeros((8, 1), F32).at[:n_f, 0].set(b_forget[l])

        sb_qkv, fox_qkv, *dil_qkvs, lru_xg, f_t = _inproj(x, row(norm_mix_g[l]), w_main, w_f)
        o_sb = _sb_attention(sb_qkv)
        o_fox = _fox_attention(fox_qkv, _decay(f_t, b_f))
        dil_branches = [_dil_branch(qkv, tables, br) for br, qkv in enumerate(dil_qkvs)]
        o_lru = _lru(lru_xg, lru_conv_w[l], row(lru_conv_b[l]),
                     _block_diag(lru_w_a[l]).astype(BF16), row(lru_b_a[l]),
                     _block_diag(lru_w_x[l]).astype(BF16), row(lru_b_x[l]), row(lru_lambda[l]))

        k_mem, v_mem = _memkv(mem, row(norm_mem_g[l]), w_ck[l].astype(BF16), w_cv[l].astype(BF16))
        x = _mix_cross(x, o_sb, o_fox, o_lru, dil_branches,
                       w_out[l].astype(BF16), row(norm_cross_g[l]),
                       (w_cq[l] * q_scale).astype(BF16), k_mem, v_mem, w_co[l].astype(BF16))

        wu = w_up[l].astype(BF16)
        cw = ffn_conv_w[l].reshape(-1, 2 * N_FF_CHUNKS, FF_CHUNK).transpose(1, 0, 2)
        cb = ffn_conv_b[l].reshape(2 * N_FF_CHUNKS, 1, FF_CHUNK)
        wd = w_down[l].reshape(N_FF_CHUNKS, FF_CHUNK, D_MODEL).astype(BF16)
        x = _ffn(x, row(norm_ffn_g[l]), wu, cw, cb, wd, row(final_norm_g), l == depth - 1)
    return x
```

```python
import functools
import math

import jax
import jax.numpy as jnp
from jax import lax
from jax.experimental import pallas as pl
from jax.experimental.pallas import tpu as pltpu

F32 = jnp.float32
BF16 = jnp.bfloat16

LANES = 128
SUBLANES = 8
D_MODEL = 1024
HEAD_DIM = 64
N_HEADS = 4
MIX_W = N_HEADS * HEAD_DIM
D_FF = 2816
FF_CHUNK = 256
N_FF_CHUNKS = D_FF // FF_CHUNK
N_MEM = 256
NUM_BUCKETS = 32
MAX_DISTANCE = 2048
DILATIONS = (1, 4, 16)
DIL_STEPS = 128
LRU_C = 8.0
EPS = 1e-6
NEG_BIG = -1e30

TQ = 128
TQC = 256
TK = 256
VMEM_LIMIT = 56 * 1024 * 1024

NT_DIMS = (((1,), (1,)), ((), ()))


def _cparams(*sem):
    return pltpu.CompilerParams(dimension_semantics=sem, vmem_limit_bytes=VMEM_LIMIT)


def _rms(x, g):
    return x * lax.rsqrt(jnp.mean(x * x, axis=-1, keepdims=True) + EPS) * g


def _softplus(x):
    return jnp.maximum(x, 0.0) + jnp.log(1.0 + jnp.exp(-jnp.abs(x)))


def _gelu(x):
    return 0.5 * x * (1.0 + lax.erf(x * math.sqrt(0.5)))


def _dot(a, b):
    return jnp.dot(a, b, preferred_element_type=F32)


def _dot_nt(a, b):
    return lax.dot_general(a, b, NT_DIMS, preferred_element_type=F32)


def _head_of_lane(rows):
    return lax.broadcasted_iota(jnp.int32, (rows, MIX_W), 1) // HEAD_DIM


def _stack_heads(q):
    t = q.shape[0]
    head = _head_of_lane(t)
    qf = q.astype(F32)
    return jnp.concatenate([jnp.where(head == h, qf, 0.0) for h in range(N_HEADS)], axis=0).astype(BF16)


def _unstack_heads(o, t):
    head = _head_of_lane(t)
    out = jnp.where(head == 0, o[0:t], 0.0)
    for h in range(1, N_HEADS):
        out = jnp.where(head == h, o[h * t:(h + 1) * t], out)
    return out


def _unstack_cols(c, t):
    head = _head_of_lane(t)
    out = jnp.broadcast_to(c[0:t], (t, MIX_W))
    for h in range(1, N_HEADS):
        out = jnp.where(head == h, jnp.broadcast_to(c[h * t:(h + 1) * t], (t, MIX_W)), out)
    return out


def _inproj_kernel(x_ref, g_ref, w_ref, wf_ref, sb_ref, fox_ref, d1_ref, d4_ref, d16_ref,
                   lru_ref, f_ref, hn_ref, half_ref):
    hn_ref[...] = _rms(x_ref[0], g_ref[...]).astype(BF16)
    ts = hn_ref.shape[0]

    def project(rows, block):
        return _dot(hn_ref[rows], w_ref[:, block * MIX_W:(block + 1) * MIX_W])

    for j in range(3):
        res = project(slice(None), 6 + j)
        d1_ref[j, 0, 0] = res.astype(BF16)
        for h in range(2):
            half_ref[j, h] = res[:, h * LANES:(h + 1) * LANES]

    def regroup(j, ref, dil):
        for r in range(dil):
            for h in range(2):
                ref[j, 0, r, :, h * LANES:(h + 1) * LANES] = (
                    half_ref[j, h, pl.ds(r, ts // dil, stride=dil), :].astype(BF16))

    pending = [(j, ref, dil) for j in range(3)
               for ref, dil in ((d4_ref, DILATIONS[1]), (d16_ref, DILATIONS[2]))]
    for m, ref in enumerate((sb_ref, fox_ref)):
        for j in range(3):
            ref[j, 0] = project(slice(None), 3 * m + j).astype(BF16)
            regroup(*pending.pop(0))
    lru_ref[0, 0] = project(slice(None), 9)
    for rows in (slice(0, ts // 2), slice(ts // 2, ts)):
        lru_ref[1, 0, rows] = project(rows, 10)
    f_ref[0] = _dot_nt(wf_ref[...], hn_ref[...])


def _inproj(x, g, w_main, w_f, ts=1024):
    b, s, d = x.shape
    qkv_shape = jax.ShapeDtypeStruct((3, b, s, MIX_W), BF16)
    qkv_spec = pl.BlockSpec((3, 1, ts, MIX_W), lambda bi, i: (0, bi, i, 0))
    cls_shapes = [jax.ShapeDtypeStruct((3, b, dil, s // dil, MIX_W), BF16) for dil in DILATIONS]
    cls_specs = [pl.BlockSpec((3, 1, dil, ts // dil, MIX_W), lambda bi, i: (0, bi, 0, i, 0))
                 for dil in DILATIONS]
    return pl.pallas_call(
        _inproj_kernel,
        grid=(b, s // ts),
        in_specs=[
            pl.BlockSpec((1, ts, d), lambda bi, i: (bi, i, 0)),
            pl.BlockSpec((1, d), lambda bi, i: (0, 0)),
            pl.BlockSpec(w_main.shape, lambda bi, i: (0, 0)),
            pl.BlockSpec(w_f.shape, lambda bi, i: (0, 0)),
        ],
        out_specs=[
            qkv_spec, qkv_spec, *cls_specs,
            pl.BlockSpec((2, 1, ts, MIX_W), lambda bi, i: (0, bi, i, 0)),
            pl.BlockSpec((1, 8, ts), lambda bi, i: (bi, 0, i)),
        ],
        out_shape=[
            qkv_shape, qkv_shape, *cls_shapes,
            jax.ShapeDtypeStruct((2, b, s, MIX_W), F32),
            jax.ShapeDtypeStruct((b, 8, s), F32),
        ],
        scratch_shapes=[pltpu.VMEM((ts, d), BF16), pltpu.VMEM((3, 2, ts, LANES), F32)],
        compiler_params=_cparams("parallel", "parallel"),
        name="inproj",
    )(x, g, w_main, w_f)


def _decay_kernel(f_ref, bf_ref, tri_ref, o_ref):
    s = f_ref.shape[2]
    tri = tri_ref[...]
    carry = jnp.zeros((8, 1), F32)
    for c in range(s // 128):
        sl = slice(c * 128, (c + 1) * 128)
        log_f = -_softplus(-(f_ref[0, :, sl] + bf_ref[...]))
        p1 = log_f.astype(BF16)
        r1 = log_f - p1.astype(F32)
        p2 = r1.astype(BF16)
        p3 = (r1 - p2.astype(F32)).astype(BF16)
        inc = _dot(p1, tri) + _dot(p2, tri) + _dot(p3, tri)
        o_ref[0, :, sl] = inc + carry
        carry = carry + jnp.sum(log_f, axis=-1, keepdims=True)


def _decay(f_t, b_f):
    b, _, s = f_t.shape
    tri = (lax.broadcasted_iota(jnp.int32, (128, 128), 0)
           <= lax.broadcasted_iota(jnp.int32, (128, 128), 1)).astype(BF16)
    return pl.pallas_call(
        _decay_kernel,
        grid=(b,),
        in_specs=[
            pl.BlockSpec((1, 8, s), lambda bi: (bi, 0, 0)),
            pl.BlockSpec((8, 1), lambda bi: (0, 0)),
            pl.BlockSpec((128, 128), lambda bi: (0, 0)),
        ],
        out_specs=pl.BlockSpec((1, 8, s), lambda bi: (bi, 0, 0)),
        out_shape=jax.ShapeDtypeStruct((b, 8, s), F32),
        compiler_params=_cparams("parallel"),
        name="fox_decay",
    )(f_t, b_f, tri)


def _block_positions(i, ks):
    row = lax.broadcasted_iota(jnp.int32, (N_HEADS * TQC, TK), 0) % TQC + i * TQC
    col = lax.broadcasted_iota(jnp.int32, (N_HEADS * TQC, TK), 1) + ks
    return row, col


SB_SATURATED = 110.0


def _sb_kernel(q_ref, k_ref, v_ref, uo_ref, o_ref, acc_ref, car_ref):
    i = pl.program_id(1)
    qs = _stack_heads(q_ref[0, 0])

    half_rows = N_HEADS * TQC // 2
    halves = [slice(0, half_rows), slice(half_rows, 2 * half_rows)]

    def process(blocks):
        items = []
        for kb, diagonal in blocks:
            ks = pl.multiple_of(kb * TK, TK)
            k_blk = k_ref[0, 0, pl.ds(ks, TK), :]
            v_blk = v_ref[0, 0, pl.ds(ks, TK), :]
            strict = None
            if diagonal:
                row, col = _block_positions(i, ks)
                strict = (col < row)[:half_rows]
            items += [(r, k_blk, v_blk, strict) for r in halves]
        zs = [_dot_nt(qs[r], k_blk) for r, k_blk, _, _ in items]
        staged = []
        for (r, _, _, strict), z in zip(items, zs):
            drop = _softplus(z)
            cum_in = drop if strict is None else jnp.where(strict, drop, 0.0)
            staged.append((drop, _dot(cum_in.astype(BF16), uo_ref[...])))
        for (r, _, v_blk, strict), z, (drop, sm) in zip(items, zs, staged):
            if strict is not None:
                att = jnp.where(strict, jnp.exp(z - drop - sm[:, :TK]), 0.0)
                car_ref[r] = sm[:, TK:]
                acc_ref[r] = _dot(att.astype(BF16), v_blk)
            else:
                car = car_ref[r]
                att = jnp.exp(z - drop - sm[:, :TK] - jnp.concatenate([car] * (TK // LANES), axis=1))
                car_ref[r] = car + sm[:, TK:]
                acc_ref[r] += _dot(att.astype(BF16), v_blk)

    def live():
        return jnp.min(car_ref[...]) < SB_SATURATED

    pl.when(i == 0)(lambda: process([(i, True)]))
    pl.when(i > 0)(lambda: process([(i, True), (i - 1, False)]))

    def cond(state):
        kb, go = state
        return jnp.logical_and(kb >= 0, go)

    def body(state):
        kb, _ = state
        process([(kb, False)])
        return kb - 1, live()

    lax.while_loop(cond, body, (i - 2, live()))
    o_ref[0] = _unstack_heads(acc_ref[...], TQC).astype(o_ref.dtype)


def _sb_attention(qkv):
    _, b, s, _ = qkv.shape
    assert TQC == TK
    later =(lax.broadcasted_iota(jnp.int32, (TK, TK), 0)
             > lax.broadcasted_iota(jnp.int32, (TK, TK), 1))
    u = jnp.concatenate([later, jnp.ones((TK, LANES), bool)], axis=1).astype(BF16)
    return pl.pallas_call(
        _sb_kernel,
        grid=(b, s // TQC),
        in_specs=[
            pl.BlockSpec((1, 1, TQC, MIX_W), lambda bi, i: (0, bi, i, 0)),
            pl.BlockSpec((1, 1, s, MIX_W), lambda bi, i: (1, bi, 0, 0)),
            pl.BlockSpec((1, 1, s, MIX_W), lambda bi, i: (2, bi, 0, 0)),
            pl.BlockSpec((TK, TK + LANES), lambda bi, i: (0, 0)),
        ],
        out_specs=pl.BlockSpec((1, TQC, MIX_W), lambda bi, i: (bi, i, 0)),
        out_shape=jax.ShapeDtypeStruct((b, s, MIX_W), BF16),
        scratch_shapes=[
            pltpu.VMEM((N_HEADS * TQC, MIX_W), F32),
            pltpu.VMEM((N_HEADS * TQC, LANES), F32),
        ],
        compiler_params=_cparams("parallel", "arbitrary"),
        name="sb_attn",
    )(qkv, qkv, qkv, u)


def _fox_kernel(q_ref, k_ref, v_ref, cum_ref, o_ref, za_ref, zb_ref, ma_ref, mb_ref, l_ref, acc_ref):
    i = pl.program_id(1)
    n_q = pl.num_programs(1) - 1

    def front_block(qs, kb, diagonal, z_ref, mx_ref):
        ks = pl.multiple_of(kb * TK, TK)
        cum = cum_ref[0, :, pl.ds(ks, TK)]
        decay = jnp.concatenate(
            [jnp.broadcast_to(cum[h:h + 1], (TQC, TK)) for h in range(N_HEADS)], axis=0)
        z = _dot_nt(qs, k_ref[0, 0, pl.ds(ks, TK), :]) - decay
        if diagonal:
            row, col = _block_positions(i, ks)
            z = jnp.where(col <= row, z, NEG_BIG)
        z_ref[kb] = z
        zmax = jnp.maximum(z[:, :LANES], z[:, LANES:])
        mx_ref[...] = zmax if diagonal else jnp.maximum(mx_ref[...], zmax)

    def back_block(kb, z_ref, mx_ref):
        ks = pl.multiple_of(kb * TK, TK)
        m = mx_ref[...]
        p = jnp.exp(z_ref[kb] - jnp.concatenate([m, m], axis=1))
        l_ref[...] += p[:, :LANES] + p[:, LANES:]
        acc_ref[...] += _dot(p.astype(BF16), v_ref[0, 0, pl.ds(ks, TK), :])

    def step(z_cur, m_cur, z_prev, m_prev):
        @pl.when(i < n_q)
        def _():
            qs = _stack_heads(q_ref[0, 0])
            front_block(qs, i, True, z_cur, m_cur)

            def both(kb):
                front_block(qs, kb, False, z_cur, m_cur)
                back_block(kb, z_prev, m_prev)

            def pair(j, carry):
                both(2 * j)
                both(2 * j + 1)
                return carry

            lax.fori_loop(0, i // 2, pair, 0)
            pl.when(i % 2 == 1)(lambda: both(i - 1))
            m = jnp.max(m_cur[...], axis=-1, keepdims=True)
            m_cur[...] = jnp.broadcast_to(m, m_cur.shape)

        @pl.when(i == n_q)
        def _():
            def back_only(kb, carry):
                back_block(kb, z_prev, m_prev)
                return carry

            lax.fori_loop(0, i, back_only, 0)

    l_ref[...] = jnp.zeros_like(l_ref)
    acc_ref[...] = jnp.zeros_like(acc_ref)
    pl.when(i % 2 == 0)(functools.partial(step, za_ref, ma_ref, zb_ref, mb_ref))
    pl.when(i % 2 == 1)(functools.partial(step, zb_ref, mb_ref, za_ref, ma_ref))

    @pl.when(i > 0)
    def _():
        l = _unstack_cols(jnp.sum(l_ref[...], axis=-1, keepdims=True), TQC)
        o_ref[0] = (_unstack_heads(acc_ref[...], TQC) / l).astype(o_ref.dtype)


def _fox_attention(qkv, cum):
    _, b, s, _ = qkv.shape
    assert TQC == TK
    n_q = s // TQC
    rows = N_HEADS * TQC
    return pl.pallas_call(
        _fox_kernel,
        grid=(b, n_q + 1),
        in_specs=[
            pl.BlockSpec((1, 1, TQC, MIX_W), lambda bi, i: (0, bi, jnp.minimum(i, n_q - 1), 0)),
            pl.BlockSpec((1, 1, s, MIX_W), lambda bi, i: (1, bi, 0, 0)),
            pl.BlockSpec((1, 1, s, MIX_W), lambda bi, i: (2, bi, 0, 0)),
            pl.BlockSpec((1, 8, s), lambda bi, i: (bi, 0, 0)),
        ],
        out_specs=pl.BlockSpec((1, TQC, MIX_W), lambda bi, i: (bi, jnp.maximum(i - 1, 0), 0)),
        out_shape=jax.ShapeDtypeStruct((b, s, MIX_W), BF16),
        scratch_shapes=[
            pltpu.VMEM((s // TK, rows, TK), F32),
            pltpu.VMEM((s // TK, rows, TK), F32),
            pltpu.VMEM((rows, LANES), F32),
            pltpu.VMEM((rows, LANES), F32),
            pltpu.VMEM((rows, LANES), F32),
            pltpu.VMEM((rows, MIX_W), F32),
        ],
        compiler_params=_cparams("parallel", "arbitrary"),
        name="fox_attn",
    )(qkv, qkv, qkv, cum)


def _t5_bucket(dist):
    n = jnp.maximum(dist, 0)
    max_exact = NUM_BUCKETS // 2
    nf = jnp.maximum(n, 1).astype(F32)
    large = max_exact + jnp.floor(jnp.log(nf / max_exact) / math.log(MAX_DISTANCE / max_exact)
                                  * (NUM_BUCKETS - max_exact)).astype(jnp.int32)
    large = jnp.minimum(large, NUM_BUCKETS - 1)
    return jnp.where(n < max_exact, n, large)


def _bias_kernel(idx_ref, rb_ref, o_ref):
    for d in range(len(DILATIONS)):
        for variant in range(2):
            idx = idx_ref[d, variant]
            for h in range(N_HEADS):
                acc = jnp.full(idx.shape, NEG_BIG, F32)
                for bucket in range(NUM_BUCKETS):
                    acc = jnp.where(idx == bucket, rb_ref[bucket, h], acc)
                o_ref[d, variant, h] = acc


def _bias_tables(rel_bias):
    t = jnp.arange(TQ)[:, None]
    c = jnp.arange(2 * TQ)[None, :]
    delta = TQ + t - c
    valid = (delta >= 0) & (delta <= DIL_STEPS)
    idx = jnp.stack([jnp.where(valid, _t5_bucket(delta * dil), -1) for dil in DILATIONS])
    idx_first = jnp.concatenate([idx[:, :, TQ:], jnp.full_like(idx[:, :, TQ:], -1)], axis=-1)
    idx = jnp.stack([idx, idx_first], axis=1)
    out = pl.pallas_call(
        _bias_kernel,
        in_specs=[
            pl.BlockSpec(memory_space=pltpu.VMEM),
            pl.BlockSpec(memory_space=pltpu.SMEM),
        ],
        out_specs=pl.BlockSpec(memory_space=pltpu.VMEM),
        out_shape=jax.ShapeDtypeStruct((len(DILATIONS), 2, N_HEADS, TQ, 2 * TQ), F32),
        name="dil_bias",
    )(idx.astype(jnp.int32), rel_bias)
    return out.reshape(len(DILATIONS), 2, N_HEADS * TQ, 2 * TQ)


DIL_GROUP = 16


def _dil_kernel(q_ref, k_ref, v_ref, t_ref, o_ref, lse_ref, *, by_class, classes, group, n_groups):
    g = pl.program_id(2)

    def attend(c, jj, scores, v_win):
        m = jnp.max(scores, axis=-1, keepdims=True)
        p = jnp.exp(scores - m)
        l = _unstack_cols(jnp.sum(p, axis=-1, keepdims=True), TQ)
        o = (_unstack_heads(_dot(p.astype(BF16), v_win), TQ) / l).astype(o_ref.dtype)
        lse = _unstack_cols(m, TQ) + jnp.log(l)
        if by_class:
            o_ref[0, jj] = o
            lse_ref[0, jj] = lse
        else:
            o_ref[0, c, jj * TQ:(jj + 1) * TQ] = o
            lse_ref[0, c, jj * TQ:(jj + 1) * TQ] = lse

    pending = []
    for c, jj in [(c, jj) for c in range(classes) for jj in range(group)]:
        if by_class:
            scores = (_dot_nt(_stack_heads(q_ref[0, 0, jj]), k_ref[0, 0, jj])
                      + t_ref[0, 0, :, TQ:2 * TQ])
            pending.append((c, jj, scores, v_ref[0, 0, jj]))
            continue
        qs = _stack_heads(q_ref[0, 0, c, jj * TQ:(jj + 1) * TQ])
        if n_groups == 1 or jj > 0:
            n = g * group + jj if n_groups > 1 else jj
            start = (n - 1) * TQ if jj > 0 else 0
            bias = t_ref[0, 0 if jj > 0 else 1]
        else:
            start = jnp.maximum(g * group - 1, 0) * TQ
            bias = t_ref[0, (g == 0).astype(jnp.int32)]
        if not isinstance(start, int):
            start = pl.multiple_of(start, TQ)
        window = pl.ds(start, 2 * TQ)
        pending.append((c, jj, _dot_nt(qs, k_ref[0, 0, c, window, :]) + bias,
                        v_ref[0, 0, c, window, :]))
    for item in pending:
        attend(*item)


def _dil_branch(qkv, tables, branch):
    _, b, dil, cls_len, _ = qkv.shape
    n_blocks = cls_len // TQ
    by_class = n_blocks == 1
    group = min(DIL_GROUP, dil if by_class else n_blocks)
    classes = 1 if by_class else min(dil, DIL_GROUP // group)
    if by_class:
        grid = (b, dil // group, 1)
        qkv_specs = [pl.BlockSpec((1, 1, group, TQ, MIX_W), functools.partial(
            lambda bi, c, g, j: (j, bi, c, 0, 0), j=j)) for j in range(3)]
        out_spec = pl.BlockSpec((1, group, TQ, MIX_W), lambda bi, c, g: (bi, c, 0, 0))
    else:
        grid = (b, dil // classes, n_blocks // group)
        qkv_specs = [pl.BlockSpec((1, 1, classes, group * TQ, MIX_W),
                                  lambda bi, r, g: (0, bi, r, g, 0))]
        qkv_specs += [pl.BlockSpec((1, 1, classes, cls_len, MIX_W), functools.partial(
            lambda bi, r, g, j: (j, bi, r, 0, 0), j=j)) for j in (1, 2)]
        out_spec = pl.BlockSpec((1, classes, group * TQ, MIX_W), lambda bi, r, g: (bi, r, g, 0))
    o, lse = pl.pallas_call(
        functools.partial(_dil_kernel, by_class=by_class, classes=classes, group=group,
                          n_groups=grid[2]),
        grid=grid,
        in_specs=qkv_specs + [
            pl.BlockSpec((1, 2, N_HEADS * TQ, 2 * TQ), lambda bi, r, g: (branch, 0, 0, 0))],
        out_specs=[out_spec, out_spec],
        out_shape=[jax.ShapeDtypeStruct((b, dil, cls_len, MIX_W), BF16),
                   jax.ShapeDtypeStruct((b, dil, cls_len, MIX_W), F32)],
        compiler_params=_cparams("parallel", "parallel", "arbitrary"),
        name=f"dil_attn_{dil}",
    )(qkv, qkv, qkv, tables)
    to_seq = lambda a: a.transpose(0, 2, 1, 3).reshape(b, dil * cls_len, MIX_W)
    return to_seq(o), to_seq(lse)


def _dil_mixture(outs, lses):
    m = functools.reduce(jnp.maximum, lses)
    es = [jnp.exp(a - m) for a in lses]
    tot = functools.reduce(jnp.add, es)
    return functools.reduce(jnp.add, [(e / tot) * o for e, o in zip(es, outs)])


def _shift_rows(x, k, fill):
    row = lax.broadcasted_iota(jnp.int32, x.shape, 0)
    return jnp.where(row >= k, pltpu.roll(x, k, 0), fill)


def _lru_kernel(xg_ref, cw_ref, cb_ref, wa_ref, ba_ref, wx_ref, bx_ref, lam_ref, o_ref, h_ref):
    x = xg_ref[0, 0]
    gate = xg_ref[1, 0]
    s = x.shape[0]
    kw = cw_ref.shape[0]
    xc = cb_ref[...] + cw_ref[kw - 1:kw, :] * x
    for j in range(kw - 1):
        xc = xc + cw_ref[j:j + 1, :] * _shift_rows(x, kw - 1 - j, 0.0)
    xcb = xc.astype(BF16)
    r = jax.nn.sigmoid(_dot(xcb, wa_ref[...]) + ba_ref[...])
    i_gate = jax.nn.sigmoid(_dot(xcb, wx_ref[...]) + bx_ref[...])
    a = jnp.exp(-LRU_C * r * _softplus(-lam_ref[...]))
    h = jnp.sqrt(1.0 - a * a) * (i_gate * xc)
    grouped = (s // SUBLANES, SUBLANES, x.shape[1])
    h, a = h.reshape(grouped), a.reshape(grouped)
    in_group = lax.broadcasted_iota(jnp.int32, grouped, 1)
    k = 1
    while k < SUBLANES:
        prev = in_group >= k
        h = a * jnp.where(prev, pltpu.roll(h, k, 1), 0.0) + h
        a = a * jnp.where(prev, pltpu.roll(a, k, 1), 1.0)
        k *= 2
    h, a = h.reshape(x.shape), a.reshape(x.shape)
    carry = jnp.zeros((1, x.shape[1]), F32)
    for c in range(s // SUBLANES):
        rows = slice(c * SUBLANES, (c + 1) * SUBLANES)
        hc = h[rows] + a[rows] * carry
        h_ref[rows] = hc
        carry = hc[SUBLANES - 1:SUBLANES]
    o_ref[0] = (h_ref[...] * _gelu(gate)).astype(o_ref.dtype)


def _lru(xg, conv_w, conv_b, wa_bd, b_a, wx_bd, b_x, lam):
    _, b, s, w = xg.shape
    full = lambda a: pl.BlockSpec(a.shape, lambda bi: (0,) * a.ndim)
    args = (conv_w, conv_b, wa_bd, b_a, wx_bd, b_x, lam)
    return pl.pallas_call(
        _lru_kernel,
        grid=(b,),
        in_specs=[pl.BlockSpec((2, 1, s, w), lambda bi: (0, bi, 0, 0))] + [full(a) for a in args],
        out_specs=pl.BlockSpec((1, s, w), lambda bi: (bi, 0, 0)),
        out_shape=jax.ShapeDtypeStruct((b, s, w), BF16),
        scratch_shapes=[pltpu.VMEM((s, w), F32)],
        compiler_params=_cparams("parallel"),
        name="rg_lru",
    )(xg, *args)


def _memkv_kernel(mem_ref, g_ref, wk_ref, wv_ref, k_ref, v_ref):
    mn = _rms(mem_ref[0], g_ref[...]).astype(BF16)
    k_ref[0] = _dot(mn, wk_ref[...]).astype(BF16)
    v_ref[0] = _dot(mn, wv_ref[...]).astype(BF16)


def _memkv(mem, g, w_ck, w_cv):
    b, m, d = mem.shape
    out = jax.ShapeDtypeStruct((b, m, MIX_W), BF16)
    return pl.pallas_call(
        _memkv_kernel,
        grid=(b,),
        in_specs=[
            pl.BlockSpec((1, m, d), lambda bi: (bi, 0, 0)),
            pl.BlockSpec((1, d), lambda bi: (0, 0)),
            pl.BlockSpec((d, MIX_W), lambda bi: (0, 0)),
            pl.BlockSpec((d, MIX_W), lambda bi: (0, 0)),
        ],
        out_specs=[pl.BlockSpec((1, m, MIX_W), lambda bi: (bi, 0, 0))] * 2,
        out_shape=[out, out],
        compiler_params=_cparams("parallel"),
        name="mem_kv",
    )(mem, g, w_ck, w_cv)


def _mix_cross_kernel(x_ref, sb_ref, fox_ref, lru_ref, o1_ref, l1_ref, o4_ref, l4_ref,
                      o16_ref, l16_ref, wo_ref, g_ref, wq_ref, km_ref, vm_ref, wco_ref, o_ref):
    t = MIX_SUB
    subs = [slice(u * t, (u + 1) * t) for u in range(x_ref.shape[1] // t)]
    xs = []
    for r in subs:
        dil = _dil_mixture([ref[0, r] for ref in (o1_ref, o4_ref, o16_ref)],
                           [ref[0, r] for ref in (l1_ref, l4_ref, l16_ref)]).astype(BF16)
        mixed = jnp.concatenate([sb_ref[0, r], fox_ref[0, r], dil, lru_ref[0, r]], axis=1)
        xs.append(x_ref[0, r] + _dot(mixed, wo_ref[...]))
    qs = [_stack_heads(_dot(_rms(x, g_ref[...]).astype(BF16), wq_ref[...])) for x in xs]
    scores = [_dot_nt(q, km_ref[0]) for q in qs]
    outs = []
    for sc in scores:
        p = jnp.exp(sc - jnp.max(sc, axis=-1, keepdims=True))
        p = p / jnp.sum(p, axis=-1, keepdims=True)
        outs.append(_unstack_heads(_dot(p.astype(BF16), vm_ref[0]), t))
    for r, x, o in zip(subs, xs, outs):
        o_ref[0, r] = x + _dot(o.astype(BF16), wco_ref[...])


MIX_SUB = 128


def _mix_cross(x, o_sb, o_fox, o_lru, dil_branches, w_out, g, w_cq, k_mem, v_mem, w_co, ts=1024):
    b, s, d = x.shape
    m = k_mem.shape[1]
    mix_spec = pl.BlockSpec((1, ts, MIX_W), lambda bi, i: (bi, i, 0))
    const = lambda a: pl.BlockSpec(a.shape, lambda bi, i: (0,) * a.ndim)
    mem_spec = pl.BlockSpec((1, m, MIX_W), lambda bi, i: (bi, 0, 0))
    dil_flat = [a for pair in dil_branches for a in pair]
    return pl.pallas_call(
        _mix_cross_kernel,
        grid=(b, s // ts),
        in_specs=[pl.BlockSpec((1, ts, d), lambda bi, i: (bi, i, 0))] + [mix_spec] * (3 + len(dil_flat))
                 + [const(w_out), const(g), const(w_cq), mem_spec, mem_spec, const(w_co)],
        out_specs=pl.BlockSpec((1, ts, d), lambda bi, i: (bi, i, 0)),
        out_shape=jax.ShapeDtypeStruct((b, s, d), F32),
        compiler_params=_cparams("parallel", "parallel"),
        name="mix_cross",
    )(x, o_sb, o_fox, o_lru, *dil_flat, w_out, g, w_cq, k_mem, v_mem, w_co)


HALO = 8
FFN_PROJ_LEAD = 2
FFN_EXT_SLOTS = FFN_PROJ_LEAD + 1
FFN_DOWN_LAG = 2
FFN_ACT_SLOTS = FFN_DOWN_LAG + 1


def _ffn_kernel(x_ref, g_ref, wup_ref, cw_ref, cb_ref, wdn_ref, gf_ref, o_ref,
                hn_ref, ext_ref, tail_ref, act_ref, acc_ref, *, final_norm):
    i = pl.program_id(1)
    x = x_ref[0]
    t = x.shape[0]
    hn_ref[...] = _rms(x, g_ref[...]).astype(BF16)

    @pl.when(i == 0)
    def _():
        tail_ref[...] = jnp.zeros_like(tail_ref)

    def project(c):
        for half in range(2):
            blk = half * N_FF_CHUNKS + c
            slot = 2 * (c % FFN_EXT_SLOTS) + half
            pre = _dot(hn_ref[...], wup_ref[:, blk * FF_CHUNK:(blk + 1) * FF_CHUNK])
            ext_ref[slot, 0:HALO] = tail_ref[blk]
            ext_ref[slot, HALO:] = pre
            tail_ref[blk] = pre[t - HALO:]

    def conv(c, slot):
        ext = ext_ref.at[slot]
        w = cw_ref[c]
        return (cb_ref[c] + w[2:3] * ext[pl.ds(HALO, t), :]
                + w[1:2] * ext[pl.ds(HALO - 1, t), :]
                + w[0:1] * ext[pl.ds(HALO - 2, t), :])

    def down(c):
        part = _dot(act_ref[c % FFN_ACT_SLOTS], wdn_ref[c])
        acc_ref[...] = part if c == 0 else acc_ref[...] + part

    for c in range(FFN_PROJ_LEAD):
        project(c)
    for c in range(N_FF_CHUNKS):
        if c + FFN_PROJ_LEAD < N_FF_CHUNKS:
            project(c + FFN_PROJ_LEAD)
        if c >= FFN_DOWN_LAG:
            down(c - FFN_DOWN_LAG)
        up = conv(c, 2 * (c % FFN_EXT_SLOTS))
        gate = conv(N_FF_CHUNKS + c, 2 * (c % FFN_EXT_SLOTS) + 1)
        act_ref[c % FFN_ACT_SLOTS] = (_gelu(gate) * up).astype(BF16)
    for c in range(N_FF_CHUNKS - FFN_DOWN_LAG, N_FF_CHUNKS):
        down(c)
    out = x + acc_ref[...]
    if final_norm:
        out = _rms(out, gf_ref[...])
    o_ref[0] = out


def _ffn(x, g, w_up, conv_w, conv_b, w_down, g_final, final_norm, ts=256):
    b, s, d = x.shape
    const = lambda a: pl.BlockSpec(a.shape, lambda bi, i: (0,) * a.ndim)
    return pl.pallas_call(
        functools.partial(_ffn_kernel, final_norm=final_norm),
        grid=(b, s // ts),
        in_specs=[
            pl.BlockSpec((1, ts, d), lambda bi, i: (bi, i, 0)),
            const(g), const(w_up), const(conv_w), const(conv_b), const(w_down), const(g_final),
        ],
        out_specs=pl.BlockSpec((1, ts, d), lambda bi, i: (bi, i, 0)),
        out_shape=jax.ShapeDtypeStruct((b, s, d), F32),
        scratch_shapes=[
            pltpu.VMEM((ts, d), BF16),
            pltpu.VMEM((2 * FFN_EXT_SLOTS, HALO + ts, FF_CHUNK), F32),
            pltpu.VMEM((2 * N_FF_CHUNKS, HALO, FF_CHUNK), F32),
            pltpu.VMEM((FFN_ACT_SLOTS, ts, FF_CHUNK), BF16),
            pltpu.VMEM((ts, d), F32),
        ],
        compiler_params=_cparams("parallel", "arbitrary"),
        name="conv_geglu",
    )(x, g, w_up, conv_w, conv_b, w_down, g_final)


def _block_diag(w):
    g, n, _ = w.shape
    eye = jnp.eye(g, dtype=w.dtype)
    return (eye[:, None, :, None] * w[:, :, None, :]).reshape(g * n, g * n)


def kernel(x, mem, norm_mix_g, w_in, b_forget, lru_conv_w, lru_conv_b, lru_w_a, lru_b_a, lru_w_x, lru_b_x, lru_lambda, w_out, norm_cross_g, norm_mem_g, w_cq, w_ck, w_cv, w_co, norm_ffn_g, w_up, ffn_conv_w, ffn_conv_b, w_down, rel_bias, final_norm_g):
    depth = w_in.shape[0]
    qkv_w = 3 * MIX_W
    q_scale = HEAD_DIM ** -0.5
    col_scale = jnp.ones((3, 3, MIX_W), F32).at[:, 0, :].set(q_scale).reshape(-1)
    col_scale = jnp.concatenate([col_scale, jnp.ones((2 * MIX_W,), F32)])
    tables = _bias_tables(rel_bias)
    row = lambda v: v.reshape(1, -1)

    for l in range(depth):
        w = w_in[l]
        n_f = N_HEADS
        w_main = jnp.concatenate([w[:, :2 * qkv_w], w[:, 2 * qkv_w + n_f:]], axis=1)
        w_main = (w_main * col_scale).astype(BF16)
        w_f = jnp.zeros((8, D_MODEL), F32).at[:n_f].set(w[:, 2 * qkv_w:2 * qkv_w + n_f].T).astype(BF16)
        b_f = jnp.zeros((8, 1), F32).at[:n_f, 0].set(b_forget[l])

        sb_qkv, fox_qkv, *dil_qkvs, lru_xg, f_t = _inproj(x, row(norm_mix_g[l]), w_main, w_f)
        o_sb = _sb_attention(sb_qkv)
        o_fox = _fox_attention(fox_qkv, _decay(f_t, b_f))
        dil_branches = [_dil_branch(qkv, tables, br) for br, qkv in enumerate(dil_qkvs)]
        o_lru = _lru(lru_xg, lru_conv_w[l], row(lru_conv_b[l]),
                     _block_diag(lru_w_a[l]).astype(BF16), row(lru_b_a[l]),
                     _block_diag(lru_w_x[l]).astype(BF16), row(lru_b_x[l]), row(lru_lambda[l]))

        k_mem, v_mem = _memkv(mem, row(norm_mem_g[l]), w_ck[l].astype(BF16), w_cv[l].astype(BF16))
        x = _mix_cross(x, o_sb, o_fox, o_lru, dil_branches,
                       w_out[l].astype(BF16), row(norm_cross_g[l]),
                       (w_cq[l] * q_scale).astype(BF16), k_mem, v_mem, w_co[l].astype(BF16))

        wu = w_up[l].astype(BF16)
        cw = ffn_conv_w[l].reshape(-1, 2 * N_FF_CHUNKS, FF_CHUNK).transpose(1, 0, 2)
        cb = ffn_conv_b[l].reshape(2 * N_FF_CHUNKS, 1, FF_CHUNK)
        wd = w_down[l].reshape(N_FF_CHUNKS, FF_CHUNK, D_MODEL).astype(BF16)
        x = _ffn(x, row(norm_ffn_g[l]), wu, cw, cb, wd, row(final_norm_g), l == depth - 1)
    return x
```

```python
import functools
import math

import jax
import jax.numpy as jnp
from jax import lax
from jax.experimental import pallas as pl
from jax.experimental.pallas import tpu as pltpu

F32 = jnp.float32
BF16 = jnp.bfloat16

LANES = 128
SUBLANES = 8
D_MODEL = 1024
HEAD_DIM = 64
N_HEADS = 4
MIX_W = N_HEADS * HEAD_DIM
D_FF = 2816
FF_CHUNK = 256
N_FF_CHUNKS = D_FF // FF_CHUNK
N_MEM = 256
NUM_BUCKETS = 32
MAX_DISTANCE = 2048
DILATIONS = (1, 4, 16)
DIL_STEPS = 128
LRU_C = 8.0
EPS = 1e-6
NEG_BIG = -1e30

TQ = 128
TQC = 256
TK = 256
VMEM_LIMIT = 56 * 1024 * 1024

NT_DIMS = (((1,), (1,)), ((), ()))


def _cparams(*sem):
    return pltpu.CompilerParams(dimension_semantics=sem, vmem_limit_bytes=VMEM_LIMIT)


def _rms(x, g):
    return x * lax.rsqrt(jnp.mean(x * x, axis=-1, keepdims=True) + EPS) * g


def _softplus(x):
    return jnp.maximum(x, 0.0) + jnp.log(1.0 + jnp.exp(-jnp.abs(x)))


def _gelu(x):
    return 0.5 * x * (1.0 + lax.erf(x * math.sqrt(0.5)))


def _dot(a, b):
    return jnp.dot(a, b, preferred_element_type=F32)


def _dot_nt(a, b):
    return lax.dot_general(a, b, NT_DIMS, preferred_element_type=F32)


def _head_of_lane(rows):
    return lax.broadcasted_iota(jnp.int32, (rows, MIX_W), 1) // HEAD_DIM


def _stack_heads(q):
    t = q.shape[0]
    head = _head_of_lane(t)
    qf = q.astype(F32)
    return jnp.concatenate([jnp.where(head == h, qf, 0.0) for h in range(N_HEADS)], axis=0).astype(BF16)


def _unstack_heads(o, t):
    head = _head_of_lane(t)
    out = jnp.where(head == 0, o[0:t], 0.0)
    for h in range(1, N_HEADS):
        out = jnp.where(head == h, o[h * t:(h + 1) * t], out)
    return out


def _unstack_cols(c, t):
    head = _head_of_lane(t)
    out = jnp.broadcast_to(c[0:t], (t, MIX_W))
    for h in range(1, N_HEADS):
        out = jnp.where(head == h, jnp.broadcast_to(c[h * t:(h + 1) * t], (t, MIX_W)), out)
    return out


def _inproj_kernel(x_ref, g_ref, w_ref, wf_ref, sb_ref, fox_ref, d1_ref, d4_ref, d16_ref,
                   lru_ref, f_ref, hn_ref, half_ref):
    hn_ref[...] = _rms(x_ref[0], g_ref[...]).astype(BF16)
    ts = hn_ref.shape[0]

    def project(rows, block):
        return _dot(hn_ref[rows], w_ref[:, block * MIX_W:(block + 1) * MIX_W])

    for j in range(3):
        res = project(slice(None), 6 + j)
        d1_ref[j, 0, 0] = res.astype(BF16)
        for h in range(2):
            half_ref[j, h] = res[:, h * LANES:(h + 1) * LANES]

    def regroup(j, ref, dil):
        for r in range(dil):
            for h in range(2):
                ref[j, 0, r, :, h * LANES:(h + 1) * LANES] = (
                    half_ref[j, h, pl.ds(r, ts // dil, stride=dil), :].astype(BF16))

    pending = [(j, ref, dil) for j in range(3)
               for ref, dil in ((d4_ref, DILATIONS[1]), (d16_ref, DILATIONS[2]))]
    for m, ref in enumerate((sb_ref, fox_ref)):
        for j in range(3):
            ref[j, 0] = project(slice(None), 3 * m + j).astype(BF16)
            regroup(*pending.pop(0))
    lru_ref[0, 0] = project(slice(None), 9)
    for rows in (slice(0, ts // 2), slice(ts // 2, ts)):
        lru_ref[1, 0, rows] = project(rows, 10)
    f_ref[0] = _dot_nt(wf_ref[...], hn_ref[...])


def _inproj(x, g, w_main, w_f, ts=1024):
    b, s, d = x.shape
    qkv_shape = jax.ShapeDtypeStruct((3, b, s, MIX_W), BF16)
    qkv_spec = pl.BlockSpec((3, 1, ts, MIX_W), lambda bi, i: (0, bi, i, 0))
    cls_shapes = [jax.ShapeDtypeStruct((3, b, dil, s // dil, MIX_W), BF16) for dil in DILATIONS]
    cls_specs = [pl.BlockSpec((3, 1, dil, ts // dil, MIX_W), lambda bi, i: (0, bi, 0, i, 0))
                 for dil in DILATIONS]
    return pl.pallas_call(
        _inproj_kernel,
        grid=(b, s // ts),
        in_specs=[
            pl.BlockSpec((1, ts, d), lambda bi, i: (bi, i, 0)),
            pl.BlockSpec((1, d), lambda bi, i: (0, 0)),
            pl.BlockSpec(w_main.shape, lambda bi, i: (0, 0)),
            pl.BlockSpec(w_f.shape, lambda bi, i: (0, 0)),
        ],
        out_specs=[
            qkv_spec, qkv_spec, *cls_specs,
            pl.BlockSpec((2, 1, ts, MIX_W), lambda bi, i: (0, bi, i, 0)),
            pl.BlockSpec((1, 8, ts), lambda bi, i: (bi, 0, i)),
        ],
        out_shape=[
            qkv_shape, qkv_shape, *cls_shapes,
            jax.ShapeDtypeStruct((2, b, s, MIX_W), F32),
            jax.ShapeDtypeStruct((b, 8, s), F32),
        ],
        scratch_shapes=[pltpu.VMEM((ts, d), BF16), pltpu.VMEM((3, 2, ts, LANES), F32)],
        compiler_params=_cparams("parallel", "parallel"),
        name="inproj",
    )(x, g, w_main, w_f)


def _decay_kernel(f_ref, bf_ref, tri_ref, o_ref):
    s = f_ref.shape[2]
    tri = tri_ref[...]
    carry = jnp.zeros((8, 1), F32)
    for c in range(s // 128):
        sl = slice(c * 128, (c + 1) * 128)
        log_f = -_softplus(-(f_ref[0, :, sl] + bf_ref[...]))
        p1 = log_f.astype(BF16)
        r1 = log_f - p1.astype(F32)
        p2 = r1.astype(BF16)
        p3 = (r1 - p2.astype(F32)).astype(BF16)
        inc = _dot(p1, tri) + _dot(p2, tri) + _dot(p3, tri)
        o_ref[0, :, sl] = inc + carry
        carry = carry + jnp.sum(log_f, axis=-1, keepdims=True)


def _decay(f_t, b_f):
    b, _, s = f_t.shape
    tri = (lax.broadcasted_iota(jnp.int32, (128, 128), 0)
           <= lax.broadcasted_iota(jnp.int32, (128, 128), 1)).astype(BF16)
    return pl.pallas_call(
        _decay_kernel,
        grid=(b,),
        in_specs=[
            pl.BlockSpec((1, 8, s), lambda bi: (bi, 0, 0)),
            pl.BlockSpec((8, 1), lambda bi: (0, 0)),
            pl.BlockSpec((128, 128), lambda bi: (0, 0)),
        ],
        out_specs=pl.BlockSpec((1, 8, s), lambda bi: (bi, 0, 0)),
        out_shape=jax.ShapeDtypeStruct((b, 8, s), F32),
        compiler_params=_cparams("parallel"),
        name="fox_decay",
    )(f_t, b_f, tri)


def _block_positions(i, ks):
    row = lax.broadcasted_iota(jnp.int32, (N_HEADS * TQC, TK), 0) % TQC + i * TQC
    col = lax.broadcasted_iota(jnp.int32, (N_HEADS * TQC, TK), 1) + ks
    return row, col


SB_SATURATED = 110.0


def _sb_kernel(q_ref, k_ref, v_ref, uo_ref, o_ref, acc_ref, car_ref):
    i = pl.program_id(1)
    qs = _stack_heads(q_ref[0, 0])

    half_rows = N_HEADS * TQC // 2
    halves = [slice(0, half_rows), slice(half_rows, 2 * half_rows)]

    def process(blocks):
        items = []
        for kb, diagonal in blocks:
            ks = pl.multiple_of(kb * TK, TK)
            k_blk = k_ref[0, 0, pl.ds(ks, TK), :]
            v_blk = v_ref[0, 0, pl.ds(ks, TK), :]
            strict = None
            if diagonal:
                row, col = _block_positions(i, ks)
                strict = (col < row)[:half_rows]
            items += [(r, k_blk, v_blk, strict) for r in halves]
        zs = [_dot_nt(qs[r], k_blk) for r, k_blk, _, _ in items]
        staged = []
        for (r, _, _, strict), z in zip(items, zs):
            drop = _softplus(z)
            cum_in = drop if strict is None else jnp.where(strict, drop, 0.0)
            staged.append((drop, _dot(cum_in.astype(BF16), uo_ref[...])))
        for (r, _, v_blk, strict), z, (drop, sm) in zip(items, zs, staged):
            if strict is not None:
                att = jnp.where(strict, jnp.exp(z - drop - sm[:, :TK]), 0.0)
                car_ref[r] = sm[:, TK:]
                acc_ref[r] = _dot(att.astype(BF16), v_blk)
            else:
                car = car_ref[r]
                att = jnp.exp(z - drop - sm[:, :TK] - jnp.concatenate([car] * (TK // LANES), axis=1))
                car_ref[r] = car + sm[:, TK:]
                acc_ref[r] += _dot(att.astype(BF16), v_blk)

    def live():
        return jnp.min(car_ref[...]) < SB_SATURATED

    pl.when(i == 0)(lambda: process([(i, True)]))
    pl.when(i > 0)(lambda: process([(i, True), (i - 1, False)]))

    def cond(state):
        kb, go = state
        return jnp.logical_and(kb >= 0, go)

    def body(state):
        kb, _ = state
        process([(kb, False)])
        return kb - 1, live()

    lax.while_loop(cond, body, (i - 2, live()))
    o_ref[0] = _unstack_heads(acc_ref[...], TQC).astype(o_ref.dtype)


def _sb_attention(qkv):
    _, b, s, _ = qkv.shape
    assert TQC == TK
    later =(lax.broadcasted_iota(jnp.int32, (TK, TK), 0)
             > lax.broadcasted_iota(jnp.int32, (TK, TK), 1))
    u = jnp.concatenate([later, jnp.ones((TK, LANES), bool)], axis=1).astype(BF16)
    return pl.pallas_call(
        _sb_kernel,
        grid=(b, s // TQC),
        in_specs=[
            pl.BlockSpec((1, 1, TQC, MIX_W), lambda bi, i: (0, bi, i, 0)),
            pl.BlockSpec((1, 1, s, MIX_W), lambda bi, i: (1, bi, 0, 0)),
            pl.BlockSpec((1, 1, s, MIX_W), lambda bi, i: (2, bi, 0, 0)),
            pl.BlockSpec((TK, TK + LANES), lambda bi, i: (0, 0)),
        ],
        out_specs=pl.BlockSpec((1, TQC, MIX_W), lambda bi, i: (bi, i, 0)),
        out_shape=jax.ShapeDtypeStruct((b, s, MIX_W), BF16),
        scratch_shapes=[
            pltpu.VMEM((N_HEADS * TQC, MIX_W), F32),
            pltpu.VMEM((N_HEADS * TQC, LANES), F32),
        ],
        compiler_params=_cparams("parallel", "arbitrary"),
        name="sb_attn",
    )(qkv, qkv, qkv, u)


def _fox_kernel(q_ref, k_ref, v_ref, cum_ref, o_ref, za_ref, zb_ref, ma_ref, mb_ref, l_ref, acc_ref):
    i = pl.program_id(1)
    n_q = pl.num_programs(1) - 1

    def front_block(qs, kb, diagonal, z_ref, mx_ref):
        ks = pl.multiple_of(kb * TK, TK)
        cum = cum_ref[0, :, pl.ds(ks, TK)]
        decay = jnp.concatenate(
            [jnp.broadcast_to(cum[h:h + 1], (TQC, TK)) for h in range(N_HEADS)], axis=0)
        z = _dot_nt(qs, k_ref[0, 0, pl.ds(ks, TK), :]) - decay
        if diagonal:
            row, col = _block_positions(i, ks)
            z = jnp.where(col <= row, z, NEG_BIG)
        z_ref[kb] = z
        zmax = jnp.maximum(z[:, :LANES], z[:, LANES:])
        mx_ref[...] = zmax if diagonal else jnp.maximum(mx_ref[...], zmax)

    def back_block(kb, z_ref, mx_ref):
        ks = pl.multiple_of(kb * TK, TK)
        m = mx_ref[...]
        p = jnp.exp(z_ref[kb] - jnp.concatenate([m, m], axis=1))
        l_ref[...] += p[:, :LANES] + p[:, LANES:]
        acc_ref[...] += _dot(p.astype(BF16), v_ref[0, 0, pl.ds(ks, TK), :])

    def step(z_cur, m_cur, z_prev, m_prev):
        @pl.when(i < n_q)
        def _():
            qs = _stack_heads(q_ref[0, 0])
            front_block(qs, i, True, z_cur, m_cur)

            def both(kb):
                front_block(qs, kb, False, z_cur, m_cur)
                back_block(kb, z_prev, m_prev)

            def pair(j, carry):
                both(2 * j)
                both(2 * j + 1)
                return carry

            lax.fori_loop(0, i // 2, pair, 0)
            pl.when(i % 2 == 1)(lambda: both(i - 1))
            m = jnp.max(m_cur[...], axis=-1, keepdims=True)
            m_cur[...] = jnp.broadcast_to(m, m_cur.shape)

        @pl.when(i == n_q)
        def _():
            def back_only(kb, carry):
                back_block(kb, z_prev, m_prev)
                return carry

            lax.fori_loop(0, i, back_only, 0)

    l_ref[...] = jnp.zeros_like(l_ref)
    acc_ref[...] = jnp.zeros_like(acc_ref)
    pl.when(i % 2 == 0)(functools.partial(step, za_ref, ma_ref, zb_ref, mb_ref))
    pl.when(i % 2 == 1)(functools.partial(step, zb_ref, mb_ref, za_ref, ma_ref))

    @pl.when(i > 0)
    def _():
        l = _unstack_cols(jnp.sum(l_ref[...], axis=-1, keepdims=True), TQC)
        o_ref[0] = (_unstack_heads(acc_ref[...], TQC) / l).astype(o_ref.dtype)


def _fox_attention(qkv, cum):
    _, b, s, _ = qkv.shape
    assert TQC == TK
    n_q = s // TQC
    rows = N_HEADS * TQC
    return pl.pallas_call(
        _fox_kernel,
        grid=(b, n_q + 1),
        in_specs=[
            pl.BlockSpec((1, 1, TQC, MIX_W), lambda bi, i: (0, bi, jnp.minimum(i, n_q - 1), 0)),
            pl.BlockSpec((1, 1, s, MIX_W), lambda bi, i: (1, bi, 0, 0)),
            pl.BlockSpec((1, 1, s, MIX_W), lambda bi, i: (2, bi, 0, 0)),
            pl.BlockSpec((1, 8, s), lambda bi, i: (bi, 0, 0)),
        ],
        out_specs=pl.BlockSpec((1, TQC, MIX_W), lambda bi, i: (bi, jnp.maximum(i - 1, 0), 0)),
        out_shape=jax.ShapeDtypeStruct((b, s, MIX_W), BF16),
        scratch_shapes=[
            pltpu.VMEM((s // TK, rows, TK), F32),
            pltpu.VMEM((s // TK, rows, TK), F32),
            pltpu.VMEM((rows, LANES), F32),
            pltpu.VMEM((rows, LANES), F32),
            pltpu.VMEM((rows, LANES), F32),
            pltpu.VMEM((rows, MIX_W), F32),
        ],
        compiler_params=_cparams("parallel", "arbitrary"),
        name="fox_attn",
    )(qkv, qkv, qkv, cum)


def _t5_bucket(dist):
    n = jnp.maximum(dist, 0)
    max_exact = NUM_BUCKETS // 2
    nf = jnp.maximum(n, 1).astype(F32)
    large = max_exact + jnp.floor(jnp.log(nf / max_exact) / math.log(MAX_DISTANCE / max_exact)
                                  * (NUM_BUCKETS - max_exact)).astype(jnp.int32)
    large = jnp.minimum(large, NUM_BUCKETS - 1)
    return jnp.where(n < max_exact, n, large)


def _bias_kernel(idx_ref, rb_ref, o_ref):
    for d in range(len(DILATIONS)):
        for variant in range(2):
            idx = idx_ref[d, variant]
            for h in range(N_HEADS):
                acc = jnp.full(idx.shape, NEG_BIG, F32)
                for bucket in range(NUM_BUCKETS):
                    acc = jnp.where(idx == bucket, rb_ref[bucket, h], acc)
                o_ref[d, variant, h] = acc


def _bias_tables(rel_bias):
    t = jnp.arange(TQ)[:, None]
    c = jnp.arange(2 * TQ)[None, :]
    delta = TQ + t - c
    valid = (delta >= 0) & (delta <= DIL_STEPS)
    idx = jnp.stack([jnp.where(valid, _t5_bucket(delta * dil), -1) for dil in DILATIONS])
    idx_first = jnp.concatenate([idx[:, :, TQ:], jnp.full_like(idx[:, :, TQ:], -1)], axis=-1)
    idx = jnp.stack([idx, idx_first], axis=1)
    out = pl.pallas_call(
        _bias_kernel,
        in_specs=[
            pl.BlockSpec(memory_space=pltpu.VMEM),
            pl.BlockSpec(memory_space=pltpu.SMEM),
        ],
        out_specs=pl.BlockSpec(memory_space=pltpu.VMEM),
        out_shape=jax.ShapeDtypeStruct((len(DILATIONS), 2, N_HEADS, TQ, 2 * TQ), F32),
        name="dil_bias",
    )(idx.astype(jnp.int32), rel_bias)
    return out.reshape(len(DILATIONS), 2, N_HEADS * TQ, 2 * TQ)


DIL_GROUP = 16


def _dil_kernel(q_ref, k_ref, v_ref, t_ref, o_ref, lse_ref, *, by_class, classes, group, n_groups):
    g = pl.program_id(2)

    def attend(c, jj, scores, v_win):
        m = jnp.max(scores, axis=-1, keepdims=True)
        p = jnp.exp(scores - m)
        l = _unstack_cols(jnp.sum(p, axis=-1, keepdims=True), TQ)
        o = (_unstack_heads(_dot(p.astype(BF16), v_win), TQ) / l).astype(o_ref.dtype)
        lse = _unstack_cols(m, TQ) + jnp.log(l)
        if by_class:
            o_ref[0, jj] = o
            lse_ref[0, jj] = lse
        else:
            o_ref[0, c, jj * TQ:(jj + 1) * TQ] = o
            lse_ref[0, c, jj * TQ:(jj + 1) * TQ] = lse

    pending = []
    for c, jj in [(c, jj) for c in range(classes) for jj in range(group)]:
        if by_class:
            scores = (_dot_nt(_stack_heads(q_ref[0, 0, jj]), k_ref[0, 0, jj])
                      + t_ref[0, 0, :, TQ:2 * TQ])
            pending.append((c, jj, scores, v_ref[0, 0, jj]))
            continue
        qs = _stack_heads(q_ref[0, 0, c, jj * TQ:(jj + 1) * TQ])
        if n_groups == 1 or jj > 0:
            n = g * group + jj if n_groups > 1 else jj
            start = (n - 1) * TQ if jj > 0 else 0
            bias = t_ref[0, 0 if jj > 0 else 1]
        else:
            start = jnp.maximum(g * group - 1, 0) * TQ
            bias = t_ref[0, (g == 0).astype(jnp.int32)]
        if not isinstance(start, int):
            start = pl.multiple_of(start, TQ)
        window = pl.ds(start, 2 * TQ)
        pending.append((c, jj, _dot_nt(qs, k_ref[0, 0, c, window, :]) + bias,
                        v_ref[0, 0, c, window, :]))
    for item in pending:
        attend(*item)


def _dil_branch(qkv, tables, branch):
    _, b, dil, cls_len, _ = qkv.shape
    n_blocks = cls_len // TQ
    by_class = n_blocks == 1
    group = min(DIL_GROUP, dil if by_class else n_blocks)
    classes = 1 if by_class else min(dil, DIL_GROUP // group)
    if by_class:
        grid = (b, dil // group, 1)
        qkv_specs = [pl.BlockSpec((1, 1, group, TQ, MIX_W), functools.partial(
            lambda bi, c, g, j: (j, bi, c, 0, 0), j=j)) for j in range(3)]
        out_spec = pl.BlockSpec((1, group, TQ, MIX_W), lambda bi, c, g: (bi, c, 0, 0))
    else:
        grid = (b, dil // classes, n_blocks // group)
        qkv_specs = [pl.BlockSpec((1, 1, classes, group * TQ, MIX_W),
                                  lambda bi, r, g: (0, bi, r, g, 0))]
        qkv_specs += [pl.BlockSpec((1, 1, classes, cls_len, MIX_W), functools.partial(
            lambda bi, r, g, j: (j, bi, r, 0, 0), j=j)) for j in (1, 2)]
        out_spec = pl.BlockSpec((1, classes, group * TQ, MIX_W), lambda bi, r, g: (bi, r, g, 0))
    o, lse = pl.pallas_call(
        functools.partial(_dil_kernel, by_class=by_class, classes=classes, group=group,
                          n_groups=grid[2]),
        grid=grid,
        in_specs=qkv_specs + [
            pl.BlockSpec((1, 2, N_HEADS * TQ, 2 * TQ), lambda bi, r, g: (branch, 0, 0, 0))],
        out_specs=[out_spec, out_spec],
        out_shape=[jax.ShapeDtypeStruct((b, dil, cls_len, MIX_W), BF16),
                   jax.ShapeDtypeStruct((b, dil, cls_len, MIX_W), F32)],
        compiler_params=_cparams("parallel", "parallel", "arbitrary"),
        name=f"dil_attn_{dil}",
    )(qkv, qkv, qkv, tables)
    to_seq = lambda a: a.transpose(0, 2, 1, 3).reshape(b, dil * cls_len, MIX_W)
    return to_seq(o), to_seq(lse)


def _dil_mixture(outs, lses):
    m = functools.reduce(jnp.maximum, lses)
    es = [jnp.exp(a - m) for a in lses]
    tot = functools.reduce(jnp.add, es)
    return functools.reduce(jnp.add, [(e / tot) * o for e, o in zip(es, outs)])


def _shift_rows(x, k, fill):
    row = lax.broadcasted_iota(jnp.int32, x.shape, 0)
    return jnp.where(row >= k, pltpu.roll(x, k, 0), fill)


def _lru_kernel(xg_ref, cw_ref, cb_ref, wa_ref, ba_ref, wx_ref, bx_ref, lam_ref, o_ref, h_ref):
    x = xg_ref[0, 0]
    gate = xg_ref[1, 0]
    s = x.shape[0]
    kw = cw_ref.shape[0]
    xc = cb_ref[...] + cw_ref[kw - 1:kw, :] * x
    for j in range(kw - 1):
        xc = xc + cw_ref[j:j + 1, :] * _shift_rows(x, kw - 1 - j, 0.0)
    xcb = xc.astype(BF16)
    r = jax.nn.sigmoid(_dot(xcb, wa_ref[...]) + ba_ref[...])
    i_gate = jax.nn.sigmoid(_dot(xcb, wx_ref[...]) + bx_ref[...])
    a = jnp.exp(-LRU_C * r * _softplus(-lam_ref[...]))
    h = jnp.sqrt(1.0 - a * a) * (i_gate * xc)
    grouped = (s // SUBLANES, SUBLANES, x.shape[1])
    h, a = h.reshape(grouped), a.reshape(grouped)
    in_group = lax.broadcasted_iota(jnp.int32, grouped, 1)
    k = 1
    while k < SUBLANES:
        prev = in_group >= k
        h = a * jnp.where(prev, pltpu.roll(h, k, 1), 0.0) + h
        a = a * jnp.where(prev, pltpu.roll(a, k, 1), 1.0)
        k *= 2
    h, a = h.reshape(x.shape), a.reshape(x.shape)
    carry = jnp.zeros((1, x.shape[1]), F32)
    for c in range(s // SUBLANES):
        rows = slice(c * SUBLANES, (c + 1) * SUBLANES)
        hc = h[rows] + a[rows] * carry
        h_ref[rows] = hc
        carry = hc[SUBLANES - 1:SUBLANES]
    o_ref[0] = (h_ref[...] * _gelu(gate)).astype(o_ref.dtype)


def _lru(xg, conv_w, conv_b, wa_bd, b_a, wx_bd, b_x, lam):
    _, b, s, w = xg.shape
    full = lambda a: pl.BlockSpec(a.shape, lambda bi: (0,) * a.ndim)
    args = (conv_w, conv_b, wa_bd, b_a, wx_bd, b_x, lam)
    return pl.pallas_call(
        _lru_kernel,
        grid=(b,),
        in_specs=[pl.BlockSpec((2, 1, s, w), lambda bi: (0, bi, 0, 0))] + [full(a) for a in args],
        out_specs=pl.BlockSpec((1, s, w), lambda bi: (bi, 0, 0)),
        out_shape=jax.ShapeDtypeStruct((b, s, w), BF16),
        scratch_shapes=[pltpu.VMEM((s, w), F32)],
        compiler_params=_cparams("parallel"),
        name="rg_lru",
    )(xg, *args)


def _memkv_kernel(mem_ref, g_ref, wk_ref, wv_ref, k_ref, v_ref):
    mn = _rms(mem_ref[0], g_ref[...]).astype(BF16)
    k_ref[0] = _dot(mn, wk_ref[...]).astype(BF16)
    v_ref[0] = _dot(mn, wv_ref[...]).astype(BF16)


def _memkv(mem, g, w_ck, w_cv):
    b, m, d = mem.shape
    out = jax.ShapeDtypeStruct((b, m, MIX_W), BF16)
    return pl.pallas_call(
        _memkv_kernel,
        grid=(b,),
        in_specs=[
            pl.BlockSpec((1, m, d), lambda bi: (bi, 0, 0)),
            pl.BlockSpec((1, d), lambda bi: (0, 0)),
            pl.BlockSpec((d, MIX_W), lambda bi: (0, 0)),
            pl.BlockSpec((d, MIX_W), lambda bi: (0, 0)),
        ],
        out_specs=[pl.BlockSpec((1, m, MIX_W), lambda bi: (bi, 0, 0))] * 2,
        out_shape=[out, out],
        compiler_params=_cparams("parallel"),
        name="mem_kv",
    )(mem, g, w_ck, w_cv)


def _mix_cross_kernel(x_ref, sb_ref, fox_ref, lru_ref, o1_ref, l1_ref, o4_ref, l4_ref,
                      o16_ref, l16_ref, wo_ref, g_ref, wq_ref, km_ref, vm_ref, wco_ref, o_ref):
    t = MIX_SUB
    subs = [slice(u * t, (u + 1) * t) for u in range(x_ref.shape[1] // t)]
    xs = []
    for r in subs:
        dil = _dil_mixture([ref[0, r] for ref in (o1_ref, o4_ref, o16_ref)],
                           [ref[0, r] for ref in (l1_ref, l4_ref, l16_ref)]).astype(BF16)
        mixed = jnp.concatenate([sb_ref[0, r], fox_ref[0, r], dil, lru_ref[0, r]], axis=1)
        xs.append(x_ref[0, r] + _dot(mixed, wo_ref[...]))
    qs = [_stack_heads(_dot(_rms(x, g_ref[...]).astype(BF16), wq_ref[...])) for x in xs]
    scores = [_dot_nt(q, km_ref[0]) for q in qs]
    outs = []
    for sc in scores:
        p = jnp.exp(sc - jnp.max(sc, axis=-1, keepdims=True))
        l = _unstack_cols(jnp.sum(p, axis=-1, keepdims=True), t)
        outs.append(_unstack_heads(_dot(p.astype(BF16), vm_ref[0]), t) / l)
    for r, x, o in zip(subs, xs, outs):
        o_ref[0, r] = x + _dot(o.astype(BF16), wco_ref[...])


MIX_SUB = 128


def _mix_cross(x, o_sb, o_fox, o_lru, dil_branches, w_out, g, w_cq, k_mem, v_mem, w_co, ts=1024):
    b, s, d = x.shape
    m = k_mem.shape[1]
    mix_spec = pl.BlockSpec((1, ts, MIX_W), lambda bi, i: (bi, i, 0))
    const = lambda a: pl.BlockSpec(a.shape, lambda bi, i: (0,) * a.ndim)
    mem_spec = pl.BlockSpec((1, m, MIX_W), lambda bi, i: (bi, 0, 0))
    dil_flat = [a for pair in dil_branches for a in pair]
    return pl.pallas_call(
        _mix_cross_kernel,
        grid=(b, s // ts),
        in_specs=[pl.BlockSpec((1, ts, d), lambda bi, i: (bi, i, 0))] + [mix_spec] * (3 + len(dil_flat))
                 + [const(w_out), const(g), const(w_cq), mem_spec, mem_spec, const(w_co)],
        out_specs=pl.BlockSpec((1, ts, d), lambda bi, i: (bi, i, 0)),
        out_shape=jax.ShapeDtypeStruct((b, s, d), F32),
        compiler_params=_cparams("parallel", "parallel"),
        name="mix_cross",
    )(x, o_sb, o_fox, o_lru, *dil_flat, w_out, g, w_cq, k_mem, v_mem, w_co)


HALO = 8
FFN_PROJ_LEAD = 3
FFN_EXT_SLOTS = FFN_PROJ_LEAD + 1
FFN_DOWN_LAG = 2
FFN_ACT_SLOTS = FFN_DOWN_LAG + 1


def _ffn_kernel(x_ref, g_ref, wup_ref, cw_ref, cb_ref, wdn_ref, gf_ref, o_ref,
                hn_ref, ext_ref, tail_ref, act_ref, acc_ref, *, final_norm):
    i = pl.program_id(1)
    x = x_ref[0]
    t = x.shape[0]
    hn_ref[...] = _rms(x, g_ref[...]).astype(BF16)

    @pl.when(i == 0)
    def _():
        tail_ref[...] = jnp.zeros_like(tail_ref)

    def project(c):
        for half in range(2):
            blk = half * N_FF_CHUNKS + c
            slot = 2 * (c % FFN_EXT_SLOTS) + half
            pre = _dot(hn_ref[...], wup_ref[:, blk * FF_CHUNK:(blk + 1) * FF_CHUNK])
            ext_ref[slot, 0:HALO] = tail_ref[blk]
            ext_ref[slot, HALO:] = pre
            tail_ref[blk] = pre[t - HALO:]

    def conv(c, slot):
        ext = ext_ref.at[slot]
        w = cw_ref[c]
        return (cb_ref[c] + w[2:3] * ext[pl.ds(HALO, t), :]
                + w[1:2] * ext[pl.ds(HALO - 1, t), :]
                + w[0:1] * ext[pl.ds(HALO - 2, t), :])

    def down(c):
        part = _dot(act_ref[c % FFN_ACT_SLOTS], wdn_ref[c])
        acc_ref[...] = part if c == 0 else acc_ref[...] + part

    for c in range(FFN_PROJ_LEAD):
        project(c)
    for c in range(N_FF_CHUNKS):
        if c + FFN_PROJ_LEAD < N_FF_CHUNKS:
            project(c + FFN_PROJ_LEAD)
        if c >= FFN_DOWN_LAG:
            down(c - FFN_DOWN_LAG)
        up = conv(c, 2 * (c % FFN_EXT_SLOTS))
        gate = conv(N_FF_CHUNKS + c, 2 * (c % FFN_EXT_SLOTS) + 1)
        act_ref[c % FFN_ACT_SLOTS] = (_gelu(gate) * up).astype(BF16)
    for c in range(N_FF_CHUNKS - FFN_DOWN_LAG, N_FF_CHUNKS):
        down(c)
    out = x + acc_ref[...]
    if final_norm:
        out = _rms(out, gf_ref[...])
    o_ref[0] = out


def _ffn(x, g, w_up, conv_w, conv_b, w_down, g_final, final_norm, ts=256):
    b, s, d = x.shape
    const = lambda a: pl.BlockSpec(a.shape, lambda bi, i: (0,) * a.ndim)
    return pl.pallas_call(
        functools.partial(_ffn_kernel, final_norm=final_norm),
        grid=(b, s // ts),
        in_specs=[
            pl.BlockSpec((1, ts, d), lambda bi, i: (bi, i, 0)),
            const(g), const(w_up), const(conv_w), const(conv_b), const(w_down), const(g_final),
        ],
        out_specs=pl.BlockSpec((1, ts, d), lambda bi, i: (bi, i, 0)),
        out_shape=jax.ShapeDtypeStruct((b, s, d), F32),
        scratch_shapes=[
            pltpu.VMEM((ts, d), BF16),
            pltpu.VMEM((2 * FFN_EXT_SLOTS, HALO + ts, FF_CHUNK), F32),
            pltpu.VMEM((2 * N_FF_CHUNKS, HALO, FF_CHUNK), F32),
            pltpu.VMEM((FFN_ACT_SLOTS, ts, FF_CHUNK), BF16),
            pltpu.VMEM((ts, d), F32),
        ],
        compiler_params=_cparams("parallel", "arbitrary"),
        name="conv_geglu",
    )(x, g, w_up, conv_w, conv_b, w_down, g_final)


def _block_diag(w):
    g, n, _ = w.shape
    eye = jnp.eye(g, dtype=w.dtype)
    return (eye[:, None, :, None] * w[:, :, None, :]).reshape(g * n, g * n)


def kernel(x, mem, norm_mix_g, w_in, b_forget, lru_conv_w, lru_conv_b, lru_w_a, lru_b_a, lru_w_x, lru_b_x, lru_lambda, w_out, norm_cross_g, norm_mem_g, w_cq, w_ck, w_cv, w_co, norm_ffn_g, w_up, ffn_conv_w, ffn_conv_b, w_down, rel_bias, final_norm_g):
    depth = w_in.shape[0]
    qkv_w = 3 * MIX_W
    q_scale = HEAD_DIM ** -0.5
    col_scale = jnp.ones((3, 3, MIX_W), F32).at[:, 0, :].set(q_scale).reshape(-1)
    col_scale = jnp.concatenate([col_scale, jnp.ones((2 * MIX_W,), F32)])
    tables = _bias_tables(rel_bias)
    row = lambda v: v.reshape(1, -1)

    for l in range(depth):
        w = w_in[l]
        n_f = N_HEADS
        w_main = jnp.concatenate([w[:, :2 * qkv_w], w[:, 2 * qkv_w + n_f:]], axis=1)
        w_main = (w_main * col_scale).astype(BF16)
        w_f = jnp.zeros((8, D_MODEL), F32).at[:n_f].set(w[:, 2 * qkv_w:2 * qkv_w + n_f].T).astype(BF16)
        b_f = jnp.zeros((8, 1), F32).at[:n_f, 0].set(b_forget[l])

        sb_qkv, fox_qkv, *dil_qkvs, lru_xg, f_t = _inproj(x, row(norm_mix_g[l]), w_main, w_f)
        o_sb = _sb_attention(sb_qkv)
        o_fox = _fox_attention(fox_qkv, _decay(f_t, b_f))
        dil_branches = [_dil_branch(qkv, tables, br) for br, qkv in enumerate(dil_qkvs)]
        o_lru = _lru(lru_xg, lru_conv_w[l], row(lru_conv_b[l]),
                     _block_diag(lru_w_a[l]).astype(BF16), row(lru_b_a[l]),
                     _block_diag(lru_w_x[l]).astype(BF16), row(lru_b_x[l]), row(lru_lambda[l]))

        k_mem, v_mem = _memkv(mem, row(norm_mem_g[l]), w_ck[l].astype(BF16), w_cv[l].astype(BF16))
        x = _mix_cross(x, o_sb, o_fox, o_lru, dil_branches,
                       w_out[l].astype(BF16), row(norm_cross_g[l]),
                       (w_cq[l] * q_scale).astype(BF16), k_mem, v_mem, w_co[l].astype(BF16))

        wu = w_up[l].astype(BF16)
        cw = ffn_conv_w[l].reshape(-1, 2 * N_FF_CHUNKS, FF_CHUNK).transpose(1, 0, 2)
        cb = ffn_conv_b[l].reshape(2 * N_FF_CHUNKS, 1, FF_CHUNK)
        wd = w_down[l].reshape(N_FF_CHUNKS, FF_CHUNK, D_MODEL).astype(BF16)
        x = _ffn(x, row(norm_ffn_g[l]), wu, cw, cb, wd, row(final_norm_g), l == depth - 1)
    return x
```
